```python
import math
import jax
import jax.numpy as jnp
from jax import lax
import numpy as np

D_MODEL = 1024
BATCH = 8
SEQ = 4096
DEPTH = 2
DEC_BATCH = 32
DEC_SEQ = 8
PAST_LEN = 16384
PAGE_SIZE = 128

HEAD_DIM = 64
ROT_DIM = HEAD_DIM // 4
ROPE_THETA = 500000.0
PLE_DIM = 256
N_A_LAYERS = (DEPTH + 1) // 2
N_C_LAYERS = DEPTH // 2
HG_HEADS = 8
HG_DK = 64
HG_DV = 64
HG_WIDTH = HG_HEADS * HG_DV
HG_CHUNK = 64
CM_GROUPS = 4
CM_GW = 128
CM_WIDTH = CM_GROUPS * CM_GW
CM_CHUNK = 128
NSA_HEADS = 8
NSA_KV = 2
NSA_GRP = NSA_HEADS // NSA_KV
NSA_CMP_LEN = 32
NSA_CMP_STRIDE = 16
NSA_CMP_HIDDEN = 64
NSA_SEL_BLOCK = 64
NSA_TOPN = 16
NSA_WINDOW = 512
NSA_QBLK = 32
MOBA_HEADS = 8
MOBA_KV = 4
MOBA_GRP = MOBA_HEADS // MOBA_KV
MOBA_BLOCK = 256
MOBA_TOPK = 3
MOBA_QBLK = 16
MOE_GROUPS = 4
MOE_EPG = 8
MOE_EXPERTS = MOE_GROUPS * MOE_EPG
MOE_TOPK = 2
MOE_FF = 512
MOE_ROWBLK = 128
ROW_CMP_K = 0
ROW_CMP_V = ROW_CMP_K + NSA_KV
ROW_SEL_K = ROW_CMP_V + NSA_KV
ROW_SEL_V = ROW_SEL_K + NSA_KV
ROW_MOBA_K = ROW_SEL_V + NSA_KV
ROW_MOBA_V = ROW_MOBA_K + MOBA_KV
KV_ROW_HEADS = ROW_MOBA_V + MOBA_KV
A_SPLITS = [HG_WIDTH, HG_WIDTH, HG_WIDTH, HG_WIDTH, CM_WIDTH, CM_WIDTH]
C_SPLITS = [NSA_HEADS * HEAD_DIM] + [NSA_KV * HEAD_DIM] * 6 + [NSA_HEADS * 3, MOBA_HEADS * HEAD_DIM, MOBA_KV * HEAD_DIM, MOBA_KV * HEAD_DIM]
NEG_INF = -1e30
BIG = 1e9
f32 = jnp.float32

kernel_name = 'hgrn2_gmlp_nsa_moba_hmoe_decode_step'


def rmsnorm(x, g, eps=1e-6):
    x32 = x.astype(f32)
    y = x32 * lax.rsqrt(jnp.mean(x32 * x32, axis=-1, keepdims=True) + eps)
    return (y * g.astype(f32)).astype(x.dtype)


def masked_softmax(s, mask):
    s = jnp.where(mask, s.astype(f32), NEG_INF)
    e = jnp.where(mask, jnp.exp(s - jnp.max(s, axis=-1, keepdims=True)), 0.0)
    return e / jnp.maximum(jnp.sum(e, axis=-1, keepdims=True), 1e-30)


def partial_rope(x, pos):
    half = ROT_DIM // 2
    inv = ROPE_THETA ** (-jnp.arange(half, dtype=f32) / half)
    ang = pos.astype(f32)[:, None] * inv
    cos = jnp.cos(ang)[:, None, :]
    sin = jnp.sin(ang)[:, None, :]
    xr = x[..., :ROT_DIM].astype(f32)
    x1, x2 = xr[..., :half], xr[..., half:]
    rot = jnp.concatenate([x1 * cos - x2 * sin, x2 * cos + x1 * sin], axis=-1)
    return jnp.concatenate([rot.astype(x.dtype), x[..., ROT_DIM:]], axis=-1)


def hgrn2_recurrence(q, k, v, logf, s0):
    B, L, H, DK = q.shape
    DV = v.shape[-1]
    C = math.gcd(L, HG_CHUNK)
    n = L // C

    def to_chunks(t):
        return t.astype(f32).reshape(B, n, C, H, t.shape[-1]).transpose(1, 0, 3, 2, 4)

    tri = jnp.tril(jnp.ones((C, C), bool))[None, None, :, :, None]

    def step(S, inp):
        qc, kc, vc, gc = inp
        b = jnp.cumsum(gc, axis=2)
        decay = jnp.exp(jnp.where(tri, b[:, :, :, None, :] - b[:, :, None, :, :], -jnp.inf))
        att = jnp.einsum('bhtd,bhtsd,bhsd->bhts', qc, decay, kc)
        o = jnp.einsum('bhts,bhse->bhte', att, vc) + jnp.einsum('bhtd,bhde->bhte', qc * jnp.exp(b), S)
        b_end = b[:, :, -1:, :]
        S = jnp.exp(b_end[:, :, 0, :])[..., None] * S + jnp.einsum('bhsd,bhse->bhde', kc * jnp.exp(b_end - b), vc)
        return S, o

    S, o = lax.scan(step, s0.astype(f32), (to_chunks(q), to_chunks(k), to_chunks(v), to_chunks(logf)))
    return o.transpose(1, 0, 3, 2, 4).reshape(B, L, H, DV), S


def chunk_mix(v, ws, bs):
    B, L, G, CW = v.shape
    Lp = -(-L // CM_CHUNK) * CM_CHUNK
    vp = jnp.pad(v, ((0, 0), (0, Lp - L), (0, 0), (0, 0))).reshape(B, Lp // CM_CHUNK, CM_CHUNK, G, CW)
    w = jnp.where(jnp.tril(jnp.ones((CM_CHUNK, CM_CHUNK), bool))[None], ws, 0.0)
    out = jnp.einsum('gts,bnsgc->bntgc', w.astype(v.dtype), vp) + bs.T[None, None, :, :, None].astype(v.dtype)
    return out.reshape(B, Lp, G, CW)[:, :L]


def mixer_a(h, s0, a, prm):
    B, L, _ = h.shape
    z = h @ prm['w_in_a'][a]
    q, fl, iv, g, u, v = jnp.split(z, np.cumsum(A_SPLITS)[:-1].tolist(), axis=-1)
    q = jax.nn.silu(q).reshape(B, L, HG_HEADS, HG_DK)
    lb = jnp.cumsum(jax.nn.softmax(prm['hgrn_lb'].astype(f32), axis=0), axis=0)[a]
    f = lb + (1.0 - lb) * jax.nn.sigmoid(fl.astype(f32).reshape(B, L, HG_HEADS, HG_DK))
    o, s_new = hgrn2_recurrence(q, 1.0 - f, iv.reshape(B, L, HG_HEADS, HG_DV), jnp.log(f), s0)
    o = rmsnorm(o.astype(h.dtype), prm['hgrn_onorm'][a].reshape(HG_HEADS, HG_DV)).reshape(B, L, HG_WIDTH)
    o = o * jax.nn.silu(g)
    u = jax.nn.gelu(u)
    v = rmsnorm(jax.nn.gelu(v).reshape(B, L, CM_GROUPS, CM_GW), prm['cm_vnorm'][a].reshape(CM_GROUPS, CM_GW))
    c = u * chunk_mix(v, prm['cm_ws'][a], prm['cm_bs'][a]).reshape(B, L, CM_WIDTH)
    y = jnp.concatenate([o, c], axis=-1) @ prm['w_out_a'][a]
    return y, s_new.astype(h.dtype), v.reshape(B, L, CM_WIDTH)


def compress(kv, pe, w1, w2):
    B, Lt, G, dh = kv.shape
    span = NSA_CMP_LEN // NSA_CMP_STRIDE
    nc = (Lt - NSA_CMP_LEN) // NSA_CMP_STRIDE + 1
    nseg = nc + span - 1
    seg = kv[:, :nseg * NSA_CMP_STRIDE].reshape(B, nseg, NSA_CMP_STRIDE, G, dh)
    w1r = w1.reshape(span, NSA_CMP_STRIDE, dh, NSA_CMP_HIDDEN)
    pre = jnp.einsum('id,ide->e', pe, w1.reshape(NSA_CMP_LEN, dh, NSA_CMP_HIDDEN))
    for m in range(span):
        pre = pre + jnp.einsum('bnigd,ide->bnge', seg, w1r[m])[:, m:m + nc]
    return jnp.einsum('bnge,ed->bngd', jax.nn.gelu(pre), w2)


def blocked_fetch(rows, bs):
    B, L, G, dh = rows.shape
    nb = -(-L // bs)
    arr = jnp.pad(rows, ((0, 0), (0, nb * bs - L), (0, 0), (0, 0)))
    arr = arr.reshape(B, nb, bs, G, dh).transpose(0, 1, 3, 2, 4)

    def fetch(blk, head):
        bidx = jnp.arange(B).reshape((B,) + (1,) * (blk.ndim - 1))
        return arr[bidx, blk, head]
    return fetch


def paged_fetch(pool, pt, li, head_off, new_rows, bs):
    B = pt.shape[0]
    past_len = pt.shape[1] * PAGE_SIZE
    n_new = new_rows.shape[1]

    def fetch(blk, head):
        pos = blk[..., None] * bs + jnp.arange(bs)
        hd = jnp.asarray(head)[..., None]
        bidx = jnp.arange(B).reshape((B,) + (1,) * (pos.ndim - 1))
        pp = jnp.clip(pos, 0, past_len - 1)
        old = pool[pt[bidx, pp // PAGE_SIZE], li, pp % PAGE_SIZE, head_off + hd]
        new = new_rows[bidx, jnp.clip(pos - past_len, 0, n_new - 1), hd]
        return jnp.where((pos < past_len)[..., None], old, new.astype(old.dtype))
    return fetch


def paged_rows(pool, pt, li, h0, h1):
    B, NP = pt.shape
    return pool[pt, li, :, h0:h1].reshape(B, NP * PAGE_SIZE, h1 - h0, HEAD_DIM)


def nsa_block(qu, qr, gl, qpos, kc, vc, fetch_k, fetch_v, n_sel, wk, wv, wpos):
    B, Qb = qu.shape[:2]
    scale = HEAD_DIM ** -0.5
    qu = qu.reshape(B, Qb, NSA_KV, NSA_GRP, HEAD_DIM)
    qr = qr.reshape(B, Qb, NSA_KV, NSA_GRP, HEAD_DIM)
    tq = qpos[:, None]
    nc = kc.shape[1]
    c_end = jnp.arange(nc) * NSA_CMP_STRIDE + (NSA_CMP_LEN - 1)
    cmask = (c_end[None, :] <= tq)[None, :, None, None, :]
    p_c = masked_softmax(jnp.einsum('bqgrd,bcgd->bqgrc', qu, kc) * scale, cmask)
    o_c = jnp.einsum('bqgrc,bcgd->bqgrd', p_c.astype(vc.dtype), vc)
    ratio = NSA_SEL_BLOCK // NSA_CMP_STRIDE
    span = NSA_CMP_LEN // NSA_CMP_STRIDE
    offs = np.arange(-(span - 1), ratio)
    wts = np.array([sum(1 for m in range(ratio) for n in range(span) if m - n == o) for o in offs], np.float32)
    lo = span - 1
    imp = jnp.pad(p_c.sum(axis=3), ((0, 0), (0, 0), (0, 0), (lo, max(0, ratio * n_sel - nc))))
    idx = np.arange(n_sel)[:, None] * ratio + offs[None, :] + lo
    score = jnp.sum(imp[..., idx] * wts, axis=-1)
    blk = jnp.arange(n_sel)[None, :]
    cur = (qpos // NSA_SEL_BLOCK)[:, None]
    forced = (blk == 0) | (blk == cur) | (blk == cur - 1)
    score = jnp.where(forced[None, :, None, :], BIG, score)
    score = jnp.where((blk <= cur)[None, :, None, :], score, -BIG)
    k_top = min(NSA_TOPN, n_sel)
    top_v, top_i = lax.top_k(score, k_top)
    hk = jnp.arange(NSA_KV)[:, None]
    ks = fetch_k(top_i, hk).reshape(B, Qb, NSA_KV, k_top * NSA_SEL_BLOCK, HEAD_DIM)
    vs = fetch_v(top_i, hk).reshape(B, Qb, NSA_KV, k_top * NSA_SEL_BLOCK, HEAD_DIM)
    kpos = top_i[..., None] * NSA_SEL_BLOCK + jnp.arange(NSA_SEL_BLOCK)
    smask = ((top_v > -BIG / 2)[..., None] & (kpos <= qpos[None, :, None, None, None]))
    smask = smask.reshape(B, Qb, NSA_KV, 1, k_top * NSA_SEL_BLOCK)
    p_s = masked_softmax(jnp.einsum('bqgrd,bqgkd->bqgrk', qr, ks) * scale, smask)
    o_s = jnp.einsum('bqgrk,bqgkd->bqgrd', p_s.astype(vs.dtype), vs)
    wmask = ((wpos[None, :] <= tq) & (wpos[None, :] >= tq - NSA_WINDOW) & (wpos[None, :] >= 0))[None, :, None, None, :]
    p_w = masked_softmax(jnp.einsum('bqgrd,bkgd->bqgrk', qr, wk) * scale, wmask)
    o_w = jnp.einsum('bqgrk,bkgd->bqgrd', p_w.astype(wv.dtype), wv)
    g = jax.nn.sigmoid(gl.astype(f32)).reshape(B, Qb, NSA_KV, NSA_GRP, 3)
    o = g[..., 0:1] * o_c + g[..., 1:2] * o_s + g[..., 2:3] * o_w
    return o.reshape(B, Qb, NSA_HEADS * HEAD_DIM).astype(qu.dtype)


def moba_block(q, qpos, kbar, fetch_k, fetch_v):
    B, Qb = q.shape[:2]
    scale = HEAD_DIM ** -0.5
    q = q.reshape(B, Qb, MOBA_KV, MOBA_GRP, HEAD_DIM)
    own = qpos // MOBA_BLOCK
    hk = jnp.arange(MOBA_KV)
    own_b = jnp.broadcast_to(own[None, :, None], (B, Qb, MOBA_KV))
    ko = fetch_k(own_b, hk)
    vo = fetch_v(own_b, hk)
    opos = own[:, None] * MOBA_BLOCK + jnp.arange(MOBA_BLOCK)
    omask = jnp.broadcast_to((opos <= qpos[:, None])[None, :, None, None, :], (B, Qb, MOBA_KV, MOBA_GRP, MOBA_BLOCK))
    s_o = jnp.einsum('bqgrd,bqgkd->bqgrk', q, ko) * scale
    nbf = kbar.shape[1]
    if nbf == 0:
        p = masked_softmax(s_o, omask)
        o = jnp.einsum('bqgrk,bqgkd->bqgrd', p.astype(vo.dtype), vo)
    else:
        gs = jnp.einsum('bqgrd,bngd->bqgrn', q, kbar).astype(f32)
        gvalid = (jnp.arange(nbf)[None, :] < own[:, None])[None, :, None, None, :]
        gs = jnp.where(gvalid, gs, -BIG)
        kk = min(MOBA_TOPK, nbf)
        top_v, top_i = lax.top_k(gs, kk)
        hs = hk[:, None, None]
        ks = fetch_k(top_i, hs).reshape(B, Qb, MOBA_KV, MOBA_GRP, kk * MOBA_BLOCK, HEAD_DIM)
        vs = fetch_v(top_i, hs).reshape(B, Qb, MOBA_KV, MOBA_GRP, kk * MOBA_BLOCK, HEAD_DIM)
        smask = jnp.repeat(top_v > -BIG / 2, MOBA_BLOCK, axis=-1)
        s = jnp.concatenate([jnp.einsum('bqgrd,bqgrkd->bqgrk', q, ks) * scale, s_o], axis=-1)
        p = masked_softmax(s, jnp.concatenate([smask, omask], axis=-1))
        n_s = kk * MOBA_BLOCK
        o = (jnp.einsum('bqgrk,bqgrkd->bqgrd', p[..., :n_s].astype(vs.dtype), vs)
             + jnp.einsum('bqgrk,bqgkd->bqgrd', p[..., n_s:].astype(vo.dtype), vo))
    return o.reshape(B, Qb, MOBA_HEADS * HEAD_DIM).astype(q.dtype)


def mixer_c(h, pos, c, prm, ctx):
    B, L, _ = h.shape
    z = h @ prm['w_in_c'][c]
    nq, ck, cv, sk, sv, wk, wv, ng, mq, mk, mv = jnp.split(z, np.cumsum(C_SPLITS)[:-1].tolist(), axis=-1)
    nq = nq.reshape(B, L, NSA_HEADS, HEAD_DIM)
    nq_r = partial_rope(nq, pos)
    ck = ck.reshape(B, L, NSA_KV, HEAD_DIM)
    cv = cv.reshape(B, L, NSA_KV, HEAD_DIM)
    sk = partial_rope(sk.reshape(B, L, NSA_KV, HEAD_DIM), pos)
    sv = sv.reshape(B, L, NSA_KV, HEAD_DIM)
    wk = partial_rope(wk.reshape(B, L, NSA_KV, HEAD_DIM), pos)
    wv = wv.reshape(B, L, NSA_KV, HEAD_DIM)
    ng = ng.reshape(B, L, NSA_HEADS, 3)
    mq = partial_rope(mq.reshape(B, L, MOBA_HEADS, HEAD_DIM), pos)
    mk = partial_rope(mk.reshape(B, L, MOBA_KV, HEAD_DIM), pos)
    mv = mv.reshape(B, L, MOBA_KV, HEAD_DIM)
    rows = jnp.concatenate([ck, cv, sk, sv, mk, mv], axis=2)
    win = jnp.concatenate([wk, wv], axis=2)
    if ctx is None:
        ck_all, cv_all, mk_all = ck, cv, mk
        f_sk, f_sv = blocked_fetch(sk, NSA_SEL_BLOCK), blocked_fetch(sv, NSA_SEL_BLOCK)
        f_mk, f_mv = blocked_fetch(mk, MOBA_BLOCK), blocked_fetch(mv, MOBA_BLOCK)
    else:
        pool, pt = ctx['cache_kv'], ctx['page_table']
        cmp_past = paged_rows(pool, pt, c, ROW_CMP_K, ROW_CMP_V + NSA_KV)
        ck_all = jnp.concatenate([cmp_past[:, :, :NSA_KV], ck.astype(pool.dtype)], axis=1)
        cv_all = jnp.concatenate([cmp_past[:, :, NSA_KV:], cv.astype(pool.dtype)], axis=1)
        mk_all = jnp.concatenate([paged_rows(pool, pt, c, ROW_MOBA_K, ROW_MOBA_K + MOBA_KV), mk.astype(pool.dtype)], axis=1)
        f_sk = paged_fetch(pool, pt, c, ROW_SEL_K, sk, NSA_SEL_BLOCK)
        f_sv = paged_fetch(pool, pt, c, ROW_SEL_V, sv, NSA_SEL_BLOCK)
        f_mk = paged_fetch(pool, pt, c, ROW_MOBA_K, mk, MOBA_BLOCK)
        f_mv = paged_fetch(pool, pt, c, ROW_MOBA_V, mv, MOBA_BLOCK)
    kc = compress(ck_all, prm['cmp_pe_k'][c], prm['cmp_w1_k'][c], prm['cmp_w2_k'][c])
    vc = compress(cv_all, prm['cmp_pe_v'][c], prm['cmp_w1_v'][c], prm['cmp_w2_v'][c])
    lt = mk_all.shape[1]
    nbf = lt // MOBA_BLOCK
    kbar = mk_all[:, :nbf * MOBA_BLOCK].astype(f32).reshape(B, nbf, MOBA_BLOCK, MOBA_KV, HEAD_DIM).mean(axis=2).astype(h.dtype)
    n_sel = -(-lt // NSA_SEL_BLOCK)
    if ctx is None:
        def to_blocks(t, n):
            return t.reshape((B, L // n, n) + t.shape[2:]).swapaxes(0, 1)
        wpad = jnp.pad(win, ((0, 0), (NSA_WINDOW, 0), (0, 0), (0, 0)))

        def nsa_body(args):
            i, qu, qr, g = args
            t0 = i * NSA_QBLK
            wblk = lax.dynamic_slice_in_dim(wpad, t0, NSA_WINDOW + NSA_QBLK, axis=1)
            wpos = t0 - NSA_WINDOW + jnp.arange(NSA_WINDOW + NSA_QBLK)
            return nsa_block(qu, qr, g, t0 + jnp.arange(NSA_QBLK), kc, vc, f_sk, f_sv, n_sel,
                             wblk[:, :, :NSA_KV], wblk[:, :, NSA_KV:], wpos)

        def moba_body(args):
            i, qb = args
            return moba_block(qb, i * MOBA_QBLK + jnp.arange(MOBA_QBLK), kbar, f_mk, f_mv)

        o_n = lax.map(nsa_body, (jnp.arange(L // NSA_QBLK), to_blocks(nq, NSA_QBLK), to_blocks(nq_r, NSA_QBLK), to_blocks(ng, NSA_QBLK)))
        o_n = o_n.swapaxes(0, 1).reshape(B, L, NSA_HEADS * HEAD_DIM)
        o_m = lax.map(moba_body, (jnp.arange(L // MOBA_QBLK), to_blocks(mq, MOBA_QBLK)))
        o_m = o_m.swapaxes(0, 1).reshape(B, L, MOBA_HEADS * HEAD_DIM)
        new_win = win[:, L - min(NSA_WINDOW, L):]
    else:
        past_len = ctx['page_table'].shape[1] * PAGE_SIZE
        qpos = past_len + jnp.arange(L)
        wb = ctx['cache_win'][c]
        wall = jnp.concatenate([jnp.zeros((B, NSA_WINDOW - wb.shape[1], 2 * NSA_KV, HEAD_DIM), h.dtype), wb.astype(h.dtype), win], axis=1)
        wpos = past_len - NSA_WINDOW + jnp.arange(NSA_WINDOW + L)
        o_n = nsa_block(nq, nq_r, ng, qpos, kc, vc, f_sk, f_sv, n_sel, wall[:, :, :NSA_KV], wall[:, :, NSA_KV:], wpos)
        o_m = moba_block(mq, qpos, kbar, f_mk, f_mv)
        new_win = win
    y = jnp.concatenate([o_n, o_m], axis=-1).astype(h.dtype) @ prm['w_out_c'][c]
    return y, rows, new_win


def expert_dispatch(xt, eid, gate, w1, w3, w2):
    T, D = xt.shape
    K = eid.shape[1]
    TK = T * K
    flat = eid.reshape(-1)
    order = jnp.argsort(flat)
    se = flat[order]
    tok = order // K
    gw = gate.reshape(-1)[order]
    counts = jnp.bincount(flat, length=MOE_EXPERTS)
    padc = (counts + MOE_ROWBLK - 1) // MOE_ROWBLK * MOE_ROWBLK
    pend = jnp.cumsum(padc)
    pstart = pend - padc
    cstart = jnp.cumsum(counts) - counts
    dest = pstart[se] + jnp.arange(TK) - cstart[se]
    nblk = -(-TK // MOE_ROWBLK) + MOE_EXPERTS
    rows = jnp.full((nblk * MOE_ROWBLK,), T, jnp.int32).at[dest].set(tok)
    xp = jnp.concatenate([xt, jnp.zeros((1, D), xt.dtype)], axis=0)[rows].reshape(nblk, MOE_ROWBLK, D)
    blk_e = jnp.minimum(jnp.searchsorted(pend, jnp.arange(nblk) * MOE_ROWBLK, side='right'), MOE_EXPERTS - 1)

    def run(args):
        xb, e = args
        return (jax.nn.silu(xb @ w1[e]) * (xb @ w3[e])) @ w2[e]

    yp = lax.map(run, (xp, blk_e)).reshape(nblk * MOE_ROWBLK, D)
    return jnp.zeros((T, D), xt.dtype).at[tok].add((yp[dest] * gw[:, None]).astype(xt.dtype))


def hier_moe(x, prm, i):
    B, L, D = x.shape
    xt = x.reshape(B * L, D)
    pc = jax.nn.softmax((xt @ prm['router_c_w'][i]).astype(f32) + prm['router_c_b'][i].astype(f32), axis=-1)
    pg, grp = lax.top_k(pc, 1)
    lf = jnp.einsum('td,gde->tge', xt, prm['router_f_w'][i]).astype(f32) + prm['router_f_b'][i].astype(f32)
    pf = jax.nn.softmax(jnp.take_along_axis(lf, grp[:, :, None], axis=1)[:, 0], axis=-1)
    tv, ti = lax.top_k(pf, MOE_TOPK)
    gate = pg * tv / jnp.sum(tv, axis=-1, keepdims=True)
    eid = grp * MOE_EPG + ti
    y = expert_dispatch(xt, eid, gate, prm['moe_w1'][i], prm['moe_w3'][i], prm['moe_w2'][i])
    return y.reshape(B, L, D)


def per_layer_embed(h, p_i, prm, i):
    gate = jax.nn.sigmoid((rmsnorm(h, prm['norm_ple'][i]) @ prm['ple_gate_w'][i]).astype(f32)).astype(h.dtype)
    return h + gate * (p_i.astype(h.dtype) @ prm['ple_w'][i])


def forward(x, p, pos, prm, ctx):
    h = x
    kv, win, hg, cm = [], [], [], []
    for i in range(DEPTH):
        hn = rmsnorm(h, prm['norm_mix'][i])
        if i % 2 == 0:
            a = i // 2
            if ctx is None:
                s0 = jnp.zeros((x.shape[0], HG_HEADS, HG_DK, HG_DV), f32)
            else:
                s0 = ctx['state_hgrn'][a]
            y, s_new, v_rows = mixer_a(hn, s0, a, prm)
            hg.append(s_new)
            cm.append(v_rows)
        else:
            y, rows, w_rows = mixer_c(hn, pos, i // 2, prm, ctx)
            kv.append(rows)
            win.append(w_rows)
        h = h + y
        h = h + hier_moe(rmsnorm(h, prm['norm_ffn'][i]), prm, i)
        h = per_layer_embed(h, p[i], prm, i)
    return rmsnorm(h, prm['final_norm']), kv, win, hg, cm


def setup_inputs(seed: int = 0) -> dict:
    key = jax.random.key(seed)
    ks = jax.random.split(key, 40)

    def nrm(i, shape, scale):
        return jax.random.normal(ks[i], shape, f32) * scale

    def gain(i, shape):
        return 1.0 + nrm(i, shape, 0.1)

    n_pages = PAST_LEN // PAGE_SIZE
    n_used = DEC_BATCH * n_pages
    n_pool = n_used + -(-n_used // 4)
    page_table = jax.random.permutation(ks[0], n_pool)[:n_used].reshape(DEC_BATCH, n_pages).astype(jnp.int32)
    win_buf = min(NSA_WINDOW, PAST_LEN)
    d = D_MODEL
    return {
        'x_prompt': nrm(1, (BATCH, SEQ, d), 1.0),
        'x_sample': nrm(2, (DEC_BATCH, DEC_SEQ, d), 1.0),
        'cache_kv': nrm(3, (n_pool, N_C_LAYERS, PAGE_SIZE, KV_ROW_HEADS, HEAD_DIM), 1.0),
        'cache_win': nrm(4, (N_C_LAYERS, DEC_BATCH, win_buf, 2 * NSA_KV, HEAD_DIM), 1.0),
        'state_hgrn': nrm(5, (N_A_LAYERS, DEC_BATCH, HG_HEADS, HG_DK, HG_DV), 0.5),
        'page_table': page_table,
        'p_prompt': nrm(6, (DEPTH, BATCH, SEQ, PLE_DIM), 1.0),
        'p_sample': nrm(7, (DEPTH, DEC_BATCH, DEC_SEQ, PLE_DIM), 1.0),
        'norm_mix': gain(8, (DEPTH, d)),
        'norm_ffn': gain(9, (DEPTH, d)),
        'norm_ple': gain(10, (DEPTH, d)),
        'final_norm': gain(11, (d,)),
        'w_in_a': nrm(12, (N_A_LAYERS, d, sum(A_SPLITS)), d ** -0.5),
        'w_out_a': nrm(13, (N_A_LAYERS, HG_WIDTH + CM_WIDTH, d), (HG_WIDTH + CM_WIDTH) ** -0.5),
        'hgrn_lb': nrm(14, (N_A_LAYERS + 1, HG_DK), 1.0),
        'hgrn_onorm': gain(15, (N_A_LAYERS, HG_WIDTH)),
        'cm_vnorm': gain(16, (N_A_LAYERS, CM_WIDTH)),
        'cm_ws': nrm(17, (N_A_LAYERS, CM_GROUPS, CM_CHUNK, CM_CHUNK), CM_CHUNK ** -0.5),
        'cm_bs': gain(18, (N_A_LAYERS, CM_GROUPS, CM_CHUNK)),
        'w_in_c': nrm(19, (N_C_LAYERS, d, sum(C_SPLITS)), d ** -0.5),
        'w_out_c': nrm(20, (N_C_LAYERS, (NSA_HEADS + MOBA_HEADS) * HEAD_DIM, d), ((NSA_HEADS + MOBA_HEADS) * HEAD_DIM) ** -0.5),
        'cmp_pe_k': nrm(21, (N_C_LAYERS, NSA_CMP_LEN, HEAD_DIM), 0.5),
        'cmp_w1_k': nrm(22, (N_C_LAYERS, NSA_CMP_LEN * HEAD_DIM, NSA_CMP_HIDDEN), (NSA_CMP_LEN * HEAD_DIM) ** -0.5),
        'cmp_w2_k': nrm(23, (N_C_LAYERS, NSA_CMP_HIDDEN, HEAD_DIM), NSA_CMP_HIDDEN ** -0.5),
        'cmp_pe_v': nrm(24, (N_C_LAYERS, NSA_CMP_LEN, HEAD_DIM), 0.5),
        'cmp_w1_v': nrm(25, (N_C_LAYERS, NSA_CMP_LEN * HEAD_DIM, NSA_CMP_HIDDEN), (NSA_CMP_LEN * HEAD_DIM) ** -0.5),
        'cmp_w2_v': nrm(26, (N_C_LAYERS, NSA_CMP_HIDDEN, HEAD_DIM), NSA_CMP_HIDDEN ** -0.5),
        'router_c_w': nrm(27, (DEPTH, d, MOE_GROUPS), d ** -0.5),
        'router_c_b': nrm(28, (DEPTH, MOE_GROUPS), 0.01),
        'router_f_w': nrm(29, (DEPTH, MOE_GROUPS, d, MOE_EPG), d ** -0.5),
        'router_f_b': nrm(30, (DEPTH, MOE_GROUPS, MOE_EPG), 0.01),
        'moe_w1': nrm(31, (DEPTH, MOE_EXPERTS, d, MOE_FF), d ** -0.5),
        'moe_w3': nrm(32, (DEPTH, MOE_EXPERTS, d, MOE_FF), d ** -0.5),
        'moe_w2': nrm(33, (DEPTH, MOE_EXPERTS, MOE_FF, d), MOE_FF ** -0.5),
        'ple_w': nrm(34, (DEPTH, PLE_DIM, d), PLE_DIM ** -0.5),
        'ple_gate_w': nrm(35, (DEPTH, d, d), d ** -0.5),
    }


def reference(x_prompt, x_sample, cache_kv, cache_win, state_hgrn, page_table, p_prompt, p_sample,
              norm_mix, norm_ffn, norm_ple, final_norm, w_in_a, w_out_a, hgrn_lb, hgrn_onorm, cm_vnorm,
              cm_ws, cm_bs, w_in_c, w_out_c, cmp_pe_k, cmp_w1_k, cmp_w2_k, cmp_pe_v, cmp_w1_v, cmp_w2_v,
              router_c_w, router_c_b, router_f_w, router_f_b, moe_w1, moe_w3, moe_w2, ple_w, ple_gate_w):
    prm = dict(norm_mix=norm_mix, norm_ffn=norm_ffn, norm_ple=norm_ple, final_norm=final_norm,
               w_in_a=w_in_a, w_out_a=w_out_a, hgrn_lb=hgrn_lb, hgrn_onorm=hgrn_onorm, cm_vnorm=cm_vnorm,
               cm_ws=cm_ws, cm_bs=cm_bs, w_in_c=w_in_c, w_out_c=w_out_c, cmp_pe_k=cmp_pe_k,
               cmp_w1_k=cmp_w1_k, cmp_w2_k=cmp_w2_k, cmp_pe_v=cmp_pe_v, cmp_w1_v=cmp_w1_v, cmp_w2_v=cmp_w2_v,
               router_c_w=router_c_w, router_c_b=router_c_b, router_f_w=router_f_w, router_f_b=router_f_b,
               moe_w1=moe_w1, moe_w3=moe_w3, moe_w2=moe_w2, ple_w=ple_w, ple_gate_w=ple_gate_w)
    past_len = page_table.shape[1] * PAGE_SIZE
    y_prompt, kv_p, win_p, hg_p, _ = forward(x_prompt, p_prompt, jnp.arange(x_prompt.shape[1]), prm, None)
    ctx = dict(cache_kv=cache_kv, cache_win=cache_win, state_hgrn=state_hgrn, page_table=page_table)
    y_sample, kv_s, win_s, hg_s, cm_s = forward(x_sample, p_sample, past_len + jnp.arange(x_sample.shape[1]), prm, ctx)
    kv_prompt = jnp.stack(kv_p, axis=1)
    kv_sample = jnp.stack(kv_s, axis=1)
    win_prompt = jnp.stack(win_p, axis=0)
    win_sample = jnp.stack(win_s, axis=0)
    hgrn_prompt = jnp.stack(hg_p, axis=0)
    hgrn_sample = jnp.stack(hg_s, axis=0)
    cmlp_v_sample = jnp.stack(cm_s, axis=0)
    return (y_prompt, y_sample, kv_prompt, kv_sample, win_prompt, win_sample, hgrn_prompt, hgrn_sample, cmlp_v_sample)
```

```python
import functools
import math

import numpy as np
import jax
import jax.numpy as jnp
from jax import lax
from jax.experimental import pallas as pl
from jax.experimental.pallas import tpu as pltpu

F32 = jnp.float32
BF16 = jnp.bfloat16
HI = lax.Precision.HIGHEST

PAGE_SIZE = 128
HEAD_DIM = 64
ROT_DIM = HEAD_DIM // 4
ROPE_THETA = 500000.0
HG_HEADS = 8
HG_DK = 64
HG_WIDTH = HG_HEADS * HG_DK
HG_CHUNK = 64
CM_GROUPS = 4
CM_GW = 128
CM_WIDTH = CM_GROUPS * CM_GW
CM_CHUNK = 128
NSA_HEADS = 8
NSA_KV = 2
NSA_GRP = NSA_HEADS // NSA_KV
NSA_CMP_LEN = 32
NSA_CMP_STRIDE = 16
NSA_SEL_BLOCK = 64
NSA_TOPN = 16
NSA_WINDOW = 512
MOBA_HEADS = 8
MOBA_KV = 4
MOBA_GRP = MOBA_HEADS // MOBA_KV
MOBA_BLOCK = 256
MOBA_TOPK = 3
MOE_GROUPS = 4
MOE_EPG = 8
MOE_EXPERTS = MOE_GROUPS * MOE_EPG
KV_ROW_HEADS = 16
ROW_W = KV_ROW_HEADS * HEAD_DIM
A_SPLITS = [HG_WIDTH] * 4 + [CM_WIDTH] * 2
C_SPLITS = [NSA_HEADS * HEAD_DIM] + [NSA_KV * HEAD_DIM] * 6 + [NSA_HEADS * 3, MOBA_HEADS * HEAD_DIM,
                                                              MOBA_KV * HEAD_DIM, MOBA_KV * HEAD_DIM]
NEG = -1e30
BIG = 1e9
EPS = 1e-6
KEY_TILE = 256
LANES = 128
_MOBA_SHIFT = MOBA_BLOCK.bit_length() - 1
_SEL_SHIFT = NSA_SEL_BLOCK.bit_length() - 1
VMEM_LIMIT = 56 * 1024 * 1024


def _cp(*sem):
    return pltpu.CompilerParams(dimension_semantics=sem, vmem_limit_bytes=VMEM_LIMIT)


def _sigmoid(x):
    return 1.0 / (1.0 + jnp.exp(-x))


def _silu(x):
    return x * _sigmoid(x)


def _gelu(x):
    return 0.5 * x * (1.0 + jnp.tanh(math.sqrt(2.0 / math.pi) * (x + 0.044715 * (x * x * x))))


def _rms(x, g):
    return x * lax.rsqrt(jnp.mean(x * x, axis=-1, keepdims=True) + EPS) * g


def _dot(a, b):
    return jnp.dot(a, b, preferred_element_type=F32)


def _dot_hi(a, b):
    return jnp.dot(a, b, precision=HI, preferred_element_type=F32)


def _full(shape):
    n = len(shape)
    return pl.BlockSpec(shape, lambda *_: (0,) * n)


def _row_tile(t):
    for tm in (256, 128, 64, 32, 16, 8):
        if t % tm == 0:
            return tm
    raise ValueError(f"token count {t} is not a multiple of 8")


def _proj_a_kernel(x_ref, g_ref, w_ref, lb_ref, vg_ref, q_ref, k_ref, lf_ref, iv_ref, sg_ref, gu_ref, vn_ref):
    xn = _rms(x_ref[...], g_ref[...]).astype(BF16)
    z = _dot(xn, w_ref[...])
    w = HG_WIDTH
    q_ref[...] = _silu(z[:, 0:w])
    lb = lb_ref[...]
    f = lb + (1.0 - lb) * _sigmoid(z[:, w:2 * w])
    k_ref[...] = 1.0 - f
    lf_ref[...] = jnp.log(f)
    iv_ref[...] = z[:, 2 * w:3 * w]
    sg_ref[...] = _silu(z[:, 3 * w:4 * w])
    gu_ref[...] = _gelu(z[:, 4 * w:4 * w + CM_WIDTH])
    v = _gelu(z[:, 4 * w + CM_WIDTH:])
    for gi in range(CM_GROUPS):
        sl = slice(gi * CM_GW, (gi + 1) * CM_GW)
        vn_ref[:, sl] = _rms(v[:, sl], vg_ref[:, sl])


def _proj_a(x, g, w_bf, lb512, vgain):
    t, d = x.shape
    tm = _row_tile(t)
    n = w_bf.shape[1]
    row = lambda wd: pl.BlockSpec((tm, wd), lambda i: (i, 0))
    outs = [jax.ShapeDtypeStruct((t, HG_WIDTH), F32)] * 5 + [jax.ShapeDtypeStruct((t, CM_WIDTH), F32)] * 2
    return pl.pallas_call(
        _proj_a_kernel,
        grid=(t // tm,),
        in_specs=[row(d), _full((1, d)), _full((d, n)), _full((1, HG_WIDTH)), _full((1, CM_WIDTH))],
        out_specs=[row(HG_WIDTH)] * 5 + [row(CM_WIDTH)] * 2,
        out_shape=outs,
        compiler_params=_cp("parallel"),
        name="proj_a",
    )(x, g, w_bf, lb512, vgain)


def _hgrn_kernel(q_ref, k_ref, lf_ref, v_ref, sg_ref, s0_ref, gain_ref, bones_ref, bmask_ref, tri_ref,
                 o_ref, sout_ref, st_scr, p_scr, *, chunk):
    c = pl.program_id(1)
    w = HG_WIDTH
    hw = w // 2

    @pl.when(c == 0)
    def _():
        st_scr[...] = s0_ref[0]

    q = q_ref[0]
    k = k_ref[0]
    v = v_ref[0]
    b = _dot_hi(tri_ref[...], lf_ref[0])
    bones = bones_ref[...]

    def head_sum(x):
        xb = x.astype(BF16)
        return jnp.concatenate([_dot(xb[:, :hw], bones), _dot(xb[:, hw:], bones)], axis=1)

    o_rows = []
    for blk in range(chunk // 8):
        s_len = 8 * (blk + 1)
        bs = b[:s_len]
        ks = k[:s_len]
        vs = v[:s_len]
        row_id = lax.broadcasted_iota(jnp.int32, (s_len, w), 0)
        for t in range(8):
            r = 8 * blk + t
            diff = jnp.where(row_id <= r, b[r:r + 1, :] - bs, NEG)
            p_scr[t * s_len:(t + 1) * s_len, :] = jnp.exp(diff) * q[r:r + 1, :] * ks
        att = head_sum(p_scr[0:8 * s_len, :])
        for t in range(8):
            o_rows.append(jnp.sum(att[t * s_len:(t + 1) * s_len] * vs, axis=0, keepdims=True))
    o_intra = jnp.concatenate(o_rows, axis=0)

    st = st_scr[...]
    qe = (q * jnp.exp(b)).astype(BF16)
    o = o_intra + lax.dot_general(qe, st.astype(BF16), (((1,), (1,)), ((), ())), preferred_element_type=F32)

    b_end = b[chunk - 1:chunk, :]
    kd = (k * jnp.exp(b_end - b)).astype(BF16)
    upd = lax.dot_general(v.astype(BF16), kd, (((0,), (0,)), ((), ())), preferred_element_type=F32)
    st_new = st * jnp.exp(b_end) + upd * bmask_ref[...]
    st_scr[...] = st_new

    o2 = o * o
    hi = o2.astype(BF16).astype(F32)
    ms = (head_sum(hi) + head_sum(o2 - hi)) * (1.0 / HG_DK)
    o_ref[0] = o * lax.rsqrt(ms + EPS) * gain_ref[...] * sg_ref[0]

    @pl.when(c == pl.num_programs(1) - 1)
    def _():
        sout_ref[0] = st_new


def _block_ones(n, blk, dtype):
    i = np.arange(n) // blk
    return jnp.asarray((i[:, None] == i[None, :]).astype(np.float32), dtype)


def _hgrn(q, k, lf, v, sg, st0, gain, batch, seq):
    w = HG_WIDTH
    chunk = math.gcd(seq, HG_CHUNK)
    nch = seq // chunk
    r3 = lambda a: a.reshape(batch, seq, w)
    tile = pl.BlockSpec((1, chunk, w), lambda b, c: (b, c, 0))
    state = pl.BlockSpec((1, w, w), lambda b, c: (b, 0, 0))
    tri = jnp.asarray(np.tril(np.ones((chunk, chunk), np.float32)))
    o, st = pl.pallas_call(
        functools.partial(_hgrn_kernel, chunk=chunk),
        grid=(batch, nch),
        in_specs=[tile] * 5 + [state, _full((1, w)), _full((w // 2, w // 2)), _full((w, w)), _full((chunk, chunk))],
        out_specs=[tile, state],
        out_shape=[jax.ShapeDtypeStruct((batch, seq, w), F32), jax.ShapeDtypeStruct((batch, w, w), F32)],
        scratch_shapes=[pltpu.VMEM((w, w), F32), pltpu.VMEM((8 * chunk, w), F32)],
        compiler_params=_cp("parallel", "arbitrary"),
        name="hgrn",
    )(r3(q), r3(k), r3(lf), r3(v), r3(sg), st0, gain, _block_ones(w // 2, HG_DK, BF16),
      _block_ones(w, HG_DK, F32), tri)
    return o.reshape(batch * seq, w), st


def _cmix_kernel(vn_ref, gu_ref, ws_ref, bst_ref, o_ref, *, tl):
    ri = lax.broadcasted_iota(jnp.int32, (tl, tl), 0)
    ci = lax.broadcasted_iota(jnp.int32, (tl, tl), 1)
    for gi in range(CM_GROUPS):
        sl = slice(gi * CM_GW, (gi + 1) * CM_GW)
        wg = jnp.where(ci <= ri, ws_ref[gi, 0:tl, 0:tl], 0.0).astype(BF16)
        mix = _dot(wg, vn_ref[0, :, sl].astype(BF16)) + bst_ref[0:tl, gi:gi + 1]
        o_ref[0, :, sl] = gu_ref[0, :, sl] * mix


def _cmix(vn, gu, ws, bst, batch, seq):
    tl = min(seq, CM_CHUNK)
    assert seq % tl == 0 and tl % 8 == 0
    w = CM_WIDTH
    tile = pl.BlockSpec((1, tl, w), lambda b, c: (b, c, 0))
    out = pl.pallas_call(
        functools.partial(_cmix_kernel, tl=tl),
        grid=(batch, seq // tl),
        in_specs=[tile, tile, _full(ws.shape), _full(bst.shape)],
        out_specs=tile,
        out_shape=jax.ShapeDtypeStruct((batch, seq, w), F32),
        compiler_params=_cp("parallel", "parallel"),
        name="cmix",
    )(vn.reshape(batch, seq, w), gu.reshape(batch, seq, w), ws, bst)
    return out.reshape(batch * seq, w)


def _out_proj_kernel(res_ref, a1_ref, a2_ref, w1_ref, w2_ref, o_ref):
    o_ref[...] = (res_ref[...] + _dot(a1_ref[...].astype(BF16), w1_ref[...])
                  + _dot(a2_ref[...].astype(BF16), w2_ref[...]))


def _out_proj(res, a1, a2, w_bf):
    t, d = res.shape
    k1, k2 = a1.shape[1], a2.shape[1]
    tm = _row_tile(t)
    row = lambda wd: pl.BlockSpec((tm, wd), lambda i: (i, 0))
    return pl.pallas_call(
        _out_proj_kernel,
        grid=(t // tm,),
        in_specs=[row(d), row(k1), row(k2), _full((k1, d)), _full((k2, d))],
        out_specs=row(d),
        out_shape=jax.ShapeDtypeStruct((t, d), F32),
        compiler_params=_cp("parallel"),
        name="out_proj",
    )(res, a1, a2, w_bf[:k1], w_bf[k1:])


def _router_kernel(h_ref, g_ref, w_ref, b_ref, xn_ref, info_ref):
    xn = _rms(h_ref[...], g_ref[...])
    xn_ref[...] = xn.astype(BF16)
    logit = _dot_hi(xn, w_ref[...]) + b_ref[...]
    lane = lax.broadcasted_iota(jnp.int32, logit.shape, 1)
    far = 1 << 20

    def first_max(vals, mask):
        m = jnp.max(jnp.where(mask, vals, -1.0), axis=-1, keepdims=True)
        idx = jnp.min(jnp.where(mask & (vals == m), lane, far), axis=-1, keepdims=True)
        return m, idx

    def softmax(mask):
        m = jnp.max(jnp.where(mask, logit, NEG), axis=-1, keepdims=True)
        e = jnp.where(mask, jnp.exp(logit - m), 0.0)
        return e / jnp.sum(e, axis=-1, keepdims=True)

    cmask = lane < MOE_GROUPS
    pg, grp = first_max(softmax(cmask), cmask)
    lo = MOE_GROUPS + grp * MOE_EPG
    fmask = (lane >= lo) & (lane < lo + MOE_EPG)
    pf = softmax(fmask)
    v1, i1 = first_max(pf, fmask)
    v2, i2 = first_max(pf, fmask & (lane != i1))
    den = v1 + v2
    info = jnp.where(lane == 0, (i1 - MOE_GROUPS).astype(F32), 0.0)
    info = jnp.where(lane == 1, (i2 - MOE_GROUPS).astype(F32), info)
    info = jnp.where(lane == 2, pg * v1 / den, info)
    info = jnp.where(lane == 3, pg * v2 / den, info)
    info_ref[...] = info


def _router(h, g, w_r, b_r):
    t, d = h.shape
    tm = _row_tile(t)
    row = lambda wd: pl.BlockSpec((tm, wd), lambda i: (i, 0))
    return pl.pallas_call(
        _router_kernel,
        grid=(t // tm,),
        in_specs=[row(d), _full((1, d)), _full((d, LANES)), _full((1, LANES))],
        out_specs=[row(d), row(LANES)],
        out_shape=[jax.ShapeDtypeStruct((t, d), BF16), jax.ShapeDtypeStruct((t, LANES), F32)],
        compiler_params=_cp("parallel"),
        name="router",
    )(h, g, w_r, b_r)


def _moe_ffn_kernel(be_ref, nu_ref, x_ref, w1_ref, w3_ref, w2_ref, o_ref):
    i = pl.program_id(0)

    @pl.when(i < nu_ref[0])
    def _():
        x = x_ref[...]
        hdn = _silu(_dot(x, w1_ref[0])) * _dot(x, w3_ref[0])
        o_ref[...] = _dot(hdn.astype(BF16), w2_ref[0])

    @pl.when(i >= nu_ref[0])
    def _():
        o_ref[...] = jnp.zeros_like(o_ref)


def _moe_ffn(xp, blk_e, nused, w1, w3, w2, rb):
    n, d = xp.shape
    ff = w1.shape[2]
    nblk = n // rb
    gs = pltpu.PrefetchScalarGridSpec(
        num_scalar_prefetch=2,
        grid=(nblk,),
        in_specs=[pl.BlockSpec((rb, d), lambda i, be, nu: (i, 0)),
                  pl.BlockSpec((1, d, ff), lambda i, be, nu: (be[i], 0, 0)),
                  pl.BlockSpec((1, d, ff), lambda i, be, nu: (be[i], 0, 0)),
                  pl.BlockSpec((1, ff, d), lambda i, be, nu: (be[i], 0, 0))],
        out_specs=pl.BlockSpec((rb, d), lambda i, be, nu: (i, 0)),
    )
    return pl.pallas_call(
        _moe_ffn_kernel,
        grid_spec=gs,
        out_shape=jax.ShapeDtypeStruct((n, d), F32),
        compiler_params=_cp("arbitrary"),
        name="moe_ffn",
    )(blk_e, nused, xp, w1, w3, w2)


def _moe(h, g, w_r, b_r, w1, w3, w2):
    t, d = h.shape
    xn, info = _router(h, g, w_r, b_r)
    eid = info[:, 0:2].astype(jnp.int32)
    gate = info[:, 2:4]
    ne = MOE_EXPERTS
    tk = 2 * t
    rb = 256 if t >= 4096 else 32
    flat = eid.reshape(-1)
    order = jnp.argsort(flat)
    se = flat[order]
    tok = order // 2
    counts = jnp.bincount(flat, length=ne)
    padc = (counts + rb - 1) // rb * rb
    pend = jnp.cumsum(padc)
    pstart = pend - padc
    cstart = jnp.cumsum(counts) - counts
    dest = (pstart[se] + jnp.arange(tk) - cstart[se]).astype(jnp.int32)
    nblk = -(-tk // rb) + ne
    rows = jnp.zeros((nblk * rb,), jnp.int32).at[dest].set(tok.astype(jnp.int32))
    xp = xn[rows]
    blk_e = jnp.minimum(jnp.searchsorted(pend, jnp.arange(nblk) * rb, side='right'), ne - 1).astype(jnp.int32)
    nused = (pend[-1:] // rb).astype(jnp.int32)
    yp = _moe_ffn(xp, blk_e, nused, w1, w3, w2, rb)
    slot = jnp.zeros((tk,), jnp.int32).at[order].set(dest).reshape(t, 2)
    return gate[:, 0:1] * yp[slot[:, 0]] + gate[:, 1:2] * yp[slot[:, 1]]


def _ple_kernel(h_ref, y_ref, p_ref, g_ref, wg_ref, wp_ref, fg_ref, o_ref, *, final):
    h = h_ref[...] + y_ref[...]
    gate = _sigmoid(_dot(_rms(h, g_ref[...]).astype(BF16), wg_ref[...]))
    out = h + gate * _dot(p_ref[...].astype(BF16), wp_ref[...])
    if final:
        out = _rms(out, fg_ref[...])
    o_ref[...] = out


def _ple(h, y, p, g, wg_bf, wp_bf, fg, final):
    t, d = h.shape
    pd = p.shape[1]
    tm = _row_tile(t)
    row = lambda wd: pl.BlockSpec((tm, wd), lambda i: (i, 0))
    return pl.pallas_call(
        functools.partial(_ple_kernel, final=final),
        grid=(t // tm,),
        in_specs=[row(d), row(d), row(pd), _full((1, d)), _full((d, d)), _full((pd, d)), _full((1, d))],
        out_specs=row(d),
        out_shape=jax.ShapeDtypeStruct((t, d), F32),
        compiler_params=_cp("parallel"),
        name="ple",
    )(h, y, p, g, wg_bf, wp_bf, fg)


_HD = HEAD_DIM
_C_NQ = NSA_HEADS * _HD
_C_KV = NSA_KV * _HD
_C_MQ = MOBA_HEADS * _HD
_C_MK = MOBA_KV * _HD
_O_NQ = 0
_O_CK = _O_NQ + _C_NQ
_O_CV = _O_CK + _C_KV
_O_SK = _O_CV + _C_KV
_O_SV = _O_SK + _C_KV
_O_WK = _O_SV + _C_KV
_O_WV = _O_WK + _C_KV
_O_MQ = _O_WV + _C_KV
_O_MK = _O_MQ + _C_MQ
_O_MV = _O_MK + _C_MK
_O_NG = _O_MV + _C_MK
_O_NQS = _O_NG + LANES
_O_SKS = _O_NQS + _C_NQ
_O_WKS = _O_SKS + _C_KV
_O_MQS = _O_WKS + _C_KV
_O_MKS = _O_MQS + _C_MQ
_C_TOTAL = _O_MKS + _C_MK


def _proj_c_kernel(x_ref, g_ref, w_ref, cos_ref, sin_ref, qu_ref, qr_ref, mq_ref, ng_ref, rows_ref, win_ref):
    xn = _rms(x_ref[...], g_ref[...]).astype(BF16)
    z = _dot(xn, w_ref[...])
    cs = cos_ref[...]
    sn = sin_ref[...]

    def rope(o, os, wd):
        reps = wd // LANES
        return z[:, o:o + wd] * jnp.tile(cs, (1, reps)) + z[:, os:os + wd] * jnp.tile(sn, (1, reps))

    qu_ref[...] = z[:, _O_NQ:_O_NQ + _C_NQ]
    qr_ref[...] = rope(_O_NQ, _O_NQS, _C_NQ)
    mq_ref[...] = rope(_O_MQ, _O_MQS, _C_MQ)
    ng_ref[...] = z[:, _O_NG:_O_NG + LANES]
    rows_ref[:, 0:2 * _C_KV] = z[:, _O_CK:_O_CK + 2 * _C_KV]
    rows_ref[:, 2 * _C_KV:3 * _C_KV] = rope(_O_SK, _O_SKS, _C_KV)
    rows_ref[:, 3 * _C_KV:4 * _C_KV] = z[:, _O_SV:_O_SV + _C_KV]
    rows_ref[:, 4 * _C_KV:4 * _C_KV + _C_MK] = rope(_O_MK, _O_MKS, _C_MK)
    rows_ref[:, 4 * _C_KV + _C_MK:] = z[:, _O_MV:_O_MV + _C_MK]
    win_ref[:, 0:_C_KV] = rope(_O_WK, _O_WKS, _C_KV)
    win_ref[:, _C_KV:] = z[:, _O_WV:_O_WV + _C_KV]


def _proj_c_weights(w):
    offs = np.concatenate([[0], np.cumsum(C_SPLITS)])
    nq, ck, cv, sk, sv, wk, wv, ng, mq, mk, mv = [w[:, offs[i]:offs[i + 1]] for i in range(len(C_SPLITS))]
    half = ROT_DIM // 2

    def swapped(m):
        d = m.shape[0]
        m3 = m.reshape(d, -1, _HD)
        out = jnp.concatenate([m3[..., half:ROT_DIM], m3[..., :half], jnp.zeros_like(m3[..., ROT_DIM:])], axis=-1)
        return out.reshape(d, -1)

    ngp = jnp.pad(ng, ((0, 0), (0, LANES - ng.shape[1])))
    return jnp.concatenate([nq, ck, cv, sk, sv, wk, wv, mq, mk, mv, ngp,
                            swapped(nq), swapped(sk), swapped(wk), swapped(mq), swapped(mk)], axis=1).astype(BF16)


def _rope_tables(pos):
    half = ROT_DIM // 2
    inv = ROPE_THETA ** (-jnp.arange(half, dtype=F32) / half)
    ang = pos.astype(F32)[:, None] * inv
    cos, sin = jnp.cos(ang), jnp.sin(ang)
    n = pos.shape[0]
    c64 = jnp.concatenate([cos, cos, jnp.ones((n, _HD - ROT_DIM), F32)], axis=1)
    s64 = jnp.concatenate([-sin, sin, jnp.zeros((n, _HD - ROT_DIM), F32)], axis=1)
    return jnp.tile(c64, (1, LANES // _HD)), jnp.tile(s64, (1, LANES // _HD))


def _proj_c(x, g, wc_bf, pos, seq):
    t, d = x.shape
    tm = _row_tile(t)
    tr = max(seq, tm)
    assert tr % tm == 0 and tr % seq == 0
    cs, sn = _rope_tables(jnp.tile(pos, tr // seq))
    ntab = tr // tm
    row = lambda wd: pl.BlockSpec((tm, wd), lambda i: (i, 0))
    tab = pl.BlockSpec((tm, LANES), lambda i: (i % ntab, 0))
    widths = [_C_NQ, _C_NQ, _C_MQ, LANES, ROW_W, 2 * _C_KV]
    return pl.pallas_call(
        _proj_c_kernel,
        grid=(t // tm,),
        in_specs=[row(d), _full((1, d)), _full((d, _C_TOTAL)), tab, tab],
        out_specs=[row(wd) for wd in widths],
        out_shape=[jax.ShapeDtypeStruct((t, wd), F32) for wd in widths],
        compiler_params=_cp("parallel"),
        name="proj_c",
    )(x, g, wc_bf, cs, sn)


def _gather_kernel(pt_ref, page_ref, new_ref, o_ref, *, npages):
    j = pl.program_id(1)

    @pl.when(j < npages)
    def _():
        o_ref[0] = page_ref[0]

    @pl.when(j == npages)
    def _():
        o_ref[0] = new_ref[0]

    @pl.when(j > npages)
    def _():
        o_ref[0] = jnp.zeros(o_ref.shape[1:], o_ref.dtype)


def _gather_pages(pool2, pt, layer, n_layers, new_rows, lk_pad):
    batch, npages = pt.shape
    ps = PAGE_SIZE
    newp = jnp.pad(new_rows, ((0, 0), (0, ps - new_rows.shape[1]), (0, 0)))
    nj = lk_pad // ps
    gs = pltpu.PrefetchScalarGridSpec(
        num_scalar_prefetch=1,
        grid=(batch, nj),
        in_specs=[pl.BlockSpec((1, ps, ROW_W),
                               lambda b, j, p: (p[b * npages + jnp.minimum(j, npages - 1)] * n_layers + layer, 0, 0)),
                  pl.BlockSpec((1, ps, ROW_W), lambda b, j, p: (b, 0, 0))],
        out_specs=pl.BlockSpec((1, ps, ROW_W), lambda b, j, p: (b, j, 0)),
    )
    return pl.pallas_call(
        functools.partial(_gather_kernel, npages=npages),
        grid_spec=gs,
        out_shape=jax.ShapeDtypeStruct((batch, lk_pad, ROW_W), F32),
        compiler_params=_cp("parallel", "arbitrary"),
        name="gather_pages",
    )(pt.reshape(-1), pool2, newp)


def _mm_kernel(a_ref, w_ref, o_ref):
    o_ref[...] = _dot(a_ref[...], w_ref[...])


def _mm(a_bf, w_bf):
    m, k = a_bf.shape
    n = w_bf.shape[1]
    tm = _row_tile(m)
    return pl.pallas_call(
        _mm_kernel,
        grid=(m // tm,),
        in_specs=[pl.BlockSpec((tm, k), lambda i: (i, 0)), _full((k, n))],
        out_specs=pl.BlockSpec((tm, n), lambda i: (i, 0)),
        out_shape=jax.ShapeDtypeStruct((m, n), F32),
        compiler_params=_cp("parallel"),
        name="mm",
    )(a_bf, w_bf)


def _cmp_fin_kernel(a_ref, pb_ref, w2_ref, o_ref, *, nseg):
    a = a_ref[0]
    hw = a.shape[1] // 2
    bias = pb_ref[0:1, :hw] + pb_ref[1:2, hw:]
    pre = a[:, :hw] + pltpu.roll(a[:, hw:], nseg - 1, axis=0) + bias
    o_ref[0] = _dot(_gelu(pre).astype(BF16), w2_ref[...])


def _compress_weights(w1k, w1v, w2k, w2v, pek, pev):
    span = NSA_CMP_LEN // NSA_CMP_STRIDE
    st = NSA_CMP_STRIDE
    hid = w1k.shape[1]
    slots = 2 * NSA_KV
    kinds = [0] * NSA_KV + [1] * NSA_KV
    w1 = [w1k.reshape(span, st, _HD, hid), w1v.reshape(span, st, _HD, hid)]
    pe = [pek.reshape(span, st, _HD), pev.reshape(span, st, _HD)]
    wbig = jnp.zeros((st, slots, _HD, span, slots, hid), F32)
    pex = jnp.zeros((8, st, slots, _HD), F32)
    w2big = jnp.zeros((slots, hid, slots, _HD), F32)
    for s in range(slots):
        kd = kinds[s]
        wbig = wbig.at[:, s, :, :, s, :].set(w1[kd].transpose(1, 2, 0, 3))
        pex = pex.at[0:span, :, s, :].set(pe[kd])
        w2big = w2big.at[s, :, s, :].set([w2k, w2v][kd])
    return (wbig.reshape(st * slots * _HD, span * slots * hid).astype(BF16),
            pex.reshape(8, st * slots * _HD).astype(BF16),
            w2big.reshape(slots * hid, slots * _HD).astype(BF16))


def _compress(cmpflat_bf, wbig, pex, w2big):
    batch, nseg, kdim = cmpflat_bf.shape
    a = _mm(cmpflat_bf.reshape(batch * nseg, kdim), wbig).reshape(batch, nseg, -1)
    pb = _mm(pex, wbig)
    n2 = a.shape[2]
    ow = w2big.shape[1]
    return pl.pallas_call(
        functools.partial(_cmp_fin_kernel, nseg=nseg),
        grid=(batch,),
        in_specs=[pl.BlockSpec((1, nseg, n2), lambda b: (b, 0, 0)), _full(pb.shape), _full(w2big.shape)],
        out_specs=pl.BlockSpec((1, nseg, ow), lambda b: (b, 0, 0)),
        out_shape=jax.ShapeDtypeStruct((batch, nseg, ow), F32),
        compiler_params=_cp("parallel"),
        name="compress_fin",
    )(a, pb, w2big)


def _block_mean_kernel(k_ref, o_ref):
    o_ref[0, 0] = jnp.sum(k_ref[0], axis=0, keepdims=True) * (1.0 / MOBA_BLOCK)


def _block_mean(km, nbf):
    batch, _, wd = km.shape
    return pl.pallas_call(
        _block_mean_kernel,
        grid=(batch, nbf),
        in_specs=[pl.BlockSpec((1, MOBA_BLOCK, wd), lambda b, n: (b, n, 0))],
        out_specs=pl.BlockSpec((1, 1, 1, wd), lambda b, n: (b, n, 0, 0)),
        out_shape=jax.ShapeDtypeStruct((batch, nbf, 1, wd), F32),
        compiler_params=_cp("parallel", "parallel"),
        name="block_mean",
    )(km)


def _top_rounds(score, row_id, k):
    far = 1 << 20
    sel = jnp.zeros(score.shape, F32)
    cur = score
    for _ in range(k):
        m = jnp.max(cur, axis=0, keepdims=True)
        idx = jnp.min(jnp.where(cur == m, row_id, far), axis=0, keepdims=True)
        pick = row_id == idx
        sel = jnp.where(pick, 1.0, sel)
        cur = jnp.where(pick, -jnp.inf, cur)
    return sel


def _softmax0(s, mask):
    s = jnp.where(mask, s, NEG)
    e = jnp.where(mask, jnp.exp(s - jnp.max(s, axis=0, keepdims=True)), 0.0)
    return e / jnp.maximum(jnp.sum(e, axis=0, keepdims=True), 1e-30)


def _nsa_kernel(qu_ref, qr_ref, gl_ref, kc_ref, vct_ref, mt_ref, ks_ref, vst_ref, kw_ref, vwt_ref,
                o_ref, sel_scr, *, tq, nc, n_sel, qpos0, nkt_max, wl, wpos0, w_fixed):
    qi = pl.program_id(1)
    n = NSA_GRP * tq
    tk = KEY_TILE
    t0 = qpos0 + qi * tq
    lane = lax.broadcasted_iota(jnp.int32, (1, n), 1)
    tpos = t0 + (lane & (tq - 1))
    tpos1 = tpos[:, :tq]
    nc_pad = kc_ref.shape[2]
    nsel_pad = mt_ref.shape[0]
    ci = lax.broadcasted_iota(jnp.int32, (nc_pad, 1), 0)
    bi = lax.broadcasted_iota(jnp.int32, (nsel_pad, tq), 0)
    cur = tpos1 >> _SEL_SHIFT
    nkt = jnp.minimum((t0 + tq - 1) // tk + 1, nkt_max)
    if w_fixed:
        wstart = 0
    else:
        wstart = pl.multiple_of(jnp.maximum(t0 - NSA_WINDOW, 0), LANES)
    wpos = wpos0 + wstart + lax.broadcasted_iota(jnp.int32, (wl, 1), 0)
    k_top = min(NSA_TOPN, n_sel)

    for g in range(NSA_KV):
        qu = qu_ref[0, 0, g]
        qr = qr_ref[0, 0, g].astype(BF16)
        cmask = ((ci * NSA_CMP_STRIDE + (NSA_CMP_LEN - 1)) <= tpos) & (ci < nc)
        pc = _softmax0(_dot_hi(kc_ref[0, g], qu), cmask)
        o_c = _dot(vct_ref[0, g].astype(BF16), pc.astype(BF16))
        imp = pc[:, 0:tq]
        for r in range(1, NSA_GRP):
            imp = imp + pc[:, r * tq:(r + 1) * tq]
        score = _dot_hi(mt_ref[...], imp)
        forced = (bi == 0) | (bi == cur) | (bi == cur - 1)
        score = jnp.where(forced, BIG, score)
        score = jnp.where(bi <= cur, score, -BIG)
        score = jnp.where(bi < n_sel, score, -3e38)
        sel = _top_rounds(score, bi, k_top)
        sel_scr[g] = jnp.where(bi <= cur, sel, 0.0)

        def body(j, carry):
            m, l, acc = carry
            kt = ks_ref[0, g, pl.ds(pl.multiple_of(j * tk, tk), tk), :].astype(BF16)
            s = _dot(kt, qr)
            nb = tk // NSA_SEL_BLOCK
            pieces = [jnp.broadcast_to(sel_scr[g, pl.ds(j * nb + u, 1), :], (NSA_SEL_BLOCK, tq)) for u in range(nb)]
            mk = jnp.concatenate(pieces, axis=0)
            mk = jnp.concatenate([mk] * NSA_GRP, axis=1)
            kpos = j * tk + lax.broadcasted_iota(jnp.int32, (tk, 1), 0)
            mask = (mk > 0.5) & (kpos <= tpos)
            s = jnp.where(mask, s, NEG)
            m_new = jnp.maximum(m, jnp.max(s, axis=0, keepdims=True))
            alpha = jnp.exp(m - m_new)
            p = jnp.where(mask, jnp.exp(s - m_new), 0.0)
            l = alpha * l + jnp.sum(p, axis=0, keepdims=True)
            acc = alpha * acc + _dot(vst_ref[0, g, j].astype(BF16), p.astype(BF16))
            return m_new, l, acc

        init = (jnp.full((1, n), NEG, F32), jnp.zeros((1, n), F32), jnp.zeros((_HD, n), F32))
        _, l, acc = lax.fori_loop(0, nkt, body, init)
        o_s = acc / jnp.maximum(l, 1e-30)

        kw = kw_ref[0, g, pl.ds(wstart, wl), :].astype(BF16)
        wmask = (wpos <= tpos) & (wpos >= tpos - NSA_WINDOW) & (wpos >= 0)
        pw = _softmax0(_dot(kw, qr), wmask).astype(BF16)
        o_w = jnp.zeros((_HD, n), F32)
        wt0 = wstart // LANES
        for u in range(wl // LANES):
            o_w = o_w + _dot(vwt_ref[0, g, wt0 + u].astype(BF16), pw[u * LANES:(u + 1) * LANES])

        gate = _sigmoid(gl_ref[0, 0, g])
        o_ref[0, 0, g] = gate[0:1] * o_c + gate[1:2] * o_s + gate[2:3] * o_w


def _moba_kernel(q_ref, kb_ref, km_ref, vmt_ref, o_ref, sel_scr, *, tq, nbf, qpos0, nkt_max):
    qi = pl.program_id(1)
    n = MOBA_GRP * tq
    tk = KEY_TILE
    t0 = qpos0 + qi * tq
    lane = lax.broadcasted_iota(jnp.int32, (1, n), 1)
    tpos = t0 + (lane & (tq - 1))
    own = t0 // MOBA_BLOCK
    nbf_pad = kb_ref.shape[2]
    ni = lax.broadcasted_iota(jnp.int32, (nbf_pad, n), 0)
    kk = min(MOBA_TOPK, nbf)
    nkt = jnp.minimum(own + 1, nkt_max)
    scale = _HD ** -0.5

    for g in range(MOBA_KV):
        q = q_ref[0, 0, g]
        qb = (q * scale).astype(BF16)
        gs = _dot_hi(kb_ref[0, g], q)
        gs = jnp.where(ni < own, gs, -BIG)
        gs = jnp.where(ni < nbf, gs, -3e38)
        sel = _top_rounds(gs, ni, kk)
        sel_scr[g] = jnp.where((ni < own) & (ni < nbf), sel, 0.0)

        def body(j, carry):
            m, l, acc = carry
            kt = km_ref[0, g, pl.ds(pl.multiple_of(j * tk, tk), tk), :].astype(BF16)
            s = _dot(kt, qb)
            kpos = j * tk + lax.broadcasted_iota(jnp.int32, (tk, 1), 0)
            picked = jnp.broadcast_to(sel_scr[g, pl.ds(j, 1), :], (tk, n)) > 0.5
            mask = picked | (((kpos >> _MOBA_SHIFT) == own) & (kpos <= tpos))
            s = jnp.where(mask, s, NEG)
            m_new = jnp.maximum(m, jnp.max(s, axis=0, keepdims=True))
            alpha = jnp.exp(m - m_new)
            p = jnp.where(mask, jnp.exp(s - m_new), 0.0)
            l = alpha * l + jnp.sum(p, axis=0, keepdims=True)
            acc = alpha * acc + _dot(vmt_ref[0, g, j].astype(BF16), p.astype(BF16))
            return m_new, l, acc

        init = (jnp.full((1, n), NEG, F32), jnp.zeros((1, n), F32), jnp.zeros((_HD, n), F32))
        _, l, acc = lax.fori_loop(0, nkt, body, init)
        o_ref[0, 0, g] = acc / jnp.maximum(l, 1e-30)


def _to_qt(x, batch, nt, tq, groups, rep):
    x = x.reshape(batch, nt, tq, groups, rep, _HD).transpose(0, 1, 3, 5, 4, 2)
    return x.reshape(batch, nt, groups, _HD, rep * tq)


def _from_qt(x, batch, nt, tq, groups, rep):
    x = x.reshape(batch, nt, groups, _HD, rep, tq).transpose(0, 1, 5, 2, 4, 3)
    return x.reshape(batch, nt * tq, groups * rep * _HD)


def _head_major(x, heads):
    b, l, _ = x.shape
    return x.reshape(b, l, heads, _HD).transpose(0, 2, 1, 3)


def _vt_tiles(x, heads, tile):
    b, l, _ = x.shape
    return x.reshape(b, l // tile, tile, heads, _HD).transpose(0, 3, 1, 4, 2)


def _sel_score_matrix(nsel_pad, nc_pad, nc):
    ratio = NSA_SEL_BLOCK // NSA_CMP_STRIDE
    span = NSA_CMP_LEN // NSA_CMP_STRIDE
    mt = np.zeros((nsel_pad, nc_pad), np.float32)
    for j in range(nsel_pad):
        for o in range(-(span - 1), ratio):
            c = j * ratio + o
            wt = sum(1 for m in range(ratio) for q in range(span) if m - q == o)
            if 0 <= c < nc:
                mt[j, c] = wt
    return jnp.asarray(mt)


def _mixer_c_attention(qu, qr, mq, ng, rfull, wfull, kcvc, batch, lq, lt, qpos0, wpos0):
    tk = KEY_TILE
    tq = LANES if lq % LANES == 0 else 64
    lqp = -(-lq // tq) * tq
    nt = lqp // tq
    lk_pad = rfull.shape[1]
    nkt_max = lk_pad // tk
    nc = (lt - NSA_CMP_LEN) // NSA_CMP_STRIDE + 1
    nc_pad = kcvc.shape[1]
    n_sel = -(-lt // NSA_SEL_BLOCK)
    nsel_pad = -(-max(n_sel, nkt_max * (tk // NSA_SEL_BLOCK)) // 8) * 8
    nbf = lt // MOBA_BLOCK
    nbf_pad = -(-max(nbf, nkt_max) // 8) * 8
    scale = _HD ** -0.5

    def padq(x):
        x = x.reshape(batch, lq, -1)
        return jnp.pad(x, ((0, 0), (0, lqp - lq), (0, 0))).reshape(batch * lqp, -1)

    qu_t = _to_qt(padq(qu) * scale, batch, nt, tq, NSA_KV, NSA_GRP)
    qr_t = _to_qt(padq(qr) * scale, batch, nt, tq, NSA_KV, NSA_GRP)
    mq_t = _to_qt(padq(mq), batch, nt, tq, MOBA_KV, MOBA_GRP)
    gl = padq(ng)[:, :NSA_HEADS * 3].reshape(batch, nt, tq, NSA_KV, NSA_GRP, 3).transpose(0, 1, 3, 5, 4, 2)
    gl = gl.reshape(batch, nt, NSA_KV, 3, NSA_GRP * tq)

    kc = _head_major(kcvc[:, :, 0:_C_KV], NSA_KV)
    vct = _head_major(kcvc[:, :, _C_KV:2 * _C_KV], NSA_KV).transpose(0, 1, 3, 2)
    mt = _sel_score_matrix(nsel_pad, nc_pad, nc)
    ks = _head_major(rfull[:, :, 2 * _C_KV:3 * _C_KV], NSA_KV).astype(BF16)
    vst = _vt_tiles(rfull[:, :, 3 * _C_KV:4 * _C_KV], NSA_KV, tk).astype(BF16)
    km_flat = rfull[:, :, 4 * _C_KV:4 * _C_KV + _C_MK]
    km = _head_major(km_flat, MOBA_KV).astype(BF16)
    vmt = _vt_tiles(rfull[:, :, 4 * _C_KV + _C_MK:], MOBA_KV, tk).astype(BF16)
    kw = _head_major(wfull[:, :, 0:_C_KV], NSA_KV).astype(BF16)
    vwt = _vt_tiles(wfull[:, :, _C_KV:], NSA_KV, LANES).astype(BF16)
    lw_pad = wfull.shape[1]
    w_fixed = nt == 1
    wl = lw_pad if w_fixed else NSA_WINDOW + tq
    assert wl % LANES == 0 and lw_pad >= wl

    kb = _block_mean(km_flat, nbf)[:, :, 0, :]
    kb = jnp.pad(kb, ((0, 0), (0, nbf_pad - nbf), (0, 0)))
    kb = _head_major(kb, MOBA_KV)

    n_n = NSA_GRP * tq
    qspec = pl.BlockSpec((1, 1, NSA_KV, _HD, n_n), lambda b, i: (b, i, 0, 0, 0))
    per_b = lambda shp: pl.BlockSpec((1,) + shp, lambda b, i: (b,) + (0,) * len(shp))
    o_n = pl.pallas_call(
        functools.partial(_nsa_kernel, tq=tq, nc=nc, n_sel=n_sel, qpos0=qpos0, nkt_max=nkt_max, wl=wl,
                          wpos0=wpos0, w_fixed=w_fixed),
        grid=(batch, nt),
        in_specs=[qspec, qspec, pl.BlockSpec((1, 1, NSA_KV, 3, n_n), lambda b, i: (b, i, 0, 0, 0)),
                  per_b(kc.shape[1:]), per_b(vct.shape[1:]), _full(mt.shape),
                  per_b(ks.shape[1:]), per_b(vst.shape[1:]), per_b(kw.shape[1:]), per_b(vwt.shape[1:])],
        out_specs=qspec,
        out_shape=jax.ShapeDtypeStruct((batch, nt, NSA_KV, _HD, n_n), F32),
        scratch_shapes=[pltpu.VMEM((NSA_KV, nsel_pad, tq), F32)],
        compiler_params=_cp("parallel", "arbitrary"),
        name="nsa",
    )(qu_t, qr_t, gl, kc, vct, mt, ks, vst, kw, vwt)

    n_m = MOBA_GRP * tq
    mspec = pl.BlockSpec((1, 1, MOBA_KV, _HD, n_m), lambda b, i: (b, i, 0, 0, 0))
    o_m = pl.pallas_call(
        functools.partial(_moba_kernel, tq=tq, nbf=nbf, qpos0=qpos0, nkt_max=nkt_max),
        grid=(batch, nt),
        in_specs=[mspec, per_b(kb.shape[1:]), per_b(km.shape[1:]), per_b(vmt.shape[1:])],
        out_specs=mspec,
        out_shape=jax.ShapeDtypeStruct((batch, nt, MOBA_KV, _HD, n_m), F32),
        scratch_shapes=[pltpu.VMEM((MOBA_KV, nbf_pad, n_m), F32)],
        compiler_params=_cp("parallel", "arbitrary"),
        name="moba",
    )(mq_t, kb, km, vmt)

    o_n = _from_qt(o_n, batch, nt, tq, NSA_KV, NSA_GRP)[:, :lq].reshape(batch * lq, -1)
    o_m = _from_qt(o_m, batch, nt, tq, MOBA_KV, MOBA_GRP)[:, :lq].reshape(batch * lq, -1)
    return o_n, o_m


def _prep_weights(prm):
    w = {}
    w['w_in_a'] = prm['w_in_a'].astype(BF16)
    w['w_out_a'] = prm['w_out_a'].astype(BF16)
    w['w_out_c'] = prm['w_out_c'].astype(BF16)
    w['w_in_c'] = [_proj_c_weights(prm['w_in_c'][c]) for c in range(prm['w_in_c'].shape[0])]
    w['cmp'] = [_compress_weights(prm['cmp_w1_k'][c], prm['cmp_w1_v'][c], prm['cmp_w2_k'][c], prm['cmp_w2_v'][c],
                                  prm['cmp_pe_k'][c], prm['cmp_pe_v'][c]) for c in range(prm['w_in_c'].shape[0])]
    depth, d, _ = prm['router_c_w'].shape
    wr = jnp.concatenate([prm['router_c_w'], prm['router_f_w'].transpose(0, 2, 1, 3).reshape(depth, d, -1)], axis=2)
    w['router_w'] = jnp.pad(wr, ((0, 0), (0, 0), (0, LANES - wr.shape[2])))
    br = jnp.concatenate([prm['router_c_b'], prm['router_f_b'].reshape(depth, -1)], axis=1)
    w['router_b'] = jnp.pad(br, ((0, 0), (0, LANES - br.shape[1])))[:, None, :]
    w['moe_w1'] = prm['moe_w1'].astype(BF16)
    w['moe_w3'] = prm['moe_w3'].astype(BF16)
    w['moe_w2'] = prm['moe_w2'].astype(BF16)
    w['ple_w'] = prm['ple_w'].astype(BF16)
    w['ple_gate_w'] = prm['ple_gate_w'].astype(BF16)
    w['lb'] = jnp.cumsum(jax.nn.softmax(prm['hgrn_lb'].astype(F32), axis=0), axis=0)
    return w


def _forward(x, p, pos, prm, w, ctx):
    batch, seq, d = x.shape
    t = batch * seq
    depth = p.shape[0]
    h = x.reshape(t, d)
    outs = {}
    for i in range(depth):
        gmix = prm['norm_mix'][i][None, :]
        if i % 2 == 0:
            a = i // 2
            lb512 = jnp.tile(w['lb'][a], HG_HEADS)[None, :]
            q, k, lf, iv, sg, gu, vn = _proj_a(h, gmix, w['w_in_a'][a], lb512, prm['cm_vnorm'][a][None, :])
            if ctx is None:
                st0 = jnp.zeros((batch, HG_WIDTH, HG_WIDTH), F32)
            else:
                s0 = ctx['state_hgrn'][a].astype(F32)
                eye = jnp.eye(HG_HEADS, dtype=F32)
                st0 = jnp.einsum('bhde,hg->bhegd', s0, eye).reshape(batch, HG_WIDTH, HG_WIDTH)
            o, st = _hgrn(q, k, lf, iv, sg, st0, prm['hgrn_onorm'][a][None, :], batch, seq)
            st5 = st.reshape(batch, HG_HEADS, HG_DK, HG_HEADS, HG_DK)
            s_new = jnp.stack([st5[:, hh, :, hh, :] for hh in range(HG_HEADS)], axis=1).transpose(0, 1, 3, 2)
            cm = _cmix(vn, gu, prm['cm_ws'][a], prm['cm_bs'][a].T, batch, seq)
            h = _out_proj(h, o, cm, w['w_out_a'][a])
            outs.setdefault('hg', []).append(s_new)
            outs.setdefault('cm', []).append(vn.reshape(batch, seq, CM_WIDTH))
        else:
            c = i // 2
            qu, qr, mq, ng, rows, win = _proj_c(h, gmix, w['w_in_c'][c], pos, seq)
            rows3 = rows.reshape(batch, seq, ROW_W)
            win3 = win.reshape(batch, seq, 2 * _C_KV)
            wbig, pex, w2big = w['cmp'][c]
            tk = KEY_TILE
            if ctx is None:
                lt = seq
                assert seq % tk == 0
                rfull = rows3
                wfull = win3
                qpos0, wpos0 = 0, 0
                new_win = win3[:, seq - min(NSA_WINDOW, seq):]
            else:
                pool, pt = ctx['cache_kv'], ctx['page_table']
                past = pt.shape[1] * PAGE_SIZE
                lt = past + seq
                tq = 64
                lk_pad = -(-(past + tq) // tk) * tk
                n_layers = pool.shape[1]
                pool2 = pool.reshape(pool.shape[0] * n_layers, PAGE_SIZE, ROW_W)
                rfull = _gather_pages(pool2, pt, c, n_layers, rows3, lk_pad)
                wb = ctx['cache_win'][c].reshape(batch, -1, 2 * _C_KV)
                wcat = jnp.concatenate([wb, win3], axis=1)
                lw_pad = -(-(NSA_WINDOW + tq) // LANES) * LANES
                wfull = jnp.pad(wcat, ((0, 0), (0, lw_pad - wcat.shape[1]), (0, 0)))
                qpos0, wpos0 = past, past - wb.shape[1]
                new_win = win3
            nc = (lt - NSA_CMP_LEN) // NSA_CMP_STRIDE + 1
            nseg = nc + NSA_CMP_LEN // NSA_CMP_STRIDE - 1
            cmpflat = rfull[:, :nseg * NSA_CMP_STRIDE, 0:2 * _C_KV].reshape(batch, nseg, -1).astype(BF16)
            kcvc = _compress(cmpflat, wbig, pex, w2big)
            o_n, o_m = _mixer_c_attention(qu, qr, mq, ng, rfull, wfull, kcvc, batch, seq, lt, qpos0, wpos0)
            h = _out_proj(h, o_n, o_m, w['w_out_c'][c])
            outs.setdefault('kv', []).append(rows3.reshape(batch, seq, KV_ROW_HEADS, _HD))
            outs.setdefault('win', []).append(new_win.reshape(batch, -1, 2 * NSA_KV, _HD))
        y = _moe(h, prm['norm_ffn'][i][None, :], w['router_w'][i], w['router_b'][i],
                 w['moe_w1'][i], w['moe_w3'][i], w['moe_w2'][i])
        h = _ple(h, y, p[i].reshape(t, -1), prm['norm_ple'][i][None, :], w['ple_gate_w'][i], w['ple_w'][i],
                 prm['final_norm'][None, :], final=(i == depth - 1))
    return h.reshape(batch, seq, d), outs


def kernel(x_prompt, x_sample, cache_kv, cache_win, state_hgrn, page_table, p_prompt, p_sample,
           norm_mix, norm_ffn, norm_ple, final_norm, w_in_a, w_out_a, hgrn_lb, hgrn_onorm, cm_vnorm,
           cm_ws, cm_bs, w_in_c, w_out_c, cmp_pe_k, cmp_w1_k, cmp_w2_k, cmp_pe_v, cmp_w1_v, cmp_w2_v,
           router_c_w, router_c_b, router_f_w, router_f_b, moe_w1, moe_w3, moe_w2, ple_w, ple_gate_w):
    prm = dict(norm_mix=norm_mix, norm_ffn=norm_ffn, norm_ple=norm_ple, final_norm=final_norm,
               w_in_a=w_in_a, w_out_a=w_out_a, hgrn_lb=hgrn_lb, hgrn_onorm=hgrn_onorm, cm_vnorm=cm_vnorm,
               cm_ws=cm_ws, cm_bs=cm_bs, w_in_c=w_in_c, w_out_c=w_out_c, cmp_pe_k=cmp_pe_k,
               cmp_w1_k=cmp_w1_k, cmp_w2_k=cmp_w2_k, cmp_pe_v=cmp_pe_v, cmp_w1_v=cmp_w1_v, cmp_w2_v=cmp_w2_v,
               router_c_w=router_c_w, router_c_b=router_c_b, router_f_w=router_f_w, router_f_b=router_f_b,
               moe_w1=moe_w1, moe_w3=moe_w3, moe_w2=moe_w2, ple_w=ple_w, ple_gate_w=ple_gate_w)
    w = _prep_weights(prm)
    past = page_table.shape[1] * PAGE_SIZE
    y_p, o_p = _forward(x_prompt, p_prompt, jnp.arange(x_prompt.shape[1]), prm, w, None)
    ctx = dict(cache_kv=cache_kv, cache_win=cache_win, state_hgrn=state_hgrn, page_table=page_table)
    y_s, o_s = _forward(x_sample, p_sample, past + jnp.arange(x_sample.shape[1]), prm, w, ctx)
    return (y_p, y_s,
            jnp.stack(o_p['kv'], axis=1), jnp.stack(o_s['kv'], axis=1),
            jnp.stack(o_p['win'], axis=0), jnp.stack(o_s['win'], axis=0),
            jnp.stack(o_p['hg'], axis=0), jnp.stack(o_s['hg'], axis=0),
            jnp.stack(o_s['cm'], axis=0))
```

```python
import functools
import math

import numpy as np
import jax
import jax.numpy as jnp
from jax import lax
from jax.experimental import pallas as pl
from jax.experimental.pallas import tpu as pltpu

F32 = jnp.float32
BF16 = jnp.bfloat16
HI = lax.Precision.HIGHEST

PAGE_SIZE = 128
HEAD_DIM = 64
ROT_DIM = HEAD_DIM // 4
ROPE_THETA = 500000.0
HG_HEADS = 8
HG_DK = 64
HG_WIDTH = HG_HEADS * HG_DK
HG_CHUNK = 64
CM_GROUPS = 4
CM_GW = 128
CM_WIDTH = CM_GROUPS * CM_GW
CM_CHUNK = 128
NSA_HEADS = 8
NSA_KV = 2
NSA_GRP = NSA_HEADS // NSA_KV
NSA_CMP_LEN = 32
NSA_CMP_STRIDE = 16
NSA_SEL_BLOCK = 64
NSA_TOPN = 16
NSA_WINDOW = 512
MOBA_HEADS = 8
MOBA_KV = 4
MOBA_GRP = MOBA_HEADS // MOBA_KV
MOBA_BLOCK = 256
MOBA_TOPK = 3
MOE_GROUPS = 4
MOE_EPG = 8
MOE_EXPERTS = MOE_GROUPS * MOE_EPG
KV_ROW_HEADS = 16
ROW_W = KV_ROW_HEADS * HEAD_DIM
A_SPLITS = [HG_WIDTH] * 4 + [CM_WIDTH] * 2
C_SPLITS = [NSA_HEADS * HEAD_DIM] + [NSA_KV * HEAD_DIM] * 6 + [NSA_HEADS * 3, MOBA_HEADS * HEAD_DIM,
                                                              MOBA_KV * HEAD_DIM, MOBA_KV * HEAD_DIM]
NEG = -1e30
BIG = 1e9
EPS = 1e-6
KEY_TILE = 256
LANES = 128
_MOBA_SHIFT = MOBA_BLOCK.bit_length() - 1
_SEL_SHIFT = NSA_SEL_BLOCK.bit_length() - 1
VMEM_LIMIT = 56 * 1024 * 1024


def _cp(*sem):
    return pltpu.CompilerParams(dimension_semantics=sem, vmem_limit_bytes=VMEM_LIMIT)


def _sigmoid(x):
    return 1.0 / (1.0 + jnp.exp(-x))


def _silu(x):
    return x * _sigmoid(x)


def _gelu(x):
    return 0.5 * x * (1.0 + jnp.tanh(math.sqrt(2.0 / math.pi) * (x + 0.044715 * (x * x * x))))


def _rms(x, g):
    return x * lax.rsqrt(jnp.mean(x * x, axis=-1, keepdims=True) + EPS) * g


def _dot(a, b):
    return jnp.dot(a, b, preferred_element_type=F32)


def _dot_hi(a, b):
    return jnp.dot(a, b, precision=HI, preferred_element_type=F32)


def _full(shape):
    n = len(shape)
    return pl.BlockSpec(shape, lambda *_: (0,) * n)


def _row_tile(t):
    for tm in (256, 128, 64, 32, 16, 8):
        if t % tm == 0:
            return tm
    raise ValueError(f"token count {t} is not a multiple of 8")


def _proj_a_kernel(x_ref, g_ref, w_ref, lb_ref, vg_ref, q_ref, k_ref, lf_ref, iv_ref, sg_ref, gu_ref, vn_ref):
    xn = _rms(x_ref[...], g_ref[...]).astype(BF16)
    z = _dot(xn, w_ref[...])
    w = HG_WIDTH
    q_ref[...] = _silu(z[:, 0:w])
    lb = lb_ref[...]
    f = lb + (1.0 - lb) * _sigmoid(z[:, w:2 * w])
    k_ref[...] = 1.0 - f
    lf_ref[...] = jnp.log(f)
    iv_ref[...] = z[:, 2 * w:3 * w]
    sg_ref[...] = _silu(z[:, 3 * w:4 * w])
    gu_ref[...] = _gelu(z[:, 4 * w:4 * w + CM_WIDTH])
    v = _gelu(z[:, 4 * w + CM_WIDTH:])
    for gi in range(CM_GROUPS):
        sl = slice(gi * CM_GW, (gi + 1) * CM_GW)
        vn_ref[:, sl] = _rms(v[:, sl], vg_ref[:, sl])


def _proj_a(x, g, w_bf, lb512, vgain):
    t, d = x.shape
    tm = _row_tile(t)
    n = w_bf.shape[1]
    row = lambda wd: pl.BlockSpec((tm, wd), lambda i: (i, 0))
    outs = [jax.ShapeDtypeStruct((t, HG_WIDTH), F32)] * 5 + [jax.ShapeDtypeStruct((t, CM_WIDTH), F32)] * 2
    return pl.pallas_call(
        _proj_a_kernel,
        grid=(t // tm,),
        in_specs=[row(d), _full((1, d)), _full((d, n)), _full((1, HG_WIDTH)), _full((1, CM_WIDTH))],
        out_specs=[row(HG_WIDTH)] * 5 + [row(CM_WIDTH)] * 2,
        out_shape=outs,
        compiler_params=_cp("parallel"),
        name="proj_a",
    )(x, g, w_bf, lb512, vgain)


def _hgrn_kernel(q_ref, k_ref, lf_ref, v_ref, sg_ref, s0_ref, gain_ref, bones_ref, bmask_ref, tri_ref,
                 o_ref, sout_ref, st_scr, p_scr, *, chunk):
    c = pl.program_id(1)
    w = HG_WIDTH
    hw = w // 2

    @pl.when(c == 0)
    def _():
        st_scr[...] = s0_ref[0]

    q = q_ref[0]
    k = k_ref[0]
    v = v_ref[0]
    b = _dot_hi(tri_ref[...], lf_ref[0])
    bones = bones_ref[...]

    def head_sum(x):
        xb = x.astype(BF16)
        return jnp.concatenate([_dot(xb[:, :hw], bones), _dot(xb[:, hw:], bones)], axis=1)

    o_rows = []
    for blk in range(chunk // 8):
        s_len = 8 * (blk + 1)
        bs = b[:s_len]
        ks = k[:s_len]
        vs = v[:s_len]
        row_id = lax.broadcasted_iota(jnp.int32, (s_len, w), 0)
        for t in range(8):
            r = 8 * blk + t
            diff = jnp.where(row_id <= r, b[r:r + 1, :] - bs, NEG)
            p_scr[t * s_len:(t + 1) * s_len, :] = jnp.exp(diff) * q[r:r + 1, :] * ks
        att = head_sum(p_scr[0:8 * s_len, :])
        for t in range(8):
            o_rows.append(jnp.sum(att[t * s_len:(t + 1) * s_len] * vs, axis=0, keepdims=True))
    o_intra = jnp.concatenate(o_rows, axis=0)

    st = st_scr[...]
    qe = (q * jnp.exp(b)).astype(BF16)
    o = o_intra + lax.dot_general(qe, st.astype(BF16), (((1,), (1,)), ((), ())), preferred_element_type=F32)

    b_end = b[chunk - 1:chunk, :]
    kd = (k * jnp.exp(b_end - b)).astype(BF16)
    upd = lax.dot_general(v.astype(BF16), kd, (((0,), (0,)), ((), ())), preferred_element_type=F32)
    st_new = st * jnp.exp(b_end) + upd * bmask_ref[...]
    st_scr[...] = st_new

    o2 = o * o
    hi = o2.astype(BF16).astype(F32)
    ms = (head_sum(hi) + head_sum(o2 - hi)) * (1.0 / HG_DK)
    o_ref[0] = o * lax.rsqrt(ms + EPS) * gain_ref[...] * sg_ref[0]

    @pl.when(c == pl.num_programs(1) - 1)
    def _():
        sout_ref[0] = st_new


def _block_ones(n, blk, dtype):
    i = np.arange(n) // blk
    return jnp.asarray((i[:, None] == i[None, :]).astype(np.float32), dtype)


def _hgrn(q, k, lf, v, sg, st0, gain, batch, seq):
    w = HG_WIDTH
    chunk = math.gcd(seq, HG_CHUNK)
    nch = seq // chunk
    r3 = lambda a: a.reshape(batch, seq, w)
    tile = pl.BlockSpec((1, chunk, w), lambda b, c: (b, c, 0))
    state = pl.BlockSpec((1, w, w), lambda b, c: (b, 0, 0))
    tri = jnp.asarray(np.tril(np.ones((chunk, chunk), np.float32)))
    o, st = pl.pallas_call(
        functools.partial(_hgrn_kernel, chunk=chunk),
        grid=(batch, nch),
        in_specs=[tile] * 5 + [state, _full((1, w)), _full((w // 2, w // 2)), _full((w, w)), _full((chunk, chunk))],
        out_specs=[tile, state],
        out_shape=[jax.ShapeDtypeStruct((batch, seq, w), F32), jax.ShapeDtypeStruct((batch, w, w), F32)],
        scratch_shapes=[pltpu.VMEM((w, w), F32), pltpu.VMEM((8 * chunk, w), F32)],
        compiler_params=_cp("parallel", "arbitrary"),
        name="hgrn",
    )(r3(q), r3(k), r3(lf), r3(v), r3(sg), st0, gain, _block_ones(w // 2, HG_DK, BF16),
      _block_ones(w, HG_DK, F32), tri)
    return o.reshape(batch * seq, w), st


def _cmix_kernel(vn_ref, gu_ref, ws_ref, bst_ref, o_ref, *, tl):
    ri = lax.broadcasted_iota(jnp.int32, (tl, tl), 0)
    ci = lax.broadcasted_iota(jnp.int32, (tl, tl), 1)
    for gi in range(CM_GROUPS):
        sl = slice(gi * CM_GW, (gi + 1) * CM_GW)
        wg = jnp.where(ci <= ri, ws_ref[gi, 0:tl, 0:tl], 0.0).astype(BF16)
        mix = _dot(wg, vn_ref[0, :, sl].astype(BF16)) + bst_ref[0:tl, gi:gi + 1]
        o_ref[0, :, sl] = gu_ref[0, :, sl] * mix


def _cmix(vn, gu, ws, bst, batch, seq):
    tl = min(seq, CM_CHUNK)
    assert seq % tl == 0 and tl % 8 == 0
    w = CM_WIDTH
    tile = pl.BlockSpec((1, tl, w), lambda b, c: (b, c, 0))
    out = pl.pallas_call(
        functools.partial(_cmix_kernel, tl=tl),
        grid=(batch, seq // tl),
        in_specs=[tile, tile, _full(ws.shape), _full(bst.shape)],
        out_specs=tile,
        out_shape=jax.ShapeDtypeStruct((batch, seq, w), F32),
        compiler_params=_cp("parallel", "parallel"),
        name="cmix",
    )(vn.reshape(batch, seq, w), gu.reshape(batch, seq, w), ws, bst)
    return out.reshape(batch * seq, w)


def _out_proj_kernel(res_ref, a1_ref, a2_ref, w1_ref, w2_ref, o_ref):
    o_ref[...] = (res_ref[...] + _dot(a1_ref[...].astype(BF16), w1_ref[...])
                  + _dot(a2_ref[...].astype(BF16), w2_ref[...]))


def _out_proj(res, a1, a2, w_bf):
    t, d = res.shape
    k1, k2 = a1.shape[1], a2.shape[1]
    tm = _row_tile(t)
    row = lambda wd: pl.BlockSpec((tm, wd), lambda i: (i, 0))
    return pl.pallas_call(
        _out_proj_kernel,
        grid=(t // tm,),
        in_specs=[row(d), row(k1), row(k2), _full((k1, d)), _full((k2, d))],
        out_specs=row(d),
        out_shape=jax.ShapeDtypeStruct((t, d), F32),
        compiler_params=_cp("parallel"),
        name="out_proj",
    )(res, a1, a2, w_bf[:k1], w_bf[k1:])


def _router_kernel(h_ref, g_ref, w_ref, b_ref, xn_ref, info_ref):
    xn = _rms(h_ref[...], g_ref[...])
    xn_ref[...] = xn.astype(BF16)
    logit = _dot_hi(xn, w_ref[...]) + b_ref[...]
    lane = lax.broadcasted_iota(jnp.int32, logit.shape, 1)
    far = 1 << 20

    def first_max(vals, mask):
        m = jnp.max(jnp.where(mask, vals, -1.0), axis=-1, keepdims=True)
        idx = jnp.min(jnp.where(mask & (vals == m), lane, far), axis=-1, keepdims=True)
        return m, idx

    def softmax(mask):
        m = jnp.max(jnp.where(mask, logit, NEG), axis=-1, keepdims=True)
        e = jnp.where(mask, jnp.exp(logit - m), 0.0)
        return e / jnp.sum(e, axis=-1, keepdims=True)

    cmask = lane < MOE_GROUPS
    pg, grp = first_max(softmax(cmask), cmask)
    lo = MOE_GROUPS + grp * MOE_EPG
    fmask = (lane >= lo) & (lane < lo + MOE_EPG)
    pf = softmax(fmask)
    v1, i1 = first_max(pf, fmask)
    v2, i2 = first_max(pf, fmask & (lane != i1))
    den = v1 + v2
    info = jnp.where(lane == 0, (i1 - MOE_GROUPS).astype(F32), 0.0)
    info = jnp.where(lane == 1, (i2 - MOE_GROUPS).astype(F32), info)
    info = jnp.where(lane == 2, pg * v1 / den, info)
    info = jnp.where(lane == 3, pg * v2 / den, info)
    info_ref[...] = info


def _router(h, g, w_r, b_r):
    t, d = h.shape
    tm = _row_tile(t)
    row = lambda wd: pl.BlockSpec((tm, wd), lambda i: (i, 0))
    return pl.pallas_call(
        _router_kernel,
        grid=(t // tm,),
        in_specs=[row(d), _full((1, d)), _full((d, LANES)), _full((1, LANES))],
        out_specs=[row(d), row(LANES)],
        out_shape=[jax.ShapeDtypeStruct((t, d), BF16), jax.ShapeDtypeStruct((t, LANES), F32)],
        compiler_params=_cp("parallel"),
        name="router",
    )(h, g, w_r, b_r)


def _moe_ffn_kernel(be_ref, nu_ref, x_ref, w1_ref, w3_ref, w2_ref, o_ref, w1b, w3b, w2b):
    i = pl.program_id(0)

    @pl.when((i == 0) | (be_ref[i] != be_ref[jnp.maximum(i - 1, 0)]))
    def _():
        w1b[...] = w1_ref[0].astype(BF16)
        w3b[...] = w3_ref[0].astype(BF16)
        w2b[...] = w2_ref[0].astype(BF16)

    @pl.when(i < nu_ref[0])
    def _():
        x = x_ref[...]
        hdn = _silu(_dot(x, w1b[...])) * _dot(x, w3b[...])
        o_ref[...] = _dot(hdn.astype(BF16), w2b[...])

    @pl.when(i >= nu_ref[0])
    def _():
        o_ref[...] = jnp.zeros_like(o_ref)


def _moe_ffn(xp, blk_e, nused, w1, w3, w2, rb):
    n, d = xp.shape
    ff = w1.shape[2]
    nblk = n // rb
    gs = pltpu.PrefetchScalarGridSpec(
        num_scalar_prefetch=2,
        grid=(nblk,),
        in_specs=[pl.BlockSpec((rb, d), lambda i, be, nu: (i, 0)),
                  pl.BlockSpec((1, d, ff), lambda i, be, nu: (be[i], 0, 0)),
                  pl.BlockSpec((1, d, ff), lambda i, be, nu: (be[i], 0, 0)),
                  pl.BlockSpec((1, ff, d), lambda i, be, nu: (be[i], 0, 0))],
        out_specs=pl.BlockSpec((rb, d), lambda i, be, nu: (i, 0)),
        scratch_shapes=[pltpu.VMEM((d, ff), BF16), pltpu.VMEM((d, ff), BF16), pltpu.VMEM((ff, d), BF16)],
    )
    return pl.pallas_call(
        _moe_ffn_kernel,
        grid_spec=gs,
        out_shape=jax.ShapeDtypeStruct((n, d), F32),
        compiler_params=_cp("arbitrary"),
        name="moe_ffn",
    )(blk_e, nused, xp, w1, w3, w2)


def _moe(h, g, w_r, b_r, w1, w3, w2):
    t, d = h.shape
    xn, info = _router(h, g, w_r, b_r)
    eid = info[:, 0:2].astype(jnp.int32)
    gate = info[:, 2:4]
    ne = MOE_EXPERTS
    tk = 2 * t
    rb = 256 if t >= 4096 else 32
    flat = eid.reshape(-1)
    order = jnp.argsort(flat).astype(jnp.int32)
    rank = jnp.argsort(order).astype(jnp.int32)
    counts = jnp.sum((flat[:, None] == jnp.arange(ne)[None, :]).astype(jnp.int32), axis=0)
    padc = (counts + rb - 1) // rb * rb
    pend = jnp.cumsum(padc)
    pstart = pend - padc
    cstart = jnp.cumsum(counts) - counts
    slot = (pstart[flat] + rank - cstart[flat]).reshape(t, 2)
    nblk = -(-tk // rb) + ne
    blk_e = jnp.minimum(jnp.sum((pend[None, :] <= (jnp.arange(nblk) * rb)[:, None]).astype(jnp.int32), axis=1),
                        ne - 1).astype(jnp.int32)
    pe = jnp.repeat(blk_e, rb)
    off = jnp.arange(nblk * rb) - pstart[pe]
    rows = jnp.where(off < counts[pe], order[jnp.clip(off + cstart[pe], 0, tk - 1)] // 2, 0)
    xp = xn[rows]
    nused = (pend[-1:] // rb).astype(jnp.int32)
    yp = _moe_ffn(xp, blk_e, nused, w1, w3, w2, rb)
    return gate[:, 0:1] * yp[slot[:, 0]] + gate[:, 1:2] * yp[slot[:, 1]]


def _ple_kernel(h_ref, y_ref, p_ref, g_ref, wg_ref, wp_ref, fg_ref, o_ref, *, final):
    h = h_ref[...] + y_ref[...]
    gate = _sigmoid(_dot(_rms(h, g_ref[...]).astype(BF16), wg_ref[...]))
    out = h + gate * _dot(p_ref[...].astype(BF16), wp_ref[...])
    if final:
        out = _rms(out, fg_ref[...])
    o_ref[...] = out


def _ple(h, y, p, g, wg_bf, wp_bf, fg, final):
    t, d = h.shape
    pd = p.shape[1]
    tm = _row_tile(t)
    row = lambda wd: pl.BlockSpec((tm, wd), lambda i: (i, 0))
    return pl.pallas_call(
        functools.partial(_ple_kernel, final=final),
        grid=(t // tm,),
        in_specs=[row(d), row(d), row(pd), _full((1, d)), _full((d, d)), _full((pd, d)), _full((1, d))],
        out_specs=row(d),
        out_shape=jax.ShapeDtypeStruct((t, d), F32),
        compiler_params=_cp("parallel"),
        name="ple",
    )(h, y, p, g, wg_bf, wp_bf, fg)


_HD = HEAD_DIM
_C_NQ = NSA_HEADS * _HD
_C_KV = NSA_KV * _HD
_C_MQ = MOBA_HEADS * _HD
_C_MK = MOBA_KV * _HD
_O_NQ = 0
_O_CK = _O_NQ + _C_NQ
_O_CV = _O_CK + _C_KV
_O_SK = _O_CV + _C_KV
_O_SV = _O_SK + _C_KV
_O_WK = _O_SV + _C_KV
_O_WV = _O_WK + _C_KV
_O_MQ = _O_WV + _C_KV
_O_MK = _O_MQ + _C_MQ
_O_MV = _O_MK + _C_MK
_O_NG = _O_MV + _C_MK
_O_NQS = _O_NG + LANES
_O_SKS = _O_NQS + _C_NQ
_O_WKS = _O_SKS + _C_KV
_O_MQS = _O_WKS + _C_KV
_O_MKS = _O_MQS + _C_MQ
_C_TOTAL = _O_MKS + _C_MK


def _proj_c_kernel(x_ref, g_ref, w_ref, cos_ref, sin_ref, qu_ref, qr_ref, mq_ref, ng_ref, rows_ref, win_ref):
    xn = _rms(x_ref[...], g_ref[...]).astype(BF16)
    z = _dot(xn, w_ref[...])
    cs = cos_ref[...]
    sn = sin_ref[...]

    def rope(o, os, wd):
        reps = wd // LANES
        return z[:, o:o + wd] * jnp.tile(cs, (1, reps)) + z[:, os:os + wd] * jnp.tile(sn, (1, reps))

    qu_ref[...] = z[:, _O_NQ:_O_NQ + _C_NQ]
    qr_ref[...] = rope(_O_NQ, _O_NQS, _C_NQ)
    mq_ref[...] = rope(_O_MQ, _O_MQS, _C_MQ)
    ng_ref[...] = z[:, _O_NG:_O_NG + LANES]
    rows_ref[:, 0:2 * _C_KV] = z[:, _O_CK:_O_CK + 2 * _C_KV]
    rows_ref[:, 2 * _C_KV:3 * _C_KV] = rope(_O_SK, _O_SKS, _C_KV)
    rows_ref[:, 3 * _C_KV:4 * _C_KV] = z[:, _O_SV:_O_SV + _C_KV]
    rows_ref[:, 4 * _C_KV:4 * _C_KV + _C_MK] = rope(_O_MK, _O_MKS, _C_MK)
    rows_ref[:, 4 * _C_KV + _C_MK:] = z[:, _O_MV:_O_MV + _C_MK]
    win_ref[:, 0:_C_KV] = rope(_O_WK, _O_WKS, _C_KV)
    win_ref[:, _C_KV:] = z[:, _O_WV:_O_WV + _C_KV]


def _proj_c_weights(w):
    offs = np.concatenate([[0], np.cumsum(C_SPLITS)])
    nq, ck, cv, sk, sv, wk, wv, ng, mq, mk, mv = [w[:, offs[i]:offs[i + 1]] for i in range(len(C_SPLITS))]
    half = ROT_DIM // 2

    def swapped(m):
        d = m.shape[0]
        m3 = m.reshape(d, -1, _HD)
        out = jnp.concatenate([m3[..., half:ROT_DIM], m3[..., :half], jnp.zeros_like(m3[..., ROT_DIM:])], axis=-1)
        return out.reshape(d, -1)

    ngp = jnp.pad(ng, ((0, 0), (0, LANES - ng.shape[1])))
    return jnp.concatenate([nq, ck, cv, sk, sv, wk, wv, mq, mk, mv, ngp,
                            swapped(nq), swapped(sk), swapped(wk), swapped(mq), swapped(mk)], axis=1).astype(BF16)


def _rope_tables(pos):
    half = ROT_DIM // 2
    inv = ROPE_THETA ** (-jnp.arange(half, dtype=F32) / half)
    ang = pos.astype(F32)[:, None] * inv
    cos, sin = jnp.cos(ang), jnp.sin(ang)
    n = pos.shape[0]
    c64 = jnp.concatenate([cos, cos, jnp.ones((n, _HD - ROT_DIM), F32)], axis=1)
    s64 = jnp.concatenate([-sin, sin, jnp.zeros((n, _HD - ROT_DIM), F32)], axis=1)
    return jnp.tile(c64, (1, LANES // _HD)), jnp.tile(s64, (1, LANES // _HD))


def _proj_c(x, g, wc_bf, pos, seq):
    t, d = x.shape
    tm = _row_tile(t)
    tr = max(seq, tm)
    assert tr % tm == 0 and tr % seq == 0
    cs, sn = _rope_tables(jnp.tile(pos, tr // seq))
    ntab = tr // tm
    row = lambda wd: pl.BlockSpec((tm, wd), lambda i: (i, 0))
    tab = pl.BlockSpec((tm, LANES), lambda i: (i % ntab, 0))
    widths = [_C_NQ, _C_NQ, _C_MQ, LANES, ROW_W, 2 * _C_KV]
    return pl.pallas_call(
        _proj_c_kernel,
        grid=(t // tm,),
        in_specs=[row(d), _full((1, d)), _full((d, _C_TOTAL)), tab, tab],
        out_specs=[row(wd) for wd in widths],
        out_shape=[jax.ShapeDtypeStruct((t, wd), F32) for wd in widths],
        compiler_params=_cp("parallel"),
        name="proj_c",
    )(x, g, wc_bf, cs, sn)


def _mm_kernel(a_ref, w_ref, o_ref):
    o_ref[...] = _dot(a_ref[...], w_ref[...])


def _mm(a_bf, w_bf):
    m, k = a_bf.shape
    n = w_bf.shape[1]
    tm = _row_tile(m)
    return pl.pallas_call(
        _mm_kernel,
        grid=(m // tm,),
        in_specs=[pl.BlockSpec((tm, k), lambda i: (i, 0)), _full((k, n))],
        out_specs=pl.BlockSpec((tm, n), lambda i: (i, 0)),
        out_shape=jax.ShapeDtypeStruct((m, n), F32),
        compiler_params=_cp("parallel"),
        name="mm",
    )(a_bf, w_bf)


def _cmp_fin_kernel(a_ref, pb_ref, w2_ref, o_ref, *, nseg):
    a = a_ref[0]
    hw = a.shape[1] // 2
    bias = pb_ref[0:1, :hw] + pb_ref[1:2, hw:]
    pre = a[:, :hw] + pltpu.roll(a[:, hw:], nseg - 1, axis=0) + bias
    o_ref[0] = _dot(_gelu(pre).astype(BF16), w2_ref[...])


def _compress_weights(w1k, w1v, w2k, w2v, pek, pev):
    span = NSA_CMP_LEN // NSA_CMP_STRIDE
    st = NSA_CMP_STRIDE
    hid = w1k.shape[1]
    slots = 2 * NSA_KV
    kinds = [0] * NSA_KV + [1] * NSA_KV
    w1 = [w1k.reshape(span, st, _HD, hid), w1v.reshape(span, st, _HD, hid)]
    pe = [pek.reshape(span, st, _HD), pev.reshape(span, st, _HD)]
    wbig = jnp.zeros((st, slots, _HD, span, slots, hid), F32)
    pex = jnp.zeros((8, st, slots, _HD), F32)
    w2big = jnp.zeros((slots, hid, slots, _HD), F32)
    for s in range(slots):
        kd = kinds[s]
        wbig = wbig.at[:, s, :, :, s, :].set(w1[kd].transpose(1, 2, 0, 3))
        pex = pex.at[0:span, :, s, :].set(pe[kd])
        w2big = w2big.at[s, :, s, :].set([w2k, w2v][kd])
    return (wbig.reshape(st * slots * _HD, span * slots * hid).astype(BF16),
            pex.reshape(8, st * slots * _HD).astype(BF16),
            w2big.reshape(slots * hid, slots * _HD).astype(BF16))


def _compress(cmpflat_bf, wbig, pex, w2big):
    batch, nseg, kdim = cmpflat_bf.shape
    a = _mm(cmpflat_bf.reshape(batch * nseg, kdim), wbig).reshape(batch, nseg, -1)
    pb = _mm(pex, wbig)
    n2 = a.shape[2]
    ow = w2big.shape[1]
    return pl.pallas_call(
        functools.partial(_cmp_fin_kernel, nseg=nseg),
        grid=(batch,),
        in_specs=[pl.BlockSpec((1, nseg, n2), lambda b: (b, 0, 0)), _full(pb.shape), _full(w2big.shape)],
        out_specs=pl.BlockSpec((1, nseg, ow), lambda b: (b, 0, 0)),
        out_shape=jax.ShapeDtypeStruct((batch, nseg, ow), F32),
        compiler_params=_cp("parallel"),
        name="compress_fin",
    )(a, pb, w2big)


def _block_mean_kernel(k_ref, o_ref):
    o_ref[0, 0] = jnp.sum(k_ref[0], axis=0, keepdims=True) * (1.0 / MOBA_BLOCK)


def _block_mean(km, nbf):
    batch, _, wd = km.shape
    return pl.pallas_call(
        _block_mean_kernel,
        grid=(batch, nbf),
        in_specs=[pl.BlockSpec((1, MOBA_BLOCK, wd), lambda b, n: (b, n, 0))],
        out_specs=pl.BlockSpec((1, 1, 1, wd), lambda b, n: (b, n, 0, 0)),
        out_shape=jax.ShapeDtypeStruct((batch, nbf, 1, wd), F32),
        compiler_params=_cp("parallel", "parallel"),
        name="block_mean",
    )(km)


def _top_rounds(score, row_id, k):
    far = 1 << 20
    sel = jnp.zeros(score.shape, F32)
    cur = score
    for _ in range(k):
        m = jnp.max(cur, axis=0, keepdims=True)
        idx = jnp.min(jnp.where(cur == m, row_id, far), axis=0, keepdims=True)
        pick = row_id == idx
        sel = jnp.where(pick, 1.0, sel)
        cur = jnp.where(pick, -jnp.inf, cur)
    return sel


def _softmax0(s, mask):
    s = jnp.where(mask, s, NEG)
    e = jnp.where(mask, jnp.exp(s - jnp.max(s, axis=0, keepdims=True)), 0.0)
    return e / jnp.maximum(jnp.sum(e, axis=0, keepdims=True), 1e-30)


def _nsa_front(qu, kc, vct, mt, tpos, tq, nc, n_sel):
    nc_pad = kc.shape[0]
    nsel_pad = mt.shape[0]
    ci = lax.broadcasted_iota(jnp.int32, (nc_pad, 1), 0)
    bi = lax.broadcasted_iota(jnp.int32, (nsel_pad, tq), 0)
    cur = tpos[:, :tq] >> _SEL_SHIFT
    cmask = ((ci * NSA_CMP_STRIDE + (NSA_CMP_LEN - 1)) <= tpos) & (ci < nc)
    pc = _softmax0(_dot_hi(kc, qu), cmask)
    o_c = _dot(vct.astype(BF16), pc.astype(BF16))
    imp = pc[:, 0:tq]
    for r in range(1, NSA_GRP):
        imp = imp + pc[:, r * tq:(r + 1) * tq]
    score = _dot_hi(mt, imp)
    forced = (bi == 0) | (bi == cur) | (bi == cur - 1)
    score = jnp.where(forced, BIG, score)
    score = jnp.where(bi <= cur, score, -BIG)
    score = jnp.where(bi < n_sel, score, -3e38)
    sel = _top_rounds(score, bi, min(NSA_TOPN, n_sel))
    return o_c, jnp.where(bi <= cur, sel, 0.0)


def _nsa_window(qr, kw_ref, vwt_ref, g, tpos, wstart, wl, wpos0):
    wpos = wpos0 + wstart + lax.broadcasted_iota(jnp.int32, (wl, 1), 0)
    kw = kw_ref[0, g, pl.ds(wstart, wl), :].astype(BF16)
    wmask = (wpos <= tpos) & (wpos >= tpos - NSA_WINDOW) & (wpos >= 0)
    pw = _softmax0(_dot(kw, qr), wmask).astype(BF16)
    o_w = jnp.zeros((_HD, qr.shape[1]), F32)
    wt0 = wstart // LANES
    for u in range(wl // LANES):
        o_w = o_w + _dot(vwt_ref[0, g, wt0 + u].astype(BF16), pw[u * LANES:(u + 1) * LANES])
    return o_w


def _online_softmax(m_ref, l_ref, acc_ref, s, mask, pv):
    s = jnp.where(mask, s, NEG)
    m_old = m_ref[...]
    m_new = jnp.maximum(m_old, jnp.max(s, axis=0, keepdims=True))
    alpha = jnp.exp(m_old - m_new)
    p = jnp.where(mask, jnp.exp(s - m_new), 0.0)
    l_ref[...] = alpha * l_ref[...] + jnp.sum(p, axis=0, keepdims=True)
    acc_ref[...] = alpha * acc_ref[...] + pv(p.astype(BF16))
    m_ref[...] = m_new


def _nsa_kernel(qu_ref, qr_ref, gl_ref, kc_ref, vct_ref, mt_ref, ks_ref, vst_ref, kw_ref, vwt_ref,
                o_ref, sel_scr, m_scr, l_scr, acc_scr, *, tq, nc, n_sel, qpos0, nkt_max, wl, wpos0):
    qi = pl.program_id(1)
    n = NSA_GRP * tq
    tk = KEY_TILE
    t0 = qpos0 + qi * tq
    lane = lax.broadcasted_iota(jnp.int32, (1, n), 1)
    tpos = t0 + (lane & (tq - 1))
    nkt = jnp.minimum((t0 + tq - 1) // tk + 1, nkt_max)
    wstart = pl.multiple_of(jnp.maximum(t0 - NSA_WINDOW, 0), LANES)

    qrs = []
    for g in range(NSA_KV):
        qr = qr_ref[0, 0, g].astype(BF16)
        qrs.append(qr)
        o_c, sel = _nsa_front(qu_ref[0, 0, g], kc_ref[0, g], vct_ref[0, g], mt_ref[...], tpos, tq, nc, n_sel)
        sel_scr[g] = sel
        o_w = _nsa_window(qr, kw_ref, vwt_ref, g, tpos, wstart, wl, wpos0)
        gate = _sigmoid(gl_ref[0, 0, g])
        o_ref[0, 0, g] = gate[0:1] * o_c + gate[2:3] * o_w
    m_scr[...] = jnp.full(m_scr.shape, NEG, F32)
    l_scr[...] = jnp.zeros(l_scr.shape, F32)
    acc_scr[...] = jnp.zeros(acc_scr.shape, F32)

    def body(j, carry):
        kpos = j * tk + lax.broadcasted_iota(jnp.int32, (tk, 1), 0)
        causal = kpos <= tpos
        nb = tk // NSA_SEL_BLOCK
        rows = pl.ds(pl.multiple_of(j * tk, tk), tk)
        s = jnp.concatenate([_dot(ks_ref[0, g, rows, :].astype(BF16), qrs[g]) for g in range(NSA_KV)], axis=1)
        masks = []
        for g in range(NSA_KV):
            pieces = [jnp.broadcast_to(sel_scr[g, pl.ds(j * nb + u, 1), :], (NSA_SEL_BLOCK, tq)) for u in range(nb)]
            mk = jnp.concatenate(pieces, axis=0)
            masks.append((jnp.concatenate([mk] * NSA_GRP, axis=1) > 0.5) & causal)

        def pv(p):
            return jnp.concatenate([_dot(vst_ref[0, g, j].astype(BF16), p[:, g * n:(g + 1) * n])
                                    for g in range(NSA_KV)], axis=1)

        _online_softmax(m_scr, l_scr, acc_scr, s, jnp.concatenate(masks, axis=1), pv)
        return carry

    lax.fori_loop(0, nkt, body, 0)
    o_s = acc_scr[...] / jnp.maximum(l_scr[...], 1e-30)
    for g in range(NSA_KV):
        gate = _sigmoid(gl_ref[0, 0, g])
        o_ref[0, 0, g] = o_ref[0, 0, g] + gate[1:2] * o_s[:, g * n:(g + 1) * n]


def _nsa_select_kernel(qu_ref, qr_ref, gl_ref, kc_ref, vct_ref, mt_ref, kw_ref, vwt_ref,
                       o_ref, g1_ref, sel_ref, *, tq, nc, n_sel, qpos0, wl, wpos0):
    n = NSA_GRP * tq
    lane = lax.broadcasted_iota(jnp.int32, (1, n), 1)
    tpos = qpos0 + (lane & (tq - 1))
    for g in range(NSA_KV):
        qr = qr_ref[0, 0, g].astype(BF16)
        o_c, sel = _nsa_front(qu_ref[0, 0, g], kc_ref[0, g], vct_ref[0, g], mt_ref[...], tpos, tq, nc, n_sel)
        sel_ref[0, g] = sel
        o_w = _nsa_window(qr, kw_ref, vwt_ref, g, tpos, 0, wl, wpos0)
        gate = _sigmoid(gl_ref[0, 0, g])
        o_ref[0, g] = gate[0:1] * o_c + gate[2:3] * o_w
        g1_ref[0, g] = gate[1:2]


def _moba_select(q, kb, own, nbf):
    ni = lax.broadcasted_iota(jnp.int32, (kb.shape[0], q.shape[1]), 0)
    gs = _dot_hi(kb, q)
    gs = jnp.where(ni < own, gs, -BIG)
    gs = jnp.where(ni < nbf, gs, -3e38)
    sel = _top_rounds(gs, ni, min(MOBA_TOPK, nbf))
    return jnp.where((ni < own) & (ni < nbf), sel, 0.0)


def _moba_select_kernel(q_ref, kb_ref, sel_ref, *, own, nbf):
    for g in range(MOBA_KV):
        sel_ref[0, g] = _moba_select(q_ref[0, 0, g], kb_ref[0, g], own, nbf)


def _moba_kernel(q_ref, kb_ref, km_ref, vmt_ref, o_ref, sel_scr, m_scr, l_scr, acc_scr, *, tq, nbf, qpos0, nkt_max):
    qi = pl.program_id(1)
    n = MOBA_GRP * tq
    tk = KEY_TILE
    t0 = qpos0 + qi * tq
    lane = lax.broadcasted_iota(jnp.int32, (1, n), 1)
    tpos = t0 + (lane & (tq - 1))
    own = t0 // MOBA_BLOCK
    nkt = jnp.minimum(own + 1, nkt_max)
    scale = _HD ** -0.5

    qbs = []
    for g in range(MOBA_KV):
        q = q_ref[0, 0, g]
        qbs.append((q * scale).astype(BF16))
        sel_scr[g] = _moba_select(q, kb_ref[0, g], own, nbf)
    m_scr[...] = jnp.full(m_scr.shape, NEG, F32)
    l_scr[...] = jnp.zeros(l_scr.shape, F32)
    acc_scr[...] = jnp.zeros(acc_scr.shape, F32)

    def body(j, carry):
        kpos = j * tk + lax.broadcasted_iota(jnp.int32, (tk, 1), 0)
        own_causal = ((kpos >> _MOBA_SHIFT) == own) & (kpos <= tpos)
        rows = pl.ds(pl.multiple_of(j * tk, tk), tk)
        s = jnp.concatenate([_dot(km_ref[0, g, rows, :].astype(BF16), qbs[g]) for g in range(MOBA_KV)], axis=1)
        mask = jnp.concatenate([(jnp.broadcast_to(sel_scr[g, pl.ds(j, 1), :], (tk, n)) > 0.5) | own_causal
                                for g in range(MOBA_KV)], axis=1)

        def pv(p):
            return jnp.concatenate([_dot(vmt_ref[0, g, j].astype(BF16), p[:, g * n:(g + 1) * n])
                                    for g in range(MOBA_KV)], axis=1)

        _online_softmax(m_scr, l_scr, acc_scr, s, mask, pv)
        return carry

    lax.fori_loop(0, nkt, body, 0)
    o = acc_scr[...] / jnp.maximum(l_scr[...], 1e-30)
    for g in range(MOBA_KV):
        o_ref[0, 0, g] = o[:, g * n:(g + 1) * n]


def _to_qt(x, batch, nt, tq, groups, rep):
    x = x.reshape(batch, nt, tq, groups, rep, _HD).transpose(0, 1, 3, 5, 4, 2)
    return x.reshape(batch, nt, groups, _HD, rep * tq)


def _from_qt(x, batch, nt, tq, groups, rep):
    x = x.reshape(batch, nt, groups, _HD, rep, tq).transpose(0, 1, 5, 2, 4, 3)
    return x.reshape(batch, nt * tq, groups * rep * _HD)


def _head_major(x, heads):
    b, l, _ = x.shape
    return x.reshape(b, l, heads, _HD).transpose(0, 2, 1, 3)


def _vt_tiles(x, heads, tile):
    b, l, _ = x.shape
    return x.reshape(b, l // tile, tile, heads, _HD).transpose(0, 3, 1, 4, 2)


def _sel_score_matrix(nsel_pad, nc_pad, nc):
    ratio = NSA_SEL_BLOCK // NSA_CMP_STRIDE
    span = NSA_CMP_LEN // NSA_CMP_STRIDE
    mt = np.zeros((nsel_pad, nc_pad), np.float32)
    for j in range(nsel_pad):
        for o in range(-(span - 1), ratio):
            c = j * ratio + o
            wt = sum(1 for m in range(ratio) for q in range(span) if m - q == o)
            if 0 <= c < nc:
                mt[j, c] = wt
    return jnp.asarray(mt)


def _mixer_c_attention(qu, qr, mq, ng, rfull, wfull, kcvc, batch, lq, lt, qpos0, wpos0):
    tk = KEY_TILE
    tq = LANES if lq % LANES == 0 else 64
    lqp = -(-lq // tq) * tq
    nt = lqp // tq
    lk_pad = rfull.shape[1]
    nkt_max = lk_pad // tk
    nc = (lt - NSA_CMP_LEN) // NSA_CMP_STRIDE + 1
    nc_pad = kcvc.shape[1]
    n_sel = -(-lt // NSA_SEL_BLOCK)
    nsel_pad = -(-max(n_sel, nkt_max * (tk // NSA_SEL_BLOCK)) // 8) * 8
    nbf = lt // MOBA_BLOCK
    nbf_pad = -(-max(nbf, nkt_max) // 8) * 8
    scale = _HD ** -0.5

    def padq(x):
        x = x.reshape(batch, lq, -1)
        return jnp.pad(x, ((0, 0), (0, lqp - lq), (0, 0))).reshape(batch * lqp, -1)

    qu_t = _to_qt(padq(qu) * scale, batch, nt, tq, NSA_KV, NSA_GRP)
    qr_t = _to_qt(padq(qr) * scale, batch, nt, tq, NSA_KV, NSA_GRP)
    mq_t = _to_qt(padq(mq), batch, nt, tq, MOBA_KV, MOBA_GRP)
    gl = padq(ng)[:, :NSA_HEADS * 3].reshape(batch, nt, tq, NSA_KV, NSA_GRP, 3).transpose(0, 1, 3, 5, 4, 2)
    gl = gl.reshape(batch, nt, NSA_KV, 3, NSA_GRP * tq)

    kc = _head_major(kcvc[:, :, 0:_C_KV], NSA_KV)
    vct = _head_major(kcvc[:, :, _C_KV:2 * _C_KV], NSA_KV).transpose(0, 1, 3, 2)
    mt = _sel_score_matrix(nsel_pad, nc_pad, nc)
    ks = _head_major(rfull[:, :, 2 * _C_KV:3 * _C_KV], NSA_KV).astype(BF16)
    vst = _vt_tiles(rfull[:, :, 3 * _C_KV:4 * _C_KV], NSA_KV, tk).astype(BF16)
    km_flat = rfull[:, :, 4 * _C_KV:4 * _C_KV + _C_MK]
    km = _head_major(km_flat, MOBA_KV).astype(BF16)
    vmt = _vt_tiles(rfull[:, :, 4 * _C_KV + _C_MK:], MOBA_KV, tk).astype(BF16)
    kw = _head_major(wfull[:, :, 0:_C_KV], NSA_KV).astype(BF16)
    vwt = _vt_tiles(wfull[:, :, _C_KV:], NSA_KV, LANES).astype(BF16)
    wl = NSA_WINDOW + tq
    assert wl % LANES == 0 and wfull.shape[1] >= wl

    kb = _block_mean(km_flat, nbf)[:, :, 0, :]
    kb = jnp.pad(kb, ((0, 0), (0, nbf_pad - nbf), (0, 0)))
    kb = _head_major(kb, MOBA_KV)

    n_n = NSA_GRP * tq
    qspec = pl.BlockSpec((1, 1, NSA_KV, _HD, n_n), lambda b, i: (b, i, 0, 0, 0))
    per_b = lambda shp: pl.BlockSpec((1,) + shp, lambda b, i: (b,) + (0,) * len(shp))
    o_n = pl.pallas_call(
        functools.partial(_nsa_kernel, tq=tq, nc=nc, n_sel=n_sel, qpos0=qpos0, nkt_max=nkt_max, wl=wl,
                          wpos0=wpos0),
        grid=(batch, nt),
        in_specs=[qspec, qspec, pl.BlockSpec((1, 1, NSA_KV, 3, n_n), lambda b, i: (b, i, 0, 0, 0)),
                  per_b(kc.shape[1:]), per_b(vct.shape[1:]), _full(mt.shape),
                  per_b(ks.shape[1:]), per_b(vst.shape[1:]), per_b(kw.shape[1:]), per_b(vwt.shape[1:])],
        out_specs=qspec,
        out_shape=jax.ShapeDtypeStruct((batch, nt, NSA_KV, _HD, n_n), F32),
        scratch_shapes=[pltpu.VMEM((NSA_KV, nsel_pad, tq), F32), pltpu.VMEM((1, NSA_KV * n_n), F32),
                        pltpu.VMEM((1, NSA_KV * n_n), F32), pltpu.VMEM((_HD, NSA_KV * n_n), F32)],
        compiler_params=_cp("parallel", "arbitrary"),
        name="nsa",
    )(qu_t, qr_t, gl, kc, vct, mt, ks, vst, kw, vwt)

    n_m = MOBA_GRP * tq
    mspec = pl.BlockSpec((1, 1, MOBA_KV, _HD, n_m), lambda b, i: (b, i, 0, 0, 0))
    o_m = pl.pallas_call(
        functools.partial(_moba_kernel, tq=tq, nbf=nbf, qpos0=qpos0, nkt_max=nkt_max),
        grid=(batch, nt),
        in_specs=[mspec, per_b(kb.shape[1:]), per_b(km.shape[1:]), per_b(vmt.shape[1:])],
        out_specs=mspec,
        out_shape=jax.ShapeDtypeStruct((batch, nt, MOBA_KV, _HD, n_m), F32),
        scratch_shapes=[pltpu.VMEM((MOBA_KV, nbf_pad, n_m), F32), pltpu.VMEM((1, MOBA_KV * n_m), F32),
                        pltpu.VMEM((1, MOBA_KV * n_m), F32), pltpu.VMEM((_HD, MOBA_KV * n_m), F32)],
        compiler_params=_cp("parallel", "arbitrary"),
        name="moba",
    )(mq_t, kb, km, vmt)

    o_n = _from_qt(o_n, batch, nt, tq, NSA_KV, NSA_GRP)[:, :lq].reshape(batch * lq, -1)
    o_m = _from_qt(o_m, batch, nt, tq, MOBA_KV, MOBA_GRP)[:, :lq].reshape(batch * lq, -1)
    return o_n, o_m


PAGES_PER_STEP = 4
_SEG_PER_PAGE = PAGE_SIZE // NSA_CMP_STRIDE


def _page_specs(npages, n_layers, layer, width, col_block):
    def spec(pg):
        return pl.BlockSpec(
            (1, PAGE_SIZE, width),
            lambda b, j, p: (p[b * npages + j * PAGES_PER_STEP + pg] * n_layers + layer, 0, col_block))
    return [spec(pg) for pg in range(PAGES_PER_STEP)]


def _scan_kernel(pt_ref, *refs):
    pps = PAGES_PER_STEP
    ck_refs, cv_refs, mk_refs = refs[0:pps], refs[pps:2 * pps], refs[2 * pps:3 * pps]
    flat_ref, kbar_ref = refs[3 * pps:]
    wd = 2 * _C_KV
    for pair in range(pps // 2):
        r0 = 2 * pair * _SEG_PER_PAGE
        for i in range(NSA_CMP_STRIDE):
            for half, src in enumerate((ck_refs, cv_refs)):
                x = jnp.concatenate([src[2 * pair][0, pl.ds(i, _SEG_PER_PAGE, stride=NSA_CMP_STRIDE), :],
                                     src[2 * pair + 1][0, pl.ds(i, _SEG_PER_PAGE, stride=NSA_CMP_STRIDE), :]], axis=0)
                c0 = i * wd + half * _C_KV
                flat_ref[0, r0:r0 + 2 * _SEG_PER_PAGE, c0:c0 + _C_KV] = x.astype(BF16)
        ka, kb = mk_refs[2 * pair][0], mk_refs[2 * pair + 1][0]
        kbar_ref[0, pair] = (jnp.sum(ka, axis=0, keepdims=True)
                             + jnp.sum(kb, axis=0, keepdims=True)) * (1.0 / MOBA_BLOCK)


def _scan_pages(pool2, pt, layer, n_layers):
    batch, npages = pt.shape
    assert npages % PAGES_PER_STEP == 0 and MOBA_BLOCK == 2 * PAGE_SIZE
    ng = npages // PAGES_PER_STEP
    nseg = npages * _SEG_PER_PAGE
    nbf = npages // 2
    wd = 2 * _C_KV
    gs = pltpu.PrefetchScalarGridSpec(
        num_scalar_prefetch=1,
        grid=(batch, ng),
        in_specs=(_page_specs(npages, n_layers, layer, _C_KV, 0) + _page_specs(npages, n_layers, layer, _C_KV, 1)
                  + _page_specs(npages, n_layers, layer, _C_MK, 2)),
        out_specs=[pl.BlockSpec((1, PAGES_PER_STEP * _SEG_PER_PAGE, NSA_CMP_STRIDE * wd), lambda b, j, p: (b, j, 0)),
                   pl.BlockSpec((1, PAGES_PER_STEP // 2, 1, _C_MK), lambda b, j, p: (b, j, 0, 0))],
    )
    flat, kbar = pl.pallas_call(
        _scan_kernel,
        grid_spec=gs,
        out_shape=[jax.ShapeDtypeStruct((batch, nseg, NSA_CMP_STRIDE * wd), BF16),
                   jax.ShapeDtypeStruct((batch, nbf, 1, _C_MK), F32)],
        compiler_params=_cp("parallel", "arbitrary"),
        name="scan_pages",
    )(pt.reshape(-1), *([pool2] * (3 * PAGES_PER_STEP)))
    return flat, kbar[:, :, 0, :]


def _flash_update(m_ref, l_ref, acc_ref, k, v, q, mask):
    vb = v.astype(BF16)
    _online_softmax(m_ref, l_ref, acc_ref, _dot(k.astype(BF16), q), mask,
                    lambda p: lax.dot_general(vb, p, (((0,), (0,)), ((), ())), preferred_element_type=F32))


def _dec_attn_kernel(pt_ref, *refs, past, nsteps, seq_pad):
    pps = PAGES_PER_STEP
    sel_refs = refs[0:pps]
    mob_refs = refs[pps:2 * pps]
    (qn_ref, qm_ref, smask_ref, mmask_ref, nsel_ref, nmob_ref, opart_ref, g1_ref,
     on_ref, om_ref, mn, ln, accn, mm, lm, accm) = refs[2 * pps:]
    j = pl.program_id(1)
    kw_n = NSA_KV * _HD
    kw_m = _C_MK

    @pl.when(j == 0)
    def _():
        mn[...] = jnp.full(mn.shape, NEG, F32)
        mm[...] = jnp.full(mm.shape, NEG, F32)
        ln[...] = jnp.zeros(ln.shape, F32)
        lm[...] = jnp.zeros(lm.shape, F32)
        accn[...] = jnp.zeros(accn.shape, F32)
        accm[...] = jnp.zeros(accm.shape, F32)

    qn = qn_ref[0]
    qm = qm_ref[0]
    rows = pps * PAGE_SIZE
    ksv = jnp.concatenate([r[0] for r in sel_refs], axis=0)
    nb = rows // NSA_SEL_BLOCK
    srow = smask_ref[0, pl.ds(pl.multiple_of(j * nb, nb), nb), :]
    smask = jnp.concatenate([jnp.broadcast_to(srow[u:u + 1], (NSA_SEL_BLOCK, LANES)) for u in range(nb)], axis=0)
    _flash_update(mn, ln, accn, ksv[:, :kw_n], ksv[:, kw_n:], qn, smask > 0.5)

    kmv = jnp.concatenate([r[0] for r in mob_refs], axis=0)
    nbm = rows // MOBA_BLOCK
    mmask = jnp.concatenate([jnp.broadcast_to(mmask_ref[0, pl.ds(j * nbm + u, 1), :], (MOBA_BLOCK, LANES))
                             for u in range(nbm)], axis=0)
    _flash_update(mm, lm, accm, kmv[:, :kw_m], kmv[:, kw_m:], qm, mmask > 0.5)

    @pl.when(j == nsteps - 1)
    def _():
        lane = lax.broadcasted_iota(jnp.int32, (1, LANES), 1)
        tpos = past + (lane & (seq_pad - 1))
        new_rows = nsel_ref.shape[1]
        kpos = past + lax.broadcasted_iota(jnp.int32, (new_rows, 1), 0)
        causal = kpos <= tpos
        new_blk = past // NSA_SEL_BLOCK
        nrow = jnp.broadcast_to(smask_ref[0, new_blk:new_blk + 1, :], (new_rows, LANES)) > 0.5
        nsel = nsel_ref[0]
        _flash_update(mn, ln, accn, nsel[:, :kw_n], nsel[:, kw_n:], qn, causal & nrow)
        nmob = nmob_ref[0]
        _flash_update(mm, lm, accm, nmob[:, :kw_m], nmob[:, kw_m:], qm, causal)
        on_ref[0] = opart_ref[0] + g1_ref[0] * (accn[...] / jnp.maximum(ln[...], 1e-30))
        om_ref[0] = accm[...] / jnp.maximum(lm[...], 1e-30)


def _block_diag_q(x, batch, seq, groups, rep):
    x = x.reshape(batch, seq, groups, rep, _HD).transpose(0, 2, 4, 3, 1).reshape(batch, groups, _HD, rep * seq)
    bd = jnp.einsum('bgdn,gh->bgdhn', x, jnp.eye(groups, dtype=x.dtype))
    bd = bd.reshape(batch, groups * _HD, groups * rep * seq)
    return jnp.pad(bd, ((0, 0), (0, 0), (0, LANES - bd.shape[2])))


def _diag_blocks(o, batch, seq, groups, rep):
    o = o[:, :, :groups * rep * seq].reshape(batch, groups, _HD, groups, rep, seq)
    o = jnp.stack([o[:, g, :, g] for g in range(groups)], axis=1)
    return o.transpose(0, 4, 1, 3, 2).reshape(batch * seq, groups * rep * _HD)


def _decode_attention(qu, qr, mq, ng, rows3, win3, pool, pt, cache_win, layer, wbig, pex, w2big, batch, seq):
    npages = pt.shape[1]
    past = npages * PAGE_SIZE
    lt = past + seq
    assert seq <= 8 and seq < NSA_CMP_STRIDE and past % MOBA_BLOCK == 0
    seq_pad = 8
    tq = 64
    n_layers = pool.shape[1]
    pool2 = pool.reshape(pool.shape[0] * n_layers, PAGE_SIZE, ROW_W)
    scale = _HD ** -0.5

    cmpflat, kbar = _scan_pages(pool2, pt, layer, n_layers)
    kcvc = _compress(cmpflat, wbig, pex, w2big)
    nseg = kcvc.shape[1]
    nc = (lt - NSA_CMP_LEN) // NSA_CMP_STRIDE + 1
    assert nseg == nc + 1
    n_sel = -(-lt // NSA_SEL_BLOCK)
    nsel_pad = -(-n_sel // 8) * 8
    nbf = lt // MOBA_BLOCK
    nbf_pad = -(-(nbf + 1) // 8) * 8
    ng_steps = npages // PAGES_PER_STEP

    def padq(x):
        x = x.reshape(batch, seq, -1)
        return jnp.pad(x, ((0, 0), (0, tq - seq), (0, 0))).reshape(batch * tq, -1)

    qu_t = _to_qt(padq(qu) * scale, batch, 1, tq, NSA_KV, NSA_GRP)
    qr_t = _to_qt(padq(qr) * scale, batch, 1, tq, NSA_KV, NSA_GRP)
    mq_t = _to_qt(padq(mq), batch, 1, tq, MOBA_KV, MOBA_GRP)
    gl = padq(ng)[:, :NSA_HEADS * 3].reshape(batch, 1, tq, NSA_KV, NSA_GRP, 3).transpose(0, 1, 3, 5, 4, 2)
    gl = gl.reshape(batch, 1, NSA_KV, 3, NSA_GRP * tq)
    kc = _head_major(kcvc[:, :, 0:_C_KV], NSA_KV)
    vct = _head_major(kcvc[:, :, _C_KV:2 * _C_KV], NSA_KV).transpose(0, 1, 3, 2)
    mt = _sel_score_matrix(nsel_pad, nseg, nc)
    wb = cache_win.reshape(batch, -1, 2 * _C_KV)
    wcat = jnp.concatenate([wb, win3], axis=1)
    wl = -(-wcat.shape[1] // LANES) * LANES
    wfull = jnp.pad(wcat, ((0, 0), (0, wl - wcat.shape[1]), (0, 0)))
    kw = _head_major(wfull[:, :, 0:_C_KV], NSA_KV).astype(BF16)
    vwt = _vt_tiles(wfull[:, :, _C_KV:], NSA_KV, LANES).astype(BF16)
    kb = _head_major(jnp.pad(kbar, ((0, 0), (0, nbf_pad - nbf), (0, 0))), MOBA_KV)

    n_n = NSA_GRP * tq
    n_m = MOBA_GRP * tq
    b1 = lambda shp: pl.BlockSpec((1,) + shp, lambda b: (b,) + (0,) * len(shp))
    o_part, g1, sel = pl.pallas_call(
        functools.partial(_nsa_select_kernel, tq=tq, nc=nc, n_sel=n_sel, qpos0=past, wl=wl,
                          wpos0=past - wb.shape[1]),
        grid=(batch,),
        in_specs=[b1((1, NSA_KV, _HD, n_n)), b1((1, NSA_KV, _HD, n_n)), b1((1, NSA_KV, 3, n_n)),
                  b1(kc.shape[1:]), b1(vct.shape[1:]), _full(mt.shape), b1(kw.shape[1:]), b1(vwt.shape[1:])],
        out_specs=[b1((NSA_KV, _HD, n_n)), b1((NSA_KV, 1, n_n)), b1((NSA_KV, nsel_pad, tq))],
        out_shape=[jax.ShapeDtypeStruct((batch, NSA_KV, _HD, n_n), F32),
                   jax.ShapeDtypeStruct((batch, NSA_KV, 1, n_n), F32),
                   jax.ShapeDtypeStruct((batch, NSA_KV, nsel_pad, tq), F32)],
        compiler_params=_cp("parallel"),
        name="nsa_select",
    )(qu_t, qr_t, gl, kc, vct, mt, kw, vwt)
    msel = pl.pallas_call(
        functools.partial(_moba_select_kernel, own=past // MOBA_BLOCK, nbf=nbf),
        grid=(batch,),
        in_specs=[b1((1, MOBA_KV, _HD, n_m)), b1(kb.shape[1:])],
        out_specs=b1((MOBA_KV, nbf_pad, n_m)),
        out_shape=jax.ShapeDtypeStruct((batch, MOBA_KV, nbf_pad, n_m), F32),
        compiler_params=_cp("parallel"),
        name="moba_select",
    )(mq_t, kb)

    def lanes(x):
        return jnp.pad(x, [(0, 0)] * (x.ndim - 1) + [(0, LANES - x.shape[-1])])

    pad_s = lambda x: jnp.pad(x.reshape(batch, seq, -1), ((0, 0), (0, seq_pad - seq), (0, 0)))
    qn_bd = _block_diag_q(pad_s(qr).reshape(batch * seq_pad, -1) * scale, batch, seq_pad, NSA_KV, NSA_GRP).astype(BF16)
    qm_bd = _block_diag_q(pad_s(mq).reshape(batch * seq_pad, -1) * scale, batch, seq_pad, MOBA_KV, MOBA_GRP).astype(BF16)
    smask = jnp.broadcast_to(sel[:, :, :, None, :seq_pad], (batch, NSA_KV, nsel_pad, NSA_GRP, seq_pad))
    smask = lanes(smask.transpose(0, 2, 1, 3, 4).reshape(batch, nsel_pad, -1))
    mmask = msel.reshape(batch, MOBA_KV, nbf_pad, MOBA_GRP, tq)[..., :seq_pad]
    mmask = lanes(mmask.transpose(0, 2, 1, 3, 4).reshape(batch, nbf_pad, -1))
    op = o_part.reshape(batch, NSA_KV, _HD, NSA_GRP, tq)[..., :seq_pad]
    op_bd = jnp.einsum('bgdrt,gh->bgdhrt', op, jnp.eye(NSA_KV, dtype=F32))
    op_bd = lanes(op_bd.reshape(batch, NSA_KV * _HD, -1))
    g1l = g1.reshape(batch, NSA_KV, NSA_GRP, tq)[..., :seq_pad].reshape(batch, 1, -1)
    g1l = lanes(g1l)
    new_pad = 16
    pad_n = lambda x: jnp.pad(x, ((0, 0), (0, new_pad - seq), (0, 0)))
    new_sel = pad_n(rows3[:, :, 2 * _C_KV:4 * _C_KV])
    new_mob = pad_n(rows3[:, :, 4 * _C_KV:])

    kw_n = NSA_KV * _HD
    bj = lambda shp: pl.BlockSpec((1,) + shp, lambda b, j, p: (b,) + (0,) * len(shp))
    gs = pltpu.PrefetchScalarGridSpec(
        num_scalar_prefetch=1,
        grid=(batch, ng_steps),
        in_specs=(_page_specs(npages, n_layers, layer, 2 * kw_n, 1) + _page_specs(npages, n_layers, layer, 2 * _C_MK, 1)
                  + [bj((kw_n, LANES)), bj((_C_MK, LANES)), bj((nsel_pad, LANES)), bj((nbf_pad, LANES)),
                     bj((new_pad, 2 * kw_n)), bj((new_pad, 2 * _C_MK)), bj((kw_n, LANES)), bj((1, LANES))]),
        out_specs=[bj((kw_n, LANES)), bj((_C_MK, LANES))],
        scratch_shapes=[pltpu.VMEM((1, LANES), F32), pltpu.VMEM((1, LANES), F32), pltpu.VMEM((kw_n, LANES), F32),
                        pltpu.VMEM((1, LANES), F32), pltpu.VMEM((1, LANES), F32), pltpu.VMEM((_C_MK, LANES), F32)],
    )
    o_n, o_m = pl.pallas_call(
        functools.partial(_dec_attn_kernel, past=past, nsteps=ng_steps, seq_pad=seq_pad),
        grid_spec=gs,
        out_shape=[jax.ShapeDtypeStruct((batch, kw_n, LANES), F32), jax.ShapeDtypeStruct((batch, _C_MK, LANES), F32)],
        compiler_params=_cp("parallel", "arbitrary"),
        name="decode_attn",
    )(pt.reshape(-1), *([pool2] * (2 * PAGES_PER_STEP)), qn_bd, qm_bd, smask, mmask, new_sel, new_mob, op_bd, g1l)
    o_n = _diag_blocks(o_n, batch, seq_pad, NSA_KV, NSA_GRP).reshape(batch, seq_pad, -1)[:, :seq]
    o_m = _diag_blocks(o_m, batch, seq_pad, MOBA_KV, MOBA_GRP).reshape(batch, seq_pad, -1)[:, :seq]
    return o_n.reshape(batch * seq, -1), o_m.reshape(batch * seq, -1)


def _prep_weights(prm):
    w = {}
    w['w_in_a'] = prm['w_in_a'].astype(BF16)
    w['w_out_a'] = prm['w_out_a'].astype(BF16)
    w['w_out_c'] = prm['w_out_c'].astype(BF16)
    w['w_in_c'] = [_proj_c_weights(prm['w_in_c'][c]) for c in range(prm['w_in_c'].shape[0])]
    w['cmp'] = [_compress_weights(prm['cmp_w1_k'][c], prm['cmp_w1_v'][c], prm['cmp_w2_k'][c], prm['cmp_w2_v'][c],
                                  prm['cmp_pe_k'][c], prm['cmp_pe_v'][c]) for c in range(prm['w_in_c'].shape[0])]
    depth, d, _ = prm['router_c_w'].shape
    wr = jnp.concatenate([prm['router_c_w'], prm['router_f_w'].transpose(0, 2, 1, 3).reshape(depth, d, -1)], axis=2)
    w['router_w'] = jnp.pad(wr, ((0, 0), (0, 0), (0, LANES - wr.shape[2])))
    br = jnp.concatenate([prm['router_c_b'], prm['router_f_b'].reshape(depth, -1)], axis=1)
    w['router_b'] = jnp.pad(br, ((0, 0), (0, LANES - br.shape[1])))[:, None, :]
    w['moe_w1'] = prm['moe_w1']
    w['moe_w3'] = prm['moe_w3']
    w['moe_w2'] = prm['moe_w2']
    w['ple_w'] = prm['ple_w'].astype(BF16)
    w['ple_gate_w'] = prm['ple_gate_w'].astype(BF16)
    w['lb'] = jnp.cumsum(jax.nn.softmax(prm['hgrn_lb'].astype(F32), axis=0), axis=0)
    return w


def _forward(x, p, pos, prm, w, ctx):
    batch, seq, d = x.shape
    t = batch * seq
    depth = p.shape[0]
    h = x.reshape(t, d)
    outs = {}
    for i in range(depth):
        gmix = prm['norm_mix'][i][None, :]
        if i % 2 == 0:
            a = i // 2
            lb512 = jnp.tile(w['lb'][a], HG_HEADS)[None, :]
            q, k, lf, iv, sg, gu, vn = _proj_a(h, gmix, w['w_in_a'][a], lb512, prm['cm_vnorm'][a][None, :])
            if ctx is None:
                st0 = jnp.zeros((batch, HG_WIDTH, HG_WIDTH), F32)
            else:
                s0 = ctx['state_hgrn'][a].astype(F32)
                eye = jnp.eye(HG_HEADS, dtype=F32)
                st0 = jnp.einsum('bhde,hg->bhegd', s0, eye).reshape(batch, HG_WIDTH, HG_WIDTH)
            o, st = _hgrn(q, k, lf, iv, sg, st0, prm['hgrn_onorm'][a][None, :], batch, seq)
            st5 = st.reshape(batch, HG_HEADS, HG_DK, HG_HEADS, HG_DK)
            s_new = jnp.stack([st5[:, hh, :, hh, :] for hh in range(HG_HEADS)], axis=1).transpose(0, 1, 3, 2)
            cm = _cmix(vn, gu, prm['cm_ws'][a], prm['cm_bs'][a].T, batch, seq)
            h = _out_proj(h, o, cm, w['w_out_a'][a])
            outs.setdefault('hg', []).append(s_new)
            outs.setdefault('cm', []).append(vn.reshape(batch, seq, CM_WIDTH))
        else:
            c = i // 2
            qu, qr, mq, ng, rows, win = _proj_c(h, gmix, w['w_in_c'][c], pos, seq)
            rows3 = rows.reshape(batch, seq, ROW_W)
            win3 = win.reshape(batch, seq, 2 * _C_KV)
            wbig, pex, w2big = w['cmp'][c]
            if ctx is None:
                assert seq % KEY_TILE == 0
                nseg = seq // NSA_CMP_STRIDE
                cmpflat = rows3[:, :, 0:2 * _C_KV].reshape(batch, nseg, -1).astype(BF16)
                kcvc = _compress(cmpflat, wbig, pex, w2big)
                o_n, o_m = _mixer_c_attention(qu, qr, mq, ng, rows3, win3, kcvc, batch, seq, seq, 0, 0)
                new_win = win3[:, seq - min(NSA_WINDOW, seq):]
            else:
                o_n, o_m = _decode_attention(qu, qr, mq, ng, rows3, win3, ctx['cache_kv'], ctx['page_table'],
                                             ctx['cache_win'][c], c, wbig, pex, w2big, batch, seq)
                new_win = win3
            h = _out_proj(h, o_n, o_m, w['w_out_c'][c])
            outs.setdefault('kv', []).append(rows3.reshape(batch, seq, KV_ROW_HEADS, _HD))
            outs.setdefault('win', []).append(new_win.reshape(batch, -1, 2 * NSA_KV, _HD))
        y = _moe(h, prm['norm_ffn'][i][None, :], w['router_w'][i], w['router_b'][i],
                 w['moe_w1'][i], w['moe_w3'][i], w['moe_w2'][i])
        h = _ple(h, y, p[i].reshape(t, -1), prm['norm_ple'][i][None, :], w['ple_gate_w'][i], w['ple_w'][i],
                 prm['final_norm'][None, :], final=(i == depth - 1))
    return h.reshape(batch, seq, d), outs


def kernel(x_prompt, x_sample, cache_kv, cache_win, state_hgrn, page_table, p_prompt, p_sample,
           norm_mix, norm_ffn, norm_ple, final_norm, w_in_a, w_out_a, hgrn_lb, hgrn_onorm, cm_vnorm,
           cm_ws, cm_bs, w_in_c, w_out_c, cmp_pe_k, cmp_w1_k, cmp_w2_k, cmp_pe_v, cmp_w1_v, cmp_w2_v,
           router_c_w, router_c_b, router_f_w, router_f_b, moe_w1, moe_w3, moe_w2, ple_w, ple_gate_w):
    prm = dict(norm_mix=norm_mix, norm_ffn=norm_ffn, norm_ple=norm_ple, final_norm=final_norm,
               w_in_a=w_in_a, w_out_a=w_out_a, hgrn_lb=hgrn_lb, hgrn_onorm=hgrn_onorm, cm_vnorm=cm_vnorm,
               cm_ws=cm_ws, cm_bs=cm_bs, w_in_c=w_in_c, w_out_c=w_out_c, cmp_pe_k=cmp_pe_k,
               cmp_w1_k=cmp_w1_k, cmp_w2_k=cmp_w2_k, cmp_pe_v=cmp_pe_v, cmp_w1_v=cmp_w1_v, cmp_w2_v=cmp_w2_v,
               router_c_w=router_c_w, router_c_b=router_c_b, router_f_w=router_f_w, router_f_b=router_f_b,
               moe_w1=moe_w1, moe_w3=moe_w3, moe_w2=moe_w2, ple_w=ple_w, ple_gate_w=ple_gate_w)
    w = _prep_weights(prm)
    past = page_table.shape[1] * PAGE_SIZE
    y_p, o_p = _forward(x_prompt, p_prompt, jnp.arange(x_prompt.shape[1]), prm, w, None)
    ctx = dict(cache_kv=cache_kv, cache_win=cache_win, state_hgrn=state_hgrn, page_table=page_table)
    y_s, o_s = _forward(x_sample, p_sample, past + jnp.arange(x_sample.shape[1]), prm, w, ctx)
    return (y_p, y_s,
            jnp.stack(o_p['kv'], axis=1), jnp.stack(o_s['kv'], axis=1),
            jnp.stack(o_p['win'], axis=0), jnp.stack(o_s['win'], axis=0),
            jnp.stack(o_p['hg'], axis=0), jnp.stack(o_s['hg'], axis=0),
            jnp.stack(o_s['cm'], axis=0))
```

```python
import functools
import math

import numpy as np
import jax
import jax.numpy as jnp
from jax import lax
from jax.experimental import pallas as pl
from jax.experimental.pallas import tpu as pltpu

F32 = jnp.float32
BF16 = jnp.bfloat16
HI = lax.Precision.HIGHEST

PAGE_SIZE = 128
HEAD_DIM = 64
ROT_DIM = HEAD_DIM // 4
ROPE_THETA = 500000.0
HG_HEADS = 8
HG_DK = 64
HG_WIDTH = HG_HEADS * HG_DK
HG_CHUNK = 64
CM_GROUPS = 4
CM_GW = 128
CM_WIDTH = CM_GROUPS * CM_GW
CM_CHUNK = 128
NSA_HEADS = 8
NSA_KV = 2
NSA_GRP = NSA_HEADS // NSA_KV
NSA_CMP_LEN = 32
NSA_CMP_STRIDE = 16
NSA_SEL_BLOCK = 64
NSA_TOPN = 16
NSA_WINDOW = 512
MOBA_HEADS = 8
MOBA_KV = 4
MOBA_GRP = MOBA_HEADS // MOBA_KV
MOBA_BLOCK = 256
MOBA_TOPK = 3
MOE_GROUPS = 4
MOE_EPG = 8
MOE_EXPERTS = MOE_GROUPS * MOE_EPG
KV_ROW_HEADS = 16
ROW_W = KV_ROW_HEADS * HEAD_DIM
A_SPLITS = [HG_WIDTH] * 4 + [CM_WIDTH] * 2
C_SPLITS = [NSA_HEADS * HEAD_DIM] + [NSA_KV * HEAD_DIM] * 6 + [NSA_HEADS * 3, MOBA_HEADS * HEAD_DIM,
                                                              MOBA_KV * HEAD_DIM, MOBA_KV * HEAD_DIM]
NEG = -1e30
BIG = 1e9
EPS = 1e-6
KEY_TILE = 256
LANES = 128
_MOBA_SHIFT = MOBA_BLOCK.bit_length() - 1
_SEL_SHIFT = NSA_SEL_BLOCK.bit_length() - 1
VMEM_LIMIT = 56 * 1024 * 1024


def _cp(*sem):
    return pltpu.CompilerParams(dimension_semantics=sem, vmem_limit_bytes=VMEM_LIMIT)


def _sigmoid(x):
    return 1.0 / (1.0 + jnp.exp(-x))


def _silu(x):
    return x * _sigmoid(x)


def _gelu(x):
    return 0.5 * x * (1.0 + jnp.tanh(math.sqrt(2.0 / math.pi) * (x + 0.044715 * (x * x * x))))


def _rms(x, g):
    return x * lax.rsqrt(jnp.mean(x * x, axis=-1, keepdims=True) + EPS) * g


def _dot(a, b):
    return jnp.dot(a, b, preferred_element_type=F32)


def _dot_hi(a, b):
    return jnp.dot(a, b, precision=HI, preferred_element_type=F32)


def _dot3(a, b):
    a_hi = a.astype(BF16)
    b_hi = b.astype(BF16)
    a_lo = (a - a_hi.astype(F32)).astype(BF16)
    b_lo = (b - b_hi.astype(F32)).astype(BF16)
    return _dot(a_hi, b_hi) + (_dot(a_hi, b_lo) + _dot(a_lo, b_hi))


def _full(shape):
    n = len(shape)
    return pl.BlockSpec(shape, lambda *_: (0,) * n)


def _row_tile(t):
    for tm in (256, 128, 64, 32, 16, 8):
        if t % tm == 0:
            return tm
    raise ValueError(f"token count {t} is not a multiple of 8")


def _proj_a_kernel(x_ref, g_ref, w_ref, lb_ref, vg_ref, q_ref, k_ref, lf_ref, iv_ref, sg_ref, gu_ref, vn_ref):
    xn = _rms(x_ref[...], g_ref[...]).astype(BF16)
    z = _dot(xn, w_ref[...])
    w = HG_WIDTH
    q_ref[...] = _silu(z[:, 0:w])
    lb = lb_ref[...]
    f = lb + (1.0 - lb) * _sigmoid(z[:, w:2 * w])
    k_ref[...] = 1.0 - f
    lf_ref[...] = jnp.log(f)
    iv_ref[...] = z[:, 2 * w:3 * w]
    sg_ref[...] = _silu(z[:, 3 * w:4 * w])
    gu_ref[...] = _gelu(z[:, 4 * w:4 * w + CM_WIDTH])
    v = _gelu(z[:, 4 * w + CM_WIDTH:])
    for gi in range(CM_GROUPS):
        sl = slice(gi * CM_GW, (gi + 1) * CM_GW)
        vn_ref[:, sl] = _rms(v[:, sl], vg_ref[:, sl])


def _proj_a(x, g, w_bf, lb512, vgain):
    t, d = x.shape
    tm = _row_tile(t)
    n = w_bf.shape[1]
    row = lambda wd: pl.BlockSpec((tm, wd), lambda i: (i, 0))
    outs = [jax.ShapeDtypeStruct((t, HG_WIDTH), F32)] * 5 + [jax.ShapeDtypeStruct((t, CM_WIDTH), F32)] * 2
    return pl.pallas_call(
        _proj_a_kernel,
        grid=(t // tm,),
        in_specs=[row(d), _full((1, d)), _full((d, n)), _full((1, HG_WIDTH)), _full((1, CM_WIDTH))],
        out_specs=[row(HG_WIDTH)] * 5 + [row(CM_WIDTH)] * 2,
        out_shape=outs,
        compiler_params=_cp("parallel"),
        name="proj_a",
    )(x, g, w_bf, lb512, vgain)


def _hgrn_kernel(q_ref, k_ref, lf_ref, v_ref, sg_ref, s0_ref, gain_ref, bones_ref, bmask_ref, tri_ref,
                 o_ref, sout_ref, st_scr, p_scr, *, chunk):
    c = pl.program_id(1)
    w = HG_WIDTH
    hw = w // 2

    @pl.when(c == 0)
    def _():
        st_scr[...] = s0_ref[0]

    q = q_ref[0]
    k = k_ref[0]
    v = v_ref[0]
    b = _dot_hi(tri_ref[...], lf_ref[0])
    bones = bones_ref[...]

    def head_sum(x):
        xb = x.astype(BF16)
        return jnp.concatenate([_dot(xb[:, :hw], bones), _dot(xb[:, hw:], bones)], axis=1)

    o_rows = []
    for blk in range(chunk // 8):
        s_len = 8 * (blk + 1)
        bs = b[:s_len]
        ks = k[:s_len]
        vs = v[:s_len]
        row_id = lax.broadcasted_iota(jnp.int32, (s_len, w), 0)
        for t in range(8):
            r = 8 * blk + t
            diff = jnp.where(row_id <= r, b[r:r + 1, :] - bs, NEG)
            p_scr[t * s_len:(t + 1) * s_len, :] = jnp.exp(diff) * q[r:r + 1, :] * ks
        att = head_sum(p_scr[0:8 * s_len, :])
        for t in range(8):
            o_rows.append(jnp.sum(att[t * s_len:(t + 1) * s_len] * vs, axis=0, keepdims=True))
    o_intra = jnp.concatenate(o_rows, axis=0)

    st = st_scr[...]
    qe = (q * jnp.exp(b)).astype(BF16)
    o = o_intra + lax.dot_general(qe, st.astype(BF16), (((1,), (1,)), ((), ())), preferred_element_type=F32)

    b_end = b[chunk - 1:chunk, :]
    kd = (k * jnp.exp(b_end - b)).astype(BF16)
    upd = lax.dot_general(v.astype(BF16), kd, (((0,), (0,)), ((), ())), preferred_element_type=F32)
    st_new = st * jnp.exp(b_end) + upd * bmask_ref[...]
    st_scr[...] = st_new

    o2 = o * o
    hi = o2.astype(BF16).astype(F32)
    ms = (head_sum(hi) + head_sum(o2 - hi)) * (1.0 / HG_DK)
    o_ref[0] = o * lax.rsqrt(ms + EPS) * gain_ref[...] * sg_ref[0]

    @pl.when(c == pl.num_programs(1) - 1)
    def _():
        sout_ref[0] = st_new


def _block_ones(n, blk, dtype):
    i = np.arange(n) // blk
    return jnp.asarray((i[:, None] == i[None, :]).astype(np.float32), dtype)


def _hgrn(q, k, lf, v, sg, st0, gain, batch, seq):
    w = HG_WIDTH
    chunk = math.gcd(seq, HG_CHUNK)
    nch = seq // chunk
    r3 = lambda a: a.reshape(batch, seq, w)
    tile = pl.BlockSpec((1, chunk, w), lambda b, c: (b, c, 0))
    state = pl.BlockSpec((1, w, w), lambda b, c: (b, 0, 0))
    tri = jnp.asarray(np.tril(np.ones((chunk, chunk), np.float32)))
    o, st = pl.pallas_call(
        functools.partial(_hgrn_kernel, chunk=chunk),
        grid=(batch, nch),
        in_specs=[tile] * 5 + [state, _full((1, w)), _full((w // 2, w // 2)), _full((w, w)), _full((chunk, chunk))],
        out_specs=[tile, state],
        out_shape=[jax.ShapeDtypeStruct((batch, seq, w), F32), jax.ShapeDtypeStruct((batch, w, w), F32)],
        scratch_shapes=[pltpu.VMEM((w, w), F32), pltpu.VMEM((8 * chunk, w), F32)],
        compiler_params=_cp("parallel", "arbitrary"),
        name="hgrn",
    )(r3(q), r3(k), r3(lf), r3(v), r3(sg), st0, gain, _block_ones(w // 2, HG_DK, BF16),
      _block_ones(w, HG_DK, F32), tri)
    return o.reshape(batch * seq, w), st


def _cmix_kernel(vn_ref, gu_ref, ws_ref, bst_ref, o_ref, *, tl):
    ri = lax.broadcasted_iota(jnp.int32, (tl, tl), 0)
    ci = lax.broadcasted_iota(jnp.int32, (tl, tl), 1)
    for gi in range(CM_GROUPS):
        sl = slice(gi * CM_GW, (gi + 1) * CM_GW)
        wg = jnp.where(ci <= ri, ws_ref[gi, 0:tl, 0:tl], 0.0).astype(BF16)
        mix = _dot(wg, vn_ref[0, :, sl].astype(BF16)) + bst_ref[0:tl, gi:gi + 1]
        o_ref[0, :, sl] = gu_ref[0, :, sl] * mix


def _cmix(vn, gu, ws, bst, batch, seq):
    tl = min(seq, CM_CHUNK)
    assert seq % tl == 0 and tl % 8 == 0
    w = CM_WIDTH
    tile = pl.BlockSpec((1, tl, w), lambda b, c: (b, c, 0))
    out = pl.pallas_call(
        functools.partial(_cmix_kernel, tl=tl),
        grid=(batch, seq // tl),
        in_specs=[tile, tile, _full(ws.shape), _full(bst.shape)],
        out_specs=tile,
        out_shape=jax.ShapeDtypeStruct((batch, seq, w), F32),
        compiler_params=_cp("parallel", "parallel"),
        name="cmix",
    )(vn.reshape(batch, seq, w), gu.reshape(batch, seq, w), ws, bst)
    return out.reshape(batch * seq, w)


def _out_proj_kernel(res_ref, a1_ref, a2_ref, w1_ref, w2_ref, o_ref):
    o_ref[...] = (res_ref[...] + _dot(a1_ref[...].astype(BF16), w1_ref[...])
                  + _dot(a2_ref[...].astype(BF16), w2_ref[...]))


def _out_proj(res, a1, a2, w_bf):
    t, d = res.shape
    k1, k2 = a1.shape[1], a2.shape[1]
    tm = _row_tile(t)
    row = lambda wd: pl.BlockSpec((tm, wd), lambda i: (i, 0))
    return pl.pallas_call(
        _out_proj_kernel,
        grid=(t // tm,),
        in_specs=[row(d), row(k1), row(k2), _full((k1, d)), _full((k2, d))],
        out_specs=row(d),
        out_shape=jax.ShapeDtypeStruct((t, d), F32),
        compiler_params=_cp("parallel"),
        name="out_proj",
    )(res, a1, a2, w_bf[:k1], w_bf[k1:])


def _router_kernel(h_ref, g_ref, w_ref, b_ref, xn_ref, info_ref):
    xn = _rms(h_ref[...], g_ref[...])
    xn_ref[...] = xn.astype(BF16)
    logit = _dot3(xn, w_ref[...]) + b_ref[...]
    lane = lax.broadcasted_iota(jnp.int32, logit.shape, 1)
    far = 1 << 20

    def first_max(vals, mask):
        m = jnp.max(jnp.where(mask, vals, -1.0), axis=-1, keepdims=True)
        idx = jnp.min(jnp.where(mask & (vals == m), lane, far), axis=-1, keepdims=True)
        return m, idx

    def softmax(mask):
        m = jnp.max(jnp.where(mask, logit, NEG), axis=-1, keepdims=True)
        e = jnp.where(mask, jnp.exp(logit - m), 0.0)
        return e / jnp.sum(e, axis=-1, keepdims=True)

    cmask = lane < MOE_GROUPS
    pg, grp = first_max(softmax(cmask), cmask)
    lo = MOE_GROUPS + grp * MOE_EPG
    fmask = (lane >= lo) & (lane < lo + MOE_EPG)
    pf = softmax(fmask)
    v1, i1 = first_max(pf, fmask)
    v2, i2 = first_max(pf, fmask & (lane != i1))
    den = v1 + v2
    info = jnp.where(lane == 0, (i1 - MOE_GROUPS).astype(F32), 0.0)
    info = jnp.where(lane == 1, (i2 - MOE_GROUPS).astype(F32), info)
    info = jnp.where(lane == 2, pg * v1 / den, info)
    info = jnp.where(lane == 3, pg * v2 / den, info)
    info_ref[...] = info


def _router(h, g, w_r, b_r):
    t, d = h.shape
    tm = _row_tile(t)
    row = lambda wd: pl.BlockSpec((tm, wd), lambda i: (i, 0))
    return pl.pallas_call(
        _router_kernel,
        grid=(t // tm,),
        in_specs=[row(d), _full((1, d)), _full((d, LANES)), _full((1, LANES))],
        out_specs=[row(d), row(LANES)],
        out_shape=[jax.ShapeDtypeStruct((t, d), BF16), jax.ShapeDtypeStruct((t, LANES), F32)],
        compiler_params=_cp("parallel"),
        name="router",
    )(h, g, w_r, b_r)


def _moe_ffn_kernel(be_ref, nu_ref, x_ref, w1_ref, w3_ref, w2_ref, o_ref, w1b, w3b, w2b):
    i = pl.program_id(0)

    @pl.when((i == 0) | (be_ref[i] != be_ref[jnp.maximum(i - 1, 0)]))
    def _():
        w1b[...] = w1_ref[0].astype(BF16)
        w3b[...] = w3_ref[0].astype(BF16)
        w2b[...] = w2_ref[0].astype(BF16)

    @pl.when(i < nu_ref[0])
    def _():
        x = x_ref[...]
        hdn = _silu(_dot(x, w1b[...])) * _dot(x, w3b[...])
        o_ref[...] = _dot(hdn.astype(BF16), w2b[...])

    @pl.when(i >= nu_ref[0])
    def _():
        o_ref[...] = jnp.zeros_like(o_ref)


def _moe_ffn(xp, blk_e, nused, w1, w3, w2, rb):
    n, d = xp.shape
    ff = w1.shape[2]
    nblk = n // rb
    gs = pltpu.PrefetchScalarGridSpec(
        num_scalar_prefetch=2,
        grid=(nblk,),
        in_specs=[pl.BlockSpec((rb, d), lambda i, be, nu: (i, 0)),
                  pl.BlockSpec((1, d, ff), lambda i, be, nu: (be[i], 0, 0)),
                  pl.BlockSpec((1, d, ff), lambda i, be, nu: (be[i], 0, 0)),
                  pl.BlockSpec((1, ff, d), lambda i, be, nu: (be[i], 0, 0))],
        out_specs=pl.BlockSpec((rb, d), lambda i, be, nu: (i, 0)),
        scratch_shapes=[pltpu.VMEM((d, ff), BF16), pltpu.VMEM((d, ff), BF16), pltpu.VMEM((ff, d), BF16)],
    )
    return pl.pallas_call(
        _moe_ffn_kernel,
        grid_spec=gs,
        out_shape=jax.ShapeDtypeStruct((n, d), F32),
        compiler_params=_cp("arbitrary"),
        name="moe_ffn",
    )(blk_e, nused, xp, w1, w3, w2)


def _moe(h, g, w_r, b_r, w1, w3, w2):
    t, d = h.shape
    xn, info = _router(h, g, w_r, b_r)
    eid = info[:, 0:2].astype(jnp.int32)
    gate = info[:, 2:4]
    ne = MOE_EXPERTS
    tk = 2 * t
    rb = 256 if t >= 4096 else 32
    flat = eid.reshape(-1)
    order = jnp.argsort(flat).astype(jnp.int32)
    rank = jnp.argsort(order).astype(jnp.int32)
    counts = jnp.sum((flat[:, None] == jnp.arange(ne)[None, :]).astype(jnp.int32), axis=0)
    padc = (counts + rb - 1) // rb * rb
    pend = jnp.cumsum(padc)
    pstart = pend - padc
    cstart = jnp.cumsum(counts) - counts
    slot = (pstart[flat] + rank - cstart[flat]).reshape(t, 2)
    nblk = -(-tk // rb) + ne
    blk_e = jnp.minimum(jnp.sum((pend[None, :] <= (jnp.arange(nblk) * rb)[:, None]).astype(jnp.int32), axis=1),
                        ne - 1).astype(jnp.int32)
    pe = jnp.repeat(blk_e, rb)
    off = jnp.arange(nblk * rb) - pstart[pe]
    rows = jnp.where(off < counts[pe], order[jnp.clip(off + cstart[pe], 0, tk - 1)] // 2, 0)
    xp = xn[rows]
    nused = (pend[-1:] // rb).astype(jnp.int32)
    yp = _moe_ffn(xp, blk_e, nused, w1, w3, w2, rb)
    return gate[:, 0:1] * yp[slot[:, 0]] + gate[:, 1:2] * yp[slot[:, 1]]


def _ple_kernel(h_ref, y_ref, p_ref, g_ref, wg_ref, wp_ref, fg_ref, o_ref, *, final):
    h = h_ref[...] + y_ref[...]
    gate = _sigmoid(_dot(_rms(h, g_ref[...]).astype(BF16), wg_ref[...]))
    out = h + gate * _dot(p_ref[...].astype(BF16), wp_ref[...])
    if final:
        out = _rms(out, fg_ref[...])
    o_ref[...] = out


def _ple(h, y, p, g, wg_bf, wp_bf, fg, final):
    t, d = h.shape
    pd = p.shape[1]
    tm = _row_tile(t)
    row = lambda wd: pl.BlockSpec((tm, wd), lambda i: (i, 0))
    return pl.pallas_call(
        functools.partial(_ple_kernel, final=final),
        grid=(t // tm,),
        in_specs=[row(d), row(d), row(pd), _full((1, d)), _full((d, d)), _full((pd, d)), _full((1, d))],
        out_specs=row(d),
        out_shape=jax.ShapeDtypeStruct((t, d), F32),
        compiler_params=_cp("parallel"),
        name="ple",
    )(h, y, p, g, wg_bf, wp_bf, fg)


_HD = HEAD_DIM
_C_NQ = NSA_HEADS * _HD
_C_KV = NSA_KV * _HD
_C_MQ = MOBA_HEADS * _HD
_C_MK = MOBA_KV * _HD
_O_NQ = 0
_O_CK = _O_NQ + _C_NQ
_O_CV = _O_CK + _C_KV
_O_SK = _O_CV + _C_KV
_O_SV = _O_SK + _C_KV
_O_WK = _O_SV + _C_KV
_O_WV = _O_WK + _C_KV
_O_MQ = _O_WV + _C_KV
_O_MK = _O_MQ + _C_MQ
_O_MV = _O_MK + _C_MK
_O_NG = _O_MV + _C_MK
_O_NQS = _O_NG + LANES
_O_SKS = _O_NQS + _C_NQ
_O_WKS = _O_SKS + _C_KV
_O_MQS = _O_WKS + _C_KV
_O_MKS = _O_MQS + _C_MQ
_C_TOTAL = _O_MKS + _C_MK


def _proj_c_kernel(x_ref, g_ref, w_ref, cos_ref, sin_ref, qu_ref, qr_ref, mq_ref, ng_ref, rows_ref, win_ref):
    xn = _rms(x_ref[...], g_ref[...]).astype(BF16)
    z = _dot(xn, w_ref[...])
    cs = cos_ref[...]
    sn = sin_ref[...]

    def rope(o, os, wd):
        reps = wd // LANES
        return z[:, o:o + wd] * jnp.tile(cs, (1, reps)) + z[:, os:os + wd] * jnp.tile(sn, (1, reps))

    qu_ref[...] = z[:, _O_NQ:_O_NQ + _C_NQ]
    qr_ref[...] = rope(_O_NQ, _O_NQS, _C_NQ)
    mq_ref[...] = rope(_O_MQ, _O_MQS, _C_MQ)
    ng_ref[...] = z[:, _O_NG:_O_NG + LANES]
    rows_ref[:, 0:2 * _C_KV] = z[:, _O_CK:_O_CK + 2 * _C_KV]
    rows_ref[:, 2 * _C_KV:3 * _C_KV] = rope(_O_SK, _O_SKS, _C_KV)
    rows_ref[:, 3 * _C_KV:4 * _C_KV] = z[:, _O_SV:_O_SV + _C_KV]
    rows_ref[:, 4 * _C_KV:4 * _C_KV + _C_MK] = rope(_O_MK, _O_MKS, _C_MK)
    rows_ref[:, 4 * _C_KV + _C_MK:] = z[:, _O_MV:_O_MV + _C_MK]
    win_ref[:, 0:_C_KV] = rope(_O_WK, _O_WKS, _C_KV)
    win_ref[:, _C_KV:] = z[:, _O_WV:_O_WV + _C_KV]


def _proj_c_weights(w):
    offs = np.concatenate([[0], np.cumsum(C_SPLITS)])
    nq, ck, cv, sk, sv, wk, wv, ng, mq, mk, mv = [w[:, offs[i]:offs[i + 1]] for i in range(len(C_SPLITS))]
    half = ROT_DIM // 2

    def swapped(m):
        d = m.shape[0]
        m3 = m.reshape(d, -1, _HD)
        out = jnp.concatenate([m3[..., half:ROT_DIM], m3[..., :half], jnp.zeros_like(m3[..., ROT_DIM:])], axis=-1)
        return out.reshape(d, -1)

    ngp = jnp.pad(ng, ((0, 0), (0, LANES - ng.shape[1])))
    return jnp.concatenate([nq, ck, cv, sk, sv, wk, wv, mq, mk, mv, ngp,
                            swapped(nq), swapped(sk), swapped(wk), swapped(mq), swapped(mk)], axis=1).astype(BF16)


def _rope_tables(pos):
    half = ROT_DIM // 2
    inv = ROPE_THETA ** (-jnp.arange(half, dtype=F32) / half)
    ang = pos.astype(F32)[:, None] * inv
    cos, sin = jnp.cos(ang), jnp.sin(ang)
    n = pos.shape[0]
    c64 = jnp.concatenate([cos, cos, jnp.ones((n, _HD - ROT_DIM), F32)], axis=1)
    s64 = jnp.concatenate([-sin, sin, jnp.zeros((n, _HD - ROT_DIM), F32)], axis=1)
    return jnp.tile(c64, (1, LANES // _HD)), jnp.tile(s64, (1, LANES // _HD))


def _proj_c(x, g, wc_bf, pos, seq):
    t, d = x.shape
    tm = _row_tile(t)
    tr = max(seq, tm)
    assert tr % tm == 0 and tr % seq == 0
    cs, sn = _rope_tables(jnp.tile(pos, tr // seq))
    ntab = tr // tm
    row = lambda wd: pl.BlockSpec((tm, wd), lambda i: (i, 0))
    tab = pl.BlockSpec((tm, LANES), lambda i: (i % ntab, 0))
    widths = [_C_NQ, _C_NQ, _C_MQ, LANES, ROW_W, 2 * _C_KV]
    return pl.pallas_call(
        _proj_c_kernel,
        grid=(t // tm,),
        in_specs=[row(d), _full((1, d)), _full((d, _C_TOTAL)), tab, tab],
        out_specs=[row(wd) for wd in widths],
        out_shape=[jax.ShapeDtypeStruct((t, wd), F32) for wd in widths],
        compiler_params=_cp("parallel"),
        name="proj_c",
    )(x, g, wc_bf, cs, sn)


def _mm_kernel(a_ref, w_ref, o_ref):
    o_ref[...] = _dot(a_ref[...], w_ref[...])


def _mm(a_bf, w_bf):
    m, k = a_bf.shape
    n = w_bf.shape[1]
    tm = _row_tile(m)
    return pl.pallas_call(
        _mm_kernel,
        grid=(m // tm,),
        in_specs=[pl.BlockSpec((tm, k), lambda i: (i, 0)), _full((k, n))],
        out_specs=pl.BlockSpec((tm, n), lambda i: (i, 0)),
        out_shape=jax.ShapeDtypeStruct((m, n), F32),
        compiler_params=_cp("parallel"),
        name="mm",
    )(a_bf, w_bf)


def _cmp_fin_kernel(a_ref, pb_ref, w2_ref, o_ref, *, nseg):
    a = a_ref[0]
    hw = a.shape[1] // 2
    bias = pb_ref[0:1, :hw] + pb_ref[1:2, hw:]
    pre = a[:, :hw] + pltpu.roll(a[:, hw:], nseg - 1, axis=0) + bias
    o_ref[0] = _dot(_gelu(pre).astype(BF16), w2_ref[...])


def _compress_weights(w1k, w1v, w2k, w2v, pek, pev):
    span = NSA_CMP_LEN // NSA_CMP_STRIDE
    st = NSA_CMP_STRIDE
    hid = w1k.shape[1]
    slots = 2 * NSA_KV
    kinds = [0] * NSA_KV + [1] * NSA_KV
    w1 = [w1k.reshape(span, st, _HD, hid), w1v.reshape(span, st, _HD, hid)]
    pe = [pek.reshape(span, st, _HD), pev.reshape(span, st, _HD)]
    wbig = jnp.zeros((st, slots, _HD, span, slots, hid), F32)
    pex = jnp.zeros((8, st, slots, _HD), F32)
    w2big = jnp.zeros((slots, hid, slots, _HD), F32)
    for s in range(slots):
        kd = kinds[s]
        wbig = wbig.at[:, s, :, :, s, :].set(w1[kd].transpose(1, 2, 0, 3))
        pex = pex.at[0:span, :, s, :].set(pe[kd])
        w2big = w2big.at[s, :, s, :].set([w2k, w2v][kd])
    return (wbig.reshape(st * slots * _HD, span * slots * hid).astype(BF16),
            pex.reshape(8, st * slots * _HD).astype(BF16),
            w2big.reshape(slots * hid, slots * _HD).astype(BF16))


def _compress(cmpflat_bf, wbig, pex, w2big):
    batch, nseg, kdim = cmpflat_bf.shape
    a = _mm(cmpflat_bf.reshape(batch * nseg, kdim), wbig).reshape(batch, nseg, -1)
    pb = _mm(pex, wbig)
    n2 = a.shape[2]
    ow = w2big.shape[1]
    return pl.pallas_call(
        functools.partial(_cmp_fin_kernel, nseg=nseg),
        grid=(batch,),
        in_specs=[pl.BlockSpec((1, nseg, n2), lambda b: (b, 0, 0)), _full(pb.shape), _full(w2big.shape)],
        out_specs=pl.BlockSpec((1, nseg, ow), lambda b: (b, 0, 0)),
        out_shape=jax.ShapeDtypeStruct((batch, nseg, ow), F32),
        compiler_params=_cp("parallel"),
        name="compress_fin",
    )(a, pb, w2big)


def _block_mean_kernel(k_ref, o_ref):
    o_ref[0, 0] = jnp.sum(k_ref[0], axis=0, keepdims=True) * (1.0 / MOBA_BLOCK)


def _block_mean(rows3, nbf, col_block):
    batch = rows3.shape[0]
    wd = _C_MK
    return pl.pallas_call(
        _block_mean_kernel,
        grid=(batch, nbf),
        in_specs=[pl.BlockSpec((1, MOBA_BLOCK, wd), lambda b, n: (b, n, col_block))],
        out_specs=pl.BlockSpec((1, 1, 1, wd), lambda b, n: (b, n, 0, 0)),
        out_shape=jax.ShapeDtypeStruct((batch, nbf, 1, wd), F32),
        compiler_params=_cp("parallel", "parallel"),
        name="block_mean",
    )(rows3)


def _top_rounds(score, row_id, k):
    far = 1 << 20
    sel = jnp.zeros(score.shape, F32)
    cur = score
    for _ in range(k):
        m = jnp.max(cur, axis=0, keepdims=True)
        idx = jnp.min(jnp.where(cur == m, row_id, far), axis=0, keepdims=True)
        pick = row_id == idx
        sel = jnp.where(pick, 1.0, sel)
        cur = jnp.where(pick, -jnp.inf, cur)
    return sel


def _softmax0(s, mask):
    s = jnp.where(mask, s, NEG)
    e = jnp.where(mask, jnp.exp(s - jnp.max(s, axis=0, keepdims=True)), 0.0)
    return e / jnp.maximum(jnp.sum(e, axis=0, keepdims=True), 1e-30)


def _tdot(a, b):
    return lax.dot_general(a, b, (((0,), (0,)), ((), ())), preferred_element_type=F32)


def _nsa_front(qu, kc, vc, mt, tpos, tq, nc, n_sel):
    nc_pad = kc.shape[0]
    nsel_pad = mt.shape[0]
    ci = lax.broadcasted_iota(jnp.int32, (nc_pad, 1), 0)
    bi = lax.broadcasted_iota(jnp.int32, (nsel_pad, tq), 0)
    cur = tpos[:, :tq] >> _SEL_SHIFT
    cmask = ((ci * NSA_CMP_STRIDE + (NSA_CMP_LEN - 1)) <= tpos) & (ci < nc)
    pc = _softmax0(_dot_hi(kc, qu), cmask)
    o_c = _tdot(vc.astype(BF16), pc.astype(BF16))
    imp = pc[:, 0:tq]
    for r in range(1, NSA_GRP):
        imp = imp + pc[:, r * tq:(r + 1) * tq]
    score = _dot_hi(mt, imp)
    forced = (bi == 0) | (bi == cur) | (bi == cur - 1)
    score = jnp.where(forced, BIG, score)
    score = jnp.where(bi <= cur, score, -BIG)
    score = jnp.where(bi < n_sel, score, -3e38)
    sel = _top_rounds(score, bi, min(NSA_TOPN, n_sel))
    return o_c, jnp.where(bi <= cur, sel, 0.0)


def _nsa_window(qr, kw, vw, tpos, wpos):
    wmask = (wpos <= tpos) & (wpos >= tpos - NSA_WINDOW) & (wpos >= 0)
    pw = _softmax0(_dot(kw.astype(BF16), qr), wmask).astype(BF16)
    return _tdot(vw.astype(BF16), pw)


def _online_softmax(m_ref, l_ref, acc_ref, s, mask, pv, guard=True):
    s = jnp.where(mask, s, NEG)
    m_old = m_ref[...]
    m_new = jnp.maximum(m_old, jnp.max(s, axis=0, keepdims=True))
    alpha = jnp.exp(m_old - m_new)
    p = jnp.exp(s - m_new)
    if guard:
        p = jnp.where(mask, p, 0.0)
    l_ref[...] = alpha * l_ref[...] + jnp.sum(p, axis=0, keepdims=True)
    acc_ref[...] = alpha * acc_ref[...] + pv(p.astype(BF16))
    m_ref[...] = m_new


def _heads_t(x, tq, groups, rep):
    heads = []
    for p in range(groups * rep // 2):
        xt = x[:, p * LANES:(p + 1) * LANES].T
        heads += [xt[0:_HD], xt[_HD:2 * _HD]]
    return [jnp.concatenate(heads[g * rep:(g + 1) * rep], axis=1) for g in range(groups)]


def _heads_untranspose(o_ref, per_group, tq, rep):
    heads = [og[:, r * tq:(r + 1) * tq] for og in per_group for r in range(rep)]
    for p in range(len(heads) // 2):
        o_ref[:, p * LANES:(p + 1) * LANES] = jnp.concatenate([heads[2 * p], heads[2 * p + 1]], axis=0).T


def _block_diag(qs):
    z = jnp.zeros_like(qs[0])
    return jnp.concatenate([jnp.concatenate([q if h == g else z for h in range(len(qs))], axis=1)
                            for g, q in enumerate(qs)], axis=0)


def _nsa_kernel(qu_ref, qr_ref, ng_ref, kcvc_ref, mt_ref, ksv_ref, win_ref,
                o_ref, sel_scr, m_scr, l_scr, acc_scr, *, tq, nc, n_sel, nkt_max, wl):
    qi = pl.program_id(1)
    n = NSA_GRP * tq
    tk = KEY_TILE
    kw_n = NSA_KV * _HD
    scale = _HD ** -0.5
    t0 = qi * tq
    lane = lax.broadcasted_iota(jnp.int32, (1, n), 1)
    tpos = t0 + (lane & (tq - 1))
    nkt = jnp.minimum((t0 + tq - 1) // tk + 1, nkt_max)
    wstart = pl.multiple_of(jnp.maximum(t0 - NSA_WINDOW, 0), LANES)
    wpos = wstart + lax.broadcasted_iota(jnp.int32, (wl, 1), 0)

    qus = _heads_t(qu_ref[...] * scale, tq, NSA_KV, NSA_GRP)
    qrs = [q.astype(BF16) for q in _heads_t(qr_ref[...] * scale, tq, NSA_KV, NSA_GRP)]
    gl_t = _sigmoid(ng_ref[...].T)
    gates = [[jnp.concatenate([gl_t[(g * NSA_GRP + r) * 3 + j:(g * NSA_GRP + r) * 3 + j + 1] for r in range(NSA_GRP)],
                              axis=1) for j in range(3)] for g in range(NSA_KV)]
    kcvc = kcvc_ref[0]
    win = win_ref[0, pl.ds(wstart, wl), :]

    o_cw = []
    for g in range(NSA_KV):
        hs = slice(g * _HD, (g + 1) * _HD)
        o_c, sel = _nsa_front(qus[g], kcvc[:, hs], kcvc[:, kw_n + g * _HD:kw_n + (g + 1) * _HD], mt_ref[...],
                              tpos, tq, nc, n_sel)
        sel_scr[g] = sel
        o_w = _nsa_window(qrs[g], win[:, hs], win[:, kw_n + g * _HD:kw_n + (g + 1) * _HD], tpos, wpos)
        o_cw.append(gates[g][0] * o_c + gates[g][2] * o_w)
    m_scr[...] = jnp.full(m_scr.shape, NEG, F32)
    l_scr[...] = jnp.zeros(l_scr.shape, F32)
    acc_scr[...] = jnp.zeros(acc_scr.shape, F32)
    qbd = _block_diag(qrs)

    def tile(j, causal):
        nb = tk // NSA_SEL_BLOCK
        rows = pl.ds(pl.multiple_of(j * tk, tk), tk)
        ksv = ksv_ref[0, rows, :]
        masks = []
        for g in range(NSA_KV):
            pieces = [jnp.broadcast_to(sel_scr[g, pl.ds(j * nb + u, 1), :], (NSA_SEL_BLOCK, tq)) for u in range(nb)]
            mk = jnp.concatenate(pieces, axis=0)
            masks.append(jnp.concatenate([mk] * NSA_GRP, axis=1) > 0.5)
        mask = jnp.concatenate(masks, axis=1)
        if causal:
            kpos = j * tk + lax.broadcasted_iota(jnp.int32, (tk, 1), 0)
            mask = mask & jnp.concatenate([kpos <= tpos] * NSA_KV, axis=1)
        vb = ksv[:, kw_n:].astype(BF16)
        _online_softmax(m_scr, l_scr, acc_scr, _dot(ksv[:, :kw_n].astype(BF16), qbd), mask,
                        lambda p: _tdot(vb, p), guard=False)

    def body(j, carry):
        tile(j, False)
        return carry

    lax.fori_loop(0, nkt - 1, body, 0)
    tile(nkt - 1, True)
    o_s = acc_scr[...] / jnp.maximum(l_scr[...], 1e-30)
    outs = [o_cw[g] + gates[g][1] * o_s[g * _HD:(g + 1) * _HD, g * n:(g + 1) * n] for g in range(NSA_KV)]
    _heads_untranspose(o_ref, outs, tq, NSA_GRP)


def _nsa_select_kernel(qu_ref, qr_ref, gl_ref, kc_ref, vc_ref, mt_ref, kw_ref, vw_ref,
                       o_ref, g1_ref, sel_ref, *, tq, nc, n_sel, qpos0, wpos0):
    n = NSA_GRP * tq
    lane = lax.broadcasted_iota(jnp.int32, (1, n), 1)
    tpos = qpos0 + (lane & (tq - 1))
    wpos = wpos0 + lax.broadcasted_iota(jnp.int32, (kw_ref.shape[2], 1), 0)
    for g in range(NSA_KV):
        qr = qr_ref[0, 0, g].astype(BF16)
        o_c, sel = _nsa_front(qu_ref[0, 0, g], kc_ref[0, g], vc_ref[0, g], mt_ref[...], tpos, tq, nc, n_sel)
        sel_ref[0, g] = sel
        o_w = _nsa_window(qr, kw_ref[0, g], vw_ref[0, g], tpos, wpos)
        gate = _sigmoid(gl_ref[0, 0, g])
        o_ref[0, g] = gate[0:1] * o_c + gate[2:3] * o_w
        g1_ref[0, g] = gate[1:2]


def _moba_select(q, kb, own, nbf):
    ni = lax.broadcasted_iota(jnp.int32, (kb.shape[0], q.shape[1]), 0)
    gs = _dot_hi(kb, q)
    gs = jnp.where(ni < own, gs, -BIG)
    gs = jnp.where(ni < nbf, gs, -3e38)
    sel = _top_rounds(gs, ni, min(MOBA_TOPK, nbf))
    return jnp.where((ni < own) & (ni < nbf), sel, 0.0)


def _moba_select_kernel(q_ref, kb_ref, sel_ref, *, own, nbf):
    for g in range(MOBA_KV):
        sel_ref[0, g] = _moba_select(q_ref[0, 0, g], kb_ref[0, g], own, nbf)


def _moba_kernel(q_ref, kb_ref, kmv_ref, o_ref, sel_scr, m_scr, l_scr, acc_scr, *, tq, nbf, nkt_max):
    qi = pl.program_id(1)
    n = MOBA_GRP * tq
    tk = KEY_TILE
    assert tk == MOBA_BLOCK
    t0 = qi * tq
    lane = lax.broadcasted_iota(jnp.int32, (1, MOBA_KV * n), 1)
    tpos = t0 + (lane & (tq - 1))
    own = t0 // MOBA_BLOCK
    scale = _HD ** -0.5

    qs = _heads_t(q_ref[...], tq, MOBA_KV, MOBA_GRP)
    for g in range(MOBA_KV):
        sel_scr[g] = _moba_select(qs[g], kb_ref[0, g], own, nbf)
    qbd = _block_diag([(q * scale).astype(BF16) for q in qs])
    m_scr[...] = jnp.full(m_scr.shape, NEG, F32)
    l_scr[...] = jnp.zeros(l_scr.shape, F32)
    acc_scr[...] = jnp.zeros(acc_scr.shape, F32)

    def tile(j, mask):
        kmv = kmv_ref[0, pl.ds(pl.multiple_of(j * tk, tk), tk), :]
        vb = kmv[:, _C_MK:].astype(BF16)
        _online_softmax(m_scr, l_scr, acc_scr, _dot(kmv[:, :_C_MK].astype(BF16), qbd), mask,
                        lambda p: _tdot(vb, p), guard=False)

    kpos = own * tk + lax.broadcasted_iota(jnp.int32, (tk, 1), 0)
    tile(own, kpos <= tpos)

    def body(j, carry):
        picked = jnp.concatenate([jnp.broadcast_to(sel_scr[g, pl.ds(j, 1), :], (tk, n)) for g in range(MOBA_KV)],
                                 axis=1)
        tile(j, picked > 0.5)
        return carry

    lax.fori_loop(0, jnp.minimum(own, nkt_max), body, 0)
    o = acc_scr[...] / jnp.maximum(l_scr[...], 1e-30)
    _heads_untranspose(o_ref, [o[g * _HD:(g + 1) * _HD, g * n:(g + 1) * n] for g in range(MOBA_KV)], tq, MOBA_GRP)


def _to_qt(x, batch, nt, tq, groups, rep):
    x = x.reshape(batch, nt, tq, groups, rep, _HD).transpose(0, 1, 3, 5, 4, 2)
    return x.reshape(batch, nt, groups, _HD, rep * tq)


def _from_qt(x, batch, nt, tq, groups, rep):
    x = x.reshape(batch, nt, groups, _HD, rep, tq).transpose(0, 1, 5, 2, 4, 3)
    return x.reshape(batch, nt * tq, groups * rep * _HD)


def _head_major(x, heads):
    b, l, _ = x.shape
    return x.reshape(b, l, heads, _HD).transpose(0, 2, 1, 3)


def _vt_tiles(x, heads, tile):
    b, l, _ = x.shape
    return x.reshape(b, l // tile, tile, heads, _HD).transpose(0, 3, 1, 4, 2)


def _sel_score_matrix(nsel_pad, nc_pad, nc):
    ratio = NSA_SEL_BLOCK // NSA_CMP_STRIDE
    span = NSA_CMP_LEN // NSA_CMP_STRIDE
    mt = np.zeros((nsel_pad, nc_pad), np.float32)
    for j in range(nsel_pad):
        for o in range(-(span - 1), ratio):
            c = j * ratio + o
            wt = sum(1 for m in range(ratio) for q in range(span) if m - q == o)
            if 0 <= c < nc:
                mt[j, c] = wt
    return jnp.asarray(mt)


def _mixer_c_attention(qu, qr, mq, ng, rows3, win3, kcvc, batch, seq):
    tk = KEY_TILE
    tq = LANES
    assert seq % tk == 0 and seq >= NSA_WINDOW + tq
    nt = seq // tq
    nkt_max = seq // tk
    nc = (seq - NSA_CMP_LEN) // NSA_CMP_STRIDE + 1
    nseg = kcvc.shape[1]
    n_sel = seq // NSA_SEL_BLOCK
    nsel_pad = -(-n_sel // 8) * 8
    nbf = seq // MOBA_BLOCK
    nbf_pad = -(-nbf // 8) * 8
    wl = NSA_WINDOW + tq
    mt = _sel_score_matrix(nsel_pad, nseg, nc)
    kw_n = NSA_KV * _HD

    kb = _block_mean(rows3, nbf, col_block=2)[:, :, 0, :]
    kb = _head_major(jnp.pad(kb, ((0, 0), (0, nbf_pad - nbf), (0, 0))), MOBA_KV)

    n_n = NSA_GRP * tq
    n_m = MOBA_GRP * tq
    tile = lambda wd: pl.BlockSpec((tq, wd), lambda b, i: (b * nt + i, 0))
    per_b = lambda shp, cb=0: pl.BlockSpec((1,) + shp, lambda b, i: (b,) + (0,) * (len(shp) - 1) + (cb,))
    o_n = pl.pallas_call(
        functools.partial(_nsa_kernel, tq=tq, nc=nc, n_sel=n_sel, nkt_max=nkt_max, wl=wl),
        grid=(batch, nt),
        in_specs=[tile(NSA_HEADS * _HD), tile(NSA_HEADS * _HD), tile(LANES), per_b((nseg, 2 * kw_n)), _full(mt.shape),
                  per_b((seq, 2 * kw_n), 1), per_b((seq, 2 * kw_n))],
        out_specs=tile(NSA_HEADS * _HD),
        out_shape=jax.ShapeDtypeStruct((batch * seq, NSA_HEADS * _HD), F32),
        scratch_shapes=[pltpu.VMEM((NSA_KV, nsel_pad, tq), F32), pltpu.VMEM((1, NSA_KV * n_n), F32),
                        pltpu.VMEM((1, NSA_KV * n_n), F32), pltpu.VMEM((kw_n, NSA_KV * n_n), F32)],
        compiler_params=_cp("parallel", "arbitrary"),
        name="nsa",
    )(qu, qr, ng, kcvc, mt, rows3, win3)

    o_m = pl.pallas_call(
        functools.partial(_moba_kernel, tq=tq, nbf=nbf, nkt_max=nkt_max),
        grid=(batch, nt),
        in_specs=[tile(MOBA_HEADS * _HD), per_b(kb.shape[1:]), per_b((seq, 2 * _C_MK), 1)],
        out_specs=tile(MOBA_HEADS * _HD),
        out_shape=jax.ShapeDtypeStruct((batch * seq, MOBA_HEADS * _HD), F32),
        scratch_shapes=[pltpu.VMEM((MOBA_KV, nbf_pad, n_m), F32), pltpu.VMEM((1, MOBA_KV * n_m), F32),
                        pltpu.VMEM((1, MOBA_KV * n_m), F32), pltpu.VMEM((_C_MK, MOBA_KV * n_m), F32)],
        compiler_params=_cp("parallel", "arbitrary"),
        name="moba",
    )(mq, kb, rows3)
    return o_n, o_m


PAGES_PER_STEP = 4
_SEG_PER_PAGE = PAGE_SIZE // NSA_CMP_STRIDE


def _page_specs(npages, n_layers, layer, width, col_block):
    def spec(pg):
        return pl.BlockSpec(
            (1, PAGE_SIZE, width),
            lambda b, j, p: (p[b * npages + j * PAGES_PER_STEP + pg] * n_layers + layer, 0, col_block))
    return [spec(pg) for pg in range(PAGES_PER_STEP)]


def _scan_kernel(pt_ref, *refs):
    pps = PAGES_PER_STEP
    ck_refs, cv_refs, mk_refs = refs[0:pps], refs[pps:2 * pps], refs[2 * pps:3 * pps]
    flat_ref, kbar_ref = refs[3 * pps:]
    wd = 2 * _C_KV
    for pair in range(pps // 2):
        r0 = 2 * pair * _SEG_PER_PAGE
        for i in range(NSA_CMP_STRIDE):
            for half, src in enumerate((ck_refs, cv_refs)):
                x = jnp.concatenate([src[2 * pair][0, pl.ds(i, _SEG_PER_PAGE, stride=NSA_CMP_STRIDE), :],
                                     src[2 * pair + 1][0, pl.ds(i, _SEG_PER_PAGE, stride=NSA_CMP_STRIDE), :]], axis=0)
                c0 = i * wd + half * _C_KV
                flat_ref[0, r0:r0 + 2 * _SEG_PER_PAGE, c0:c0 + _C_KV] = x.astype(BF16)
        ka, kb = mk_refs[2 * pair][0], mk_refs[2 * pair + 1][0]
        kbar_ref[0, pair] = (jnp.sum(ka, axis=0, keepdims=True)
                             + jnp.sum(kb, axis=0, keepdims=True)) * (1.0 / MOBA_BLOCK)


def _scan_pages(pool2, pt, layer, n_layers):
    batch, npages = pt.shape
    assert npages % PAGES_PER_STEP == 0 and MOBA_BLOCK == 2 * PAGE_SIZE
    ng = npages // PAGES_PER_STEP
    nseg = npages * _SEG_PER_PAGE
    nbf = npages // 2
    wd = 2 * _C_KV
    gs = pltpu.PrefetchScalarGridSpec(
        num_scalar_prefetch=1,
        grid=(batch, ng),
        in_specs=(_page_specs(npages, n_layers, layer, _C_KV, 0) + _page_specs(npages, n_layers, layer, _C_KV, 1)
                  + _page_specs(npages, n_layers, layer, _C_MK, 2)),
        out_specs=[pl.BlockSpec((1, PAGES_PER_STEP * _SEG_PER_PAGE, NSA_CMP_STRIDE * wd), lambda b, j, p: (b, j, 0)),
                   pl.BlockSpec((1, PAGES_PER_STEP // 2, 1, _C_MK), lambda b, j, p: (b, j, 0, 0))],
    )
    flat, kbar = pl.pallas_call(
        _scan_kernel,
        grid_spec=gs,
        out_shape=[jax.ShapeDtypeStruct((batch, nseg, NSA_CMP_STRIDE * wd), BF16),
                   jax.ShapeDtypeStruct((batch, nbf, 1, _C_MK), F32)],
        compiler_params=_cp("parallel", "arbitrary"),
        name="scan_pages",
    )(pt.reshape(-1), *([pool2] * (3 * PAGES_PER_STEP)))
    return flat, kbar[:, :, 0, :]


def _flash_update(m_ref, l_ref, acc_ref, k, v, q, mask):
    vb = v.astype(BF16)
    _online_softmax(m_ref, l_ref, acc_ref, _dot(k.astype(BF16), q), mask,
                    lambda p: lax.dot_general(vb, p, (((0,), (0,)), ((), ())), preferred_element_type=F32))


def _dec_attn_kernel(pt_ref, *refs, past, nsteps, seq_pad):
    pps = PAGES_PER_STEP
    sel_refs = refs[0:pps]
    mob_refs = refs[pps:2 * pps]
    (qn_ref, qm_ref, smask_ref, mmask_ref, nsel_ref, nmob_ref, opart_ref, g1_ref,
     on_ref, om_ref, mn, ln, accn, mm, lm, accm) = refs[2 * pps:]
    j = pl.program_id(1)
    kw_n = NSA_KV * _HD
    kw_m = _C_MK

    @pl.when(j == 0)
    def _():
        mn[...] = jnp.full(mn.shape, NEG, F32)
        mm[...] = jnp.full(mm.shape, NEG, F32)
        ln[...] = jnp.zeros(ln.shape, F32)
        lm[...] = jnp.zeros(lm.shape, F32)
        accn[...] = jnp.zeros(accn.shape, F32)
        accm[...] = jnp.zeros(accm.shape, F32)

    qn = qn_ref[0]
    qm = qm_ref[0]
    rows = pps * PAGE_SIZE
    ksv = jnp.concatenate([r[0] for r in sel_refs], axis=0)
    nb = rows // NSA_SEL_BLOCK
    srow = smask_ref[0, pl.ds(pl.multiple_of(j * nb, nb), nb), :]
    smask = jnp.concatenate([jnp.broadcast_to(srow[u:u + 1], (NSA_SEL_BLOCK, LANES)) for u in range(nb)], axis=0)
    _flash_update(mn, ln, accn, ksv[:, :kw_n], ksv[:, kw_n:], qn, smask > 0.5)

    kmv = jnp.concatenate([r[0] for r in mob_refs], axis=0)
    nbm = rows // MOBA_BLOCK
    mmask = jnp.concatenate([jnp.broadcast_to(mmask_ref[0, pl.ds(j * nbm + u, 1), :], (MOBA_BLOCK, LANES))
                             for u in range(nbm)], axis=0)
    _flash_update(mm, lm, accm, kmv[:, :kw_m], kmv[:, kw_m:], qm, mmask > 0.5)

    @pl.when(j == nsteps - 1)
    def _():
        lane = lax.broadcasted_iota(jnp.int32, (1, LANES), 1)
        tpos = past + (lane & (seq_pad - 1))
        new_rows = nsel_ref.shape[1]
        kpos = past + lax.broadcasted_iota(jnp.int32, (new_rows, 1), 0)
        causal = kpos <= tpos
        new_blk = past // NSA_SEL_BLOCK
        nrow = jnp.broadcast_to(smask_ref[0, new_blk:new_blk + 1, :], (new_rows, LANES)) > 0.5
        nsel = nsel_ref[0]
        _flash_update(mn, ln, accn, nsel[:, :kw_n], nsel[:, kw_n:], qn, causal & nrow)
        nmob = nmob_ref[0]
        _flash_update(mm, lm, accm, nmob[:, :kw_m], nmob[:, kw_m:], qm, causal)
        on_ref[0] = opart_ref[0] + g1_ref[0] * (accn[...] / jnp.maximum(ln[...], 1e-30))
        om_ref[0] = accm[...] / jnp.maximum(lm[...], 1e-30)


def _block_diag_q(x, batch, seq, groups, rep):
    x = x.reshape(batch, seq, groups, rep, _HD).transpose(0, 2, 4, 3, 1).reshape(batch, groups, _HD, rep * seq)
    bd = jnp.einsum('bgdn,gh->bgdhn', x, jnp.eye(groups, dtype=x.dtype))
    bd = bd.reshape(batch, groups * _HD, groups * rep * seq)
    return jnp.pad(bd, ((0, 0), (0, 0), (0, LANES - bd.shape[2])))


def _diag_blocks(o, batch, seq, groups, rep):
    o = o[:, :, :groups * rep * seq].reshape(batch, groups, _HD, groups, rep, seq)
    o = jnp.stack([o[:, g, :, g] for g in range(groups)], axis=1)
    return o.transpose(0, 4, 1, 3, 2).reshape(batch * seq, groups * rep * _HD)


def _decode_attention(qu, qr, mq, ng, rows3, win3, pool, pt, cache_win, layer, wbig, pex, w2big, batch, seq):
    npages = pt.shape[1]
    past = npages * PAGE_SIZE
    lt = past + seq
    assert seq <= 8 and seq < NSA_CMP_STRIDE and past % MOBA_BLOCK == 0
    seq_pad = 8
    tq = 64
    n_layers = pool.shape[1]
    pool2 = pool.reshape(pool.shape[0] * n_layers, PAGE_SIZE, ROW_W)
    scale = _HD ** -0.5

    cmpflat, kbar = _scan_pages(pool2, pt, layer, n_layers)
    kcvc = _compress(cmpflat, wbig, pex, w2big)
    nseg = kcvc.shape[1]
    nc = (lt - NSA_CMP_LEN) // NSA_CMP_STRIDE + 1
    assert nseg == nc + 1
    n_sel = -(-lt // NSA_SEL_BLOCK)
    nsel_pad = -(-n_sel // 8) * 8
    nbf = lt // MOBA_BLOCK
    nbf_pad = -(-(nbf + 1) // 8) * 8
    ng_steps = npages // PAGES_PER_STEP

    def padq(x):
        x = x.reshape(batch, seq, -1)
        return jnp.pad(x, ((0, 0), (0, tq - seq), (0, 0))).reshape(batch * tq, -1)

    qu_t = _to_qt(padq(qu) * scale, batch, 1, tq, NSA_KV, NSA_GRP)
    qr_t = _to_qt(padq(qr) * scale, batch, 1, tq, NSA_KV, NSA_GRP)
    mq_t = _to_qt(padq(mq), batch, 1, tq, MOBA_KV, MOBA_GRP)
    gl = padq(ng)[:, :NSA_HEADS * 3].reshape(batch, 1, tq, NSA_KV, NSA_GRP, 3).transpose(0, 1, 3, 5, 4, 2)
    gl = gl.reshape(batch, 1, NSA_KV, 3, NSA_GRP * tq)
    kc = _head_major(kcvc[:, :, 0:_C_KV], NSA_KV)
    vc = _head_major(kcvc[:, :, _C_KV:2 * _C_KV], NSA_KV)
    mt = _sel_score_matrix(nsel_pad, nseg, nc)
    wb = cache_win.reshape(batch, -1, 2 * _C_KV)
    wcat = jnp.concatenate([wb, win3], axis=1)
    wl = -(-wcat.shape[1] // LANES) * LANES
    wfull = jnp.pad(wcat, ((0, 0), (0, wl - wcat.shape[1]), (0, 0)))
    kw = _head_major(wfull[:, :, 0:_C_KV], NSA_KV).astype(BF16)
    vw = _head_major(wfull[:, :, _C_KV:], NSA_KV).astype(BF16)
    kb = _head_major(jnp.pad(kbar, ((0, 0), (0, nbf_pad - nbf), (0, 0))), MOBA_KV)

    n_n = NSA_GRP * tq
    n_m = MOBA_GRP * tq
    b1 = lambda shp: pl.BlockSpec((1,) + shp, lambda b: (b,) + (0,) * len(shp))
    o_part, g1, sel = pl.pallas_call(
        functools.partial(_nsa_select_kernel, tq=tq, nc=nc, n_sel=n_sel, qpos0=past, wpos0=past - wb.shape[1]),
        grid=(batch,),
        in_specs=[b1((1, NSA_KV, _HD, n_n)), b1((1, NSA_KV, _HD, n_n)), b1((1, NSA_KV, 3, n_n)),
                  b1(kc.shape[1:]), b1(vc.shape[1:]), _full(mt.shape), b1(kw.shape[1:]), b1(vw.shape[1:])],
        out_specs=[b1((NSA_KV, _HD, n_n)), b1((NSA_KV, 1, n_n)), b1((NSA_KV, nsel_pad, tq))],
        out_shape=[jax.ShapeDtypeStruct((batch, NSA_KV, _HD, n_n), F32),
                   jax.ShapeDtypeStruct((batch, NSA_KV, 1, n_n), F32),
                   jax.ShapeDtypeStruct((batch, NSA_KV, nsel_pad, tq), F32)],
        compiler_params=_cp("parallel"),
        name="nsa_select",
    )(qu_t, qr_t, gl, kc, vc, mt, kw, vw)
    msel = pl.pallas_call(
        functools.partial(_moba_select_kernel, own=past // MOBA_BLOCK, nbf=nbf),
        grid=(batch,),
        in_specs=[b1((1, MOBA_KV, _HD, n_m)), b1(kb.shape[1:])],
        out_specs=b1((MOBA_KV, nbf_pad, n_m)),
        out_shape=jax.ShapeDtypeStruct((batch, MOBA_KV, nbf_pad, n_m), F32),
        compiler_params=_cp("parallel"),
        name="moba_select",
    )(mq_t, kb)

    def lanes(x):
        return jnp.pad(x, [(0, 0)] * (x.ndim - 1) + [(0, LANES - x.shape[-1])])

    pad_s = lambda x: jnp.pad(x.reshape(batch, seq, -1), ((0, 0), (0, seq_pad - seq), (0, 0)))
    qn_bd = _block_diag_q(pad_s(qr).reshape(batch * seq_pad, -1) * scale, batch, seq_pad, NSA_KV, NSA_GRP).astype(BF16)
    qm_bd = _block_diag_q(pad_s(mq).reshape(batch * seq_pad, -1) * scale, batch, seq_pad, MOBA_KV, MOBA_GRP).astype(BF16)
    smask = jnp.broadcast_to(sel[:, :, :, None, :seq_pad], (batch, NSA_KV, nsel_pad, NSA_GRP, seq_pad))
    smask = lanes(smask.transpose(0, 2, 1, 3, 4).reshape(batch, nsel_pad, -1))
    mmask = msel.reshape(batch, MOBA_KV, nbf_pad, MOBA_GRP, tq)[..., :seq_pad]
    mmask = lanes(mmask.transpose(0, 2, 1, 3, 4).reshape(batch, nbf_pad, -1))
    op = o_part.reshape(batch, NSA_KV, _HD, NSA_GRP, tq)[..., :seq_pad]
    op_bd = jnp.einsum('bgdrt,gh->bgdhrt', op, jnp.eye(NSA_KV, dtype=F32))
    op_bd = lanes(op_bd.reshape(batch, NSA_KV * _HD, -1))
    g1l = g1.reshape(batch, NSA_KV, NSA_GRP, tq)[..., :seq_pad].reshape(batch, 1, -1)
    g1l = lanes(g1l)
    new_pad = 16
    pad_n = lambda x: jnp.pad(x, ((0, 0), (0, new_pad - seq), (0, 0)))
    new_sel = pad_n(rows3[:, :, 2 * _C_KV:4 * _C_KV])
    new_mob = pad_n(rows3[:, :, 4 * _C_KV:])

    kw_n = NSA_KV * _HD
    bj = lambda shp: pl.BlockSpec((1,) + shp, lambda b, j, p: (b,) + (0,) * len(shp))
    gs = pltpu.PrefetchScalarGridSpec(
        num_scalar_prefetch=1,
        grid=(batch, ng_steps),
        in_specs=(_page_specs(npages, n_layers, layer, 2 * kw_n, 1) + _page_specs(npages, n_layers, layer, 2 * _C_MK, 1)
                  + [bj((kw_n, LANES)), bj((_C_MK, LANES)), bj((nsel_pad, LANES)), bj((nbf_pad, LANES)),
                     bj((new_pad, 2 * kw_n)), bj((new_pad, 2 * _C_MK)), bj((kw_n, LANES)), bj((1, LANES))]),
        out_specs=[bj((kw_n, LANES)), bj((_C_MK, LANES))],
        scratch_shapes=[pltpu.VMEM((1, LANES), F32), pltpu.VMEM((1, LANES), F32), pltpu.VMEM((kw_n, LANES), F32),
                        pltpu.VMEM((1, LANES), F32), pltpu.VMEM((1, LANES), F32), pltpu.VMEM((_C_MK, LANES), F32)],
    )
    o_n, o_m = pl.pallas_call(
        functools.partial(_dec_attn_kernel, past=past, nsteps=ng_steps, seq_pad=seq_pad),
        grid_spec=gs,
        out_shape=[jax.ShapeDtypeStruct((batch, kw_n, LANES), F32), jax.ShapeDtypeStruct((batch, _C_MK, LANES), F32)],
        compiler_params=_cp("parallel", "arbitrary"),
        name="decode_attn",
    )(pt.reshape(-1), *([pool2] * (2 * PAGES_PER_STEP)), qn_bd, qm_bd, smask, mmask, new_sel, new_mob, op_bd, g1l)
    o_n = _diag_blocks(o_n, batch, seq_pad, NSA_KV, NSA_GRP).reshape(batch, seq_pad, -1)[:, :seq]
    o_m = _diag_blocks(o_m, batch, seq_pad, MOBA_KV, MOBA_GRP).reshape(batch, seq_pad, -1)[:, :seq]
    return o_n.reshape(batch * seq, -1), o_m.reshape(batch * seq, -1)


def _prep_weights(prm):
    w = {}
    w['w_in_a'] = prm['w_in_a'].astype(BF16)
    w['w_out_a'] = prm['w_out_a'].astype(BF16)
    w['w_out_c'] = prm['w_out_c'].astype(BF16)
    w['w_in_c'] = [_proj_c_weights(prm['w_in_c'][c]) for c in range(prm['w_in_c'].shape[0])]
    w['cmp'] = [_compress_weights(prm['cmp_w1_k'][c], prm['cmp_w1_v'][c], prm['cmp_w2_k'][c], prm['cmp_w2_v'][c],
                                  prm['cmp_pe_k'][c], prm['cmp_pe_v'][c]) for c in range(prm['w_in_c'].shape[0])]
    depth, d, _ = prm['router_c_w'].shape
    wr = jnp.concatenate([prm['router_c_w'], prm['router_f_w'].transpose(0, 2, 1, 3).reshape(depth, d, -1)], axis=2)
    w['router_w'] = jnp.pad(wr, ((0, 0), (0, 0), (0, LANES - wr.shape[2])))
    br = jnp.concatenate([prm['router_c_b'], prm['router_f_b'].reshape(depth, -1)], axis=1)
    w['router_b'] = jnp.pad(br, ((0, 0), (0, LANES - br.shape[1])))[:, None, :]
    w['moe_w1'] = prm['moe_w1']
    w['moe_w3'] = prm['moe_w3']
    w['moe_w2'] = prm['moe_w2']
    w['ple_w'] = prm['ple_w'].astype(BF16)
    w['ple_gate_w'] = prm['ple_gate_w'].astype(BF16)
    w['lb'] = jnp.cumsum(jax.nn.softmax(prm['hgrn_lb'].astype(F32), axis=0), axis=0)
    return w


def _forward(x, p, pos, prm, w, ctx):
    batch, seq, d = x.shape
    t = batch * seq
    depth = p.shape[0]
    h = x.reshape(t, d)
    outs = {}
    for i in range(depth):
        gmix = prm['norm_mix'][i][None, :]
        if i % 2 == 0:
            a = i // 2
            lb512 = jnp.tile(w['lb'][a], HG_HEADS)[None, :]
            q, k, lf, iv, sg, gu, vn = _proj_a(h, gmix, w['w_in_a'][a], lb512, prm['cm_vnorm'][a][None, :])
            if ctx is None:
                st0 = jnp.zeros((batch, HG_WIDTH, HG_WIDTH), F32)
            else:
                s0 = ctx['state_hgrn'][a].astype(F32)
                eye = jnp.eye(HG_HEADS, dtype=F32)
                st0 = jnp.einsum('bhde,hg->bhegd', s0, eye).reshape(batch, HG_WIDTH, HG_WIDTH)
            o, st = _hgrn(q, k, lf, iv, sg, st0, prm['hgrn_onorm'][a][None, :], batch, seq)
            st5 = st.reshape(batch, HG_HEADS, HG_DK, HG_HEADS, HG_DK)
            s_new = jnp.stack([st5[:, hh, :, hh, :] for hh in range(HG_HEADS)], axis=1).transpose(0, 1, 3, 2)
            cm = _cmix(vn, gu, prm['cm_ws'][a], prm['cm_bs'][a].T, batch, seq)
            h = _out_proj(h, o, cm, w['w_out_a'][a])
            outs.setdefault('hg', []).append(s_new)
            outs.setdefault('cm', []).append(vn.reshape(batch, seq, CM_WIDTH))
        else:
            c = i // 2
            qu, qr, mq, ng, rows, win = _proj_c(h, gmix, w['w_in_c'][c], pos, seq)
            rows3 = rows.reshape(batch, seq, ROW_W)
            win3 = win.reshape(batch, seq, 2 * _C_KV)
            wbig, pex, w2big = w['cmp'][c]
            if ctx is None:
                assert seq % KEY_TILE == 0
                nseg = seq // NSA_CMP_STRIDE
                cmpflat = rows3[:, :, 0:2 * _C_KV].reshape(batch, nseg, -1).astype(BF16)
                kcvc = _compress(cmpflat, wbig, pex, w2big)
                o_n, o_m = _mixer_c_attention(qu, qr, mq, ng, rows3, win3, kcvc, batch, seq)
                new_win = win3[:, seq - min(NSA_WINDOW, seq):]
            else:
                o_n, o_m = _decode_attention(qu, qr, mq, ng, rows3, win3, ctx['cache_kv'], ctx['page_table'],
                                             ctx['cache_win'][c], c, wbig, pex, w2big, batch, seq)
                new_win = win3
            h = _out_proj(h, o_n, o_m, w['w_out_c'][c])
            outs.setdefault('kv', []).append(rows3.reshape(batch, seq, KV_ROW_HEADS, _HD))
            outs.setdefault('win', []).append(new_win.reshape(batch, -1, 2 * NSA_KV, _HD))
        y = _moe(h, prm['norm_ffn'][i][None, :], w['router_w'][i], w['router_b'][i],
                 w['moe_w1'][i], w['moe_w3'][i], w['moe_w2'][i])
        h = _ple(h, y, p[i].reshape(t, -1), prm['norm_ple'][i][None, :], w['ple_gate_w'][i], w['ple_w'][i],
                 prm['final_norm'][None, :], final=(i == depth - 1))
    return h.reshape(batch, seq, d), outs


def kernel(x_prompt, x_sample, cache_kv, cache_win, state_hgrn, page_table, p_prompt, p_sample,
           norm_mix, norm_ffn, norm_ple, final_norm, w_in_a, w_out_a, hgrn_lb, hgrn_onorm, cm_vnorm,
           cm_ws, cm_bs, w_in_c, w_out_c, cmp_pe_k, cmp_w1_k, cmp_w2_k, cmp_pe_v, cmp_w1_v, cmp_w2_v,
           router_c_w, router_c_b, router_f_w, router_f_b, moe_w1, moe_w3, moe_w2, ple_w, ple_gate_w):
    prm = dict(norm_mix=norm_mix, norm_ffn=norm_ffn, norm_ple=norm_ple, final_norm=final_norm,
               w_in_a=w_in_a, w_out_a=w_out_a, hgrn_lb=hgrn_lb, hgrn_onorm=hgrn_onorm, cm_vnorm=cm_vnorm,
               cm_ws=cm_ws, cm_bs=cm_bs, w_in_c=w_in_c, w_out_c=w_out_c, cmp_pe_k=cmp_pe_k,
               cmp_w1_k=cmp_w1_k, cmp_w2_k=cmp_w2_k, cmp_pe_v=cmp_pe_v, cmp_w1_v=cmp_w1_v, cmp_w2_v=cmp_w2_v,
               router_c_w=router_c_w, router_c_b=router_c_b, router_f_w=router_f_w, router_f_b=router_f_b,
               moe_w1=moe_w1, moe_w3=moe_w3, moe_w2=moe_w2, ple_w=ple_w, ple_gate_w=ple_gate_w)
    w = _prep_weights(prm)
    past = page_table.shape[1] * PAGE_SIZE
    y_p, o_p = _forward(x_prompt, p_prompt, jnp.arange(x_prompt.shape[1]), prm, w, None)
    ctx = dict(cache_kv=cache_kv, cache_win=cache_win, state_hgrn=state_hgrn, page_table=page_table)
    y_s, o_s = _forward(x_sample, p_sample, past + jnp.arange(x_sample.shape[1]), prm, w, ctx)
    return (y_p, y_s,
            jnp.stack(o_p['kv'], axis=1), jnp.stack(o_s['kv'], axis=1),
            jnp.stack(o_p['win'], axis=0), jnp.stack(o_s['win'], axis=0),
            jnp.stack(o_p['hg'], axis=0), jnp.stack(o_s['hg'], axis=0),
            jnp.stack(o_s['cm'], axis=0))
```

```python
import functools
import math

import numpy as np
import jax
import jax.numpy as jnp
from jax import lax
from jax.experimental import pallas as pl
from jax.experimental.pallas import tpu as pltpu

F32 = jnp.float32
BF16 = jnp.bfloat16
HI = lax.Precision.HIGHEST

PAGE_SIZE = 128
HEAD_DIM = 64
ROT_DIM = HEAD_DIM // 4
ROPE_THETA = 500000.0
HG_HEADS = 8
HG_DK = 64
HG_WIDTH = HG_HEADS * HG_DK
HG_CHUNK = 64
CM_GROUPS = 4
CM_GW = 128
CM_WIDTH = CM_GROUPS * CM_GW
CM_CHUNK = 128
NSA_HEADS = 8
NSA_KV = 2
NSA_GRP = NSA_HEADS // NSA_KV
NSA_CMP_LEN = 32
NSA_CMP_STRIDE = 16
NSA_SEL_BLOCK = 64
NSA_TOPN = 16
NSA_WINDOW = 512
MOBA_HEADS = 8
MOBA_KV = 4
MOBA_GRP = MOBA_HEADS // MOBA_KV
MOBA_BLOCK = 256
MOBA_TOPK = 3
MOE_GROUPS = 4
MOE_EPG = 8
MOE_EXPERTS = MOE_GROUPS * MOE_EPG
KV_ROW_HEADS = 16
ROW_W = KV_ROW_HEADS * HEAD_DIM
A_SPLITS = [HG_WIDTH] * 4 + [CM_WIDTH] * 2
C_SPLITS = [NSA_HEADS * HEAD_DIM] + [NSA_KV * HEAD_DIM] * 6 + [NSA_HEADS * 3, MOBA_HEADS * HEAD_DIM,
                                                              MOBA_KV * HEAD_DIM, MOBA_KV * HEAD_DIM]
NEG = -1e30
BIG = 1e9
EPS = 1e-6
KEY_TILE = 256
LANES = 128
_MOBA_SHIFT = MOBA_BLOCK.bit_length() - 1
_SEL_SHIFT = NSA_SEL_BLOCK.bit_length() - 1
VMEM_LIMIT = 56 * 1024 * 1024


def _cp(*sem):
    return pltpu.CompilerParams(dimension_semantics=sem, vmem_limit_bytes=VMEM_LIMIT)


def _sigmoid(x):
    return 1.0 / (1.0 + jnp.exp(-x))


def _silu(x):
    return x * _sigmoid(x)


def _gelu(x):
    return 0.5 * x * (1.0 + jnp.tanh(math.sqrt(2.0 / math.pi) * (x + 0.044715 * (x * x * x))))


def _rms(x, g):
    return x * lax.rsqrt(jnp.mean(x * x, axis=-1, keepdims=True) + EPS) * g


def _dot(a, b):
    return jnp.dot(a, b, preferred_element_type=F32)


def _dot_hi(a, b):
    return jnp.dot(a, b, precision=HI, preferred_element_type=F32)


def _dot3(a, b):
    a_hi = a.astype(BF16)
    b_hi = b.astype(BF16)
    a_lo = (a - a_hi.astype(F32)).astype(BF16)
    b_lo = (b - b_hi.astype(F32)).astype(BF16)
    return _dot(a_hi, b_hi) + (_dot(a_hi, b_lo) + _dot(a_lo, b_hi))


def _full(shape):
    n = len(shape)
    return pl.BlockSpec(shape, lambda *_: (0,) * n)


def _row_tile(t):
    for tm in (256, 128, 64, 32, 16, 8):
        if t % tm == 0:
            return tm
    raise ValueError(f"token count {t} is not a multiple of 8")


def _proj_a_kernel(x_ref, g_ref, w_ref, lb_ref, vg_ref, q_ref, k_ref, lf_ref, iv_ref, sg_ref, gu_ref, vn_ref):
    xn = _rms(x_ref[...], g_ref[...]).astype(BF16)
    z = _dot(xn, w_ref[...])
    w = HG_WIDTH
    q_ref[...] = _silu(z[:, 0:w])
    lb = lb_ref[...]
    f = lb + (1.0 - lb) * _sigmoid(z[:, w:2 * w])
    k_ref[...] = 1.0 - f
    lf_ref[...] = jnp.log(f)
    iv_ref[...] = z[:, 2 * w:3 * w]
    sg_ref[...] = _silu(z[:, 3 * w:4 * w])
    gu_ref[...] = _gelu(z[:, 4 * w:4 * w + CM_WIDTH])
    v = _gelu(z[:, 4 * w + CM_WIDTH:])
    for gi in range(CM_GROUPS):
        sl = slice(gi * CM_GW, (gi + 1) * CM_GW)
        vn_ref[:, sl] = _rms(v[:, sl], vg_ref[:, sl])


def _proj_a(x, g, w_bf, lb512, vgain):
    t, d = x.shape
    tm = _row_tile(t)
    n = w_bf.shape[1]
    row = lambda wd: pl.BlockSpec((tm, wd), lambda i: (i, 0))
    outs = [jax.ShapeDtypeStruct((t, HG_WIDTH), F32)] * 5 + [jax.ShapeDtypeStruct((t, CM_WIDTH), F32)] * 2
    return pl.pallas_call(
        _proj_a_kernel,
        grid=(t // tm,),
        in_specs=[row(d), _full((1, d)), _full((d, n)), _full((1, HG_WIDTH)), _full((1, CM_WIDTH))],
        out_specs=[row(HG_WIDTH)] * 5 + [row(CM_WIDTH)] * 2,
        out_shape=outs,
        compiler_params=_cp("parallel"),
        name="proj_a",
    )(x, g, w_bf, lb512, vgain)


def _hgrn_kernel(q_ref, k_ref, lf_ref, v_ref, sg_ref, s0_ref, gain_ref, bones_ref, bmask_ref, tri_ref,
                 o_ref, sout_ref, st_scr, p_scr, *, chunk):
    c = pl.program_id(1)
    w = HG_WIDTH
    hw = w // 2

    @pl.when(c == 0)
    def _():
        st_scr[...] = s0_ref[0]

    q = q_ref[0]
    k = k_ref[0]
    v = v_ref[0]
    b = _dot_hi(tri_ref[...], lf_ref[0])
    bones = bones_ref[...]

    def head_sum(x):
        xb = x.astype(BF16)
        return jnp.concatenate([_dot(xb[:, :hw], bones), _dot(xb[:, hw:], bones)], axis=1)

    o_rows = []
    for blk in range(chunk // 8):
        s_len = 8 * (blk + 1)
        bs = b[:s_len]
        ks = k[:s_len]
        vs = v[:s_len]
        row_id = lax.broadcasted_iota(jnp.int32, (s_len, w), 0)
        for t in range(8):
            r = 8 * blk + t
            diff = jnp.where(row_id <= r, b[r:r + 1, :] - bs, NEG)
            p_scr[t * s_len:(t + 1) * s_len, :] = jnp.exp(diff) * q[r:r + 1, :] * ks
        att = head_sum(p_scr[0:8 * s_len, :])
        for t in range(8):
            o_rows.append(jnp.sum(att[t * s_len:(t + 1) * s_len] * vs, axis=0, keepdims=True))
    o_intra = jnp.concatenate(o_rows, axis=0)

    st = st_scr[...]
    qe = (q * jnp.exp(b)).astype(BF16)
    o = o_intra + lax.dot_general(qe, st.astype(BF16), (((1,), (1,)), ((), ())), preferred_element_type=F32)

    b_end = b[chunk - 1:chunk, :]
    kd = (k * jnp.exp(b_end - b)).astype(BF16)
    upd = lax.dot_general(v.astype(BF16), kd, (((0,), (0,)), ((), ())), preferred_element_type=F32)
    st_new = st * jnp.exp(b_end) + upd * bmask_ref[...]
    st_scr[...] = st_new

    o2 = o * o
    hi = o2.astype(BF16).astype(F32)
    ms = (head_sum(hi) + head_sum(o2 - hi)) * (1.0 / HG_DK)
    o_ref[0] = o * lax.rsqrt(ms + EPS) * gain_ref[...] * sg_ref[0]

    @pl.when(c == pl.num_programs(1) - 1)
    def _():
        sout_ref[0] = st_new


def _block_ones(n, blk, dtype):
    i = np.arange(n) // blk
    return jnp.asarray((i[:, None] == i[None, :]).astype(np.float32), dtype)


def _hgrn(q, k, lf, v, sg, st0, gain, batch, seq):
    w = HG_WIDTH
    chunk = math.gcd(seq, HG_CHUNK)
    nch = seq // chunk
    r3 = lambda a: a.reshape(batch, seq, w)
    tile = pl.BlockSpec((1, chunk, w), lambda b, c: (b, c, 0))
    state = pl.BlockSpec((1, w, w), lambda b, c: (b, 0, 0))
    tri = jnp.asarray(np.tril(np.ones((chunk, chunk), np.float32)))
    o, st = pl.pallas_call(
        functools.partial(_hgrn_kernel, chunk=chunk),
        grid=(batch, nch),
        in_specs=[tile] * 5 + [state, _full((1, w)), _full((w // 2, w // 2)), _full((w, w)), _full((chunk, chunk))],
        out_specs=[tile, state],
        out_shape=[jax.ShapeDtypeStruct((batch, seq, w), F32), jax.ShapeDtypeStruct((batch, w, w), F32)],
        scratch_shapes=[pltpu.VMEM((w, w), F32), pltpu.VMEM((8 * chunk, w), F32)],
        compiler_params=_cp("parallel", "arbitrary"),
        name="hgrn",
    )(r3(q), r3(k), r3(lf), r3(v), r3(sg), st0, gain, _block_ones(w // 2, HG_DK, BF16),
      _block_ones(w, HG_DK, F32), tri)
    return o.reshape(batch * seq, w), st


def _cmix_kernel(vn_ref, gu_ref, ws_ref, bst_ref, o_ref, *, tl):
    ri = lax.broadcasted_iota(jnp.int32, (tl, tl), 0)
    ci = lax.broadcasted_iota(jnp.int32, (tl, tl), 1)
    for gi in range(CM_GROUPS):
        sl = slice(gi * CM_GW, (gi + 1) * CM_GW)
        wg = jnp.where(ci <= ri, ws_ref[gi, 0:tl, 0:tl], 0.0).astype(BF16)
        mix = _dot(wg, vn_ref[0, :, sl].astype(BF16)) + bst_ref[0:tl, gi:gi + 1]
        o_ref[0, :, sl] = gu_ref[0, :, sl] * mix


def _cmix(vn, gu, ws, bst, batch, seq):
    tl = min(seq, CM_CHUNK)
    assert seq % tl == 0 and tl % 8 == 0
    w = CM_WIDTH
    tile = pl.BlockSpec((1, tl, w), lambda b, c: (b, c, 0))
    out = pl.pallas_call(
        functools.partial(_cmix_kernel, tl=tl),
        grid=(batch, seq // tl),
        in_specs=[tile, tile, _full(ws.shape), _full(bst.shape)],
        out_specs=tile,
        out_shape=jax.ShapeDtypeStruct((batch, seq, w), F32),
        compiler_params=_cp("parallel", "parallel"),
        name="cmix",
    )(vn.reshape(batch, seq, w), gu.reshape(batch, seq, w), ws, bst)
    return out.reshape(batch * seq, w)


def _out_proj_kernel(res_ref, a1_ref, a2_ref, w1_ref, w2_ref, o_ref):
    o_ref[...] = (res_ref[...] + _dot(a1_ref[...].astype(BF16), w1_ref[...])
                  + _dot(a2_ref[...].astype(BF16), w2_ref[...]))


def _out_proj(res, a1, a2, w_bf):
    t, d = res.shape
    k1, k2 = a1.shape[1], a2.shape[1]
    tm = _row_tile(t)
    row = lambda wd: pl.BlockSpec((tm, wd), lambda i: (i, 0))
    return pl.pallas_call(
        _out_proj_kernel,
        grid=(t // tm,),
        in_specs=[row(d), row(k1), row(k2), _full((k1, d)), _full((k2, d))],
        out_specs=row(d),
        out_shape=jax.ShapeDtypeStruct((t, d), F32),
        compiler_params=_cp("parallel"),
        name="out_proj",
    )(res, a1, a2, w_bf[:k1], w_bf[k1:])


def _router_kernel(h_ref, g_ref, w_ref, b_ref, xn_ref, info_ref):
    xn = _rms(h_ref[...], g_ref[...])
    xn_ref[...] = xn.astype(BF16)
    logit = _dot3(xn, w_ref[...]) + b_ref[...]
    lane = lax.broadcasted_iota(jnp.int32, logit.shape, 1)
    far = 1 << 20

    def first_max(vals, mask):
        m = jnp.max(jnp.where(mask, vals, -1.0), axis=-1, keepdims=True)
        idx = jnp.min(jnp.where(mask & (vals == m), lane, far), axis=-1, keepdims=True)
        return m, idx

    def softmax(mask):
        m = jnp.max(jnp.where(mask, logit, NEG), axis=-1, keepdims=True)
        e = jnp.where(mask, jnp.exp(logit - m), 0.0)
        return e / jnp.sum(e, axis=-1, keepdims=True)

    cmask = lane < MOE_GROUPS
    pg, grp = first_max(softmax(cmask), cmask)
    lo = MOE_GROUPS + grp * MOE_EPG
    fmask = (lane >= lo) & (lane < lo + MOE_EPG)
    pf = softmax(fmask)
    v1, i1 = first_max(pf, fmask)
    v2, i2 = first_max(pf, fmask & (lane != i1))
    den = v1 + v2
    info = jnp.where(lane == 0, (i1 - MOE_GROUPS).astype(F32), 0.0)
    info = jnp.where(lane == 1, (i2 - MOE_GROUPS).astype(F32), info)
    info = jnp.where(lane == 2, pg * v1 / den, info)
    info = jnp.where(lane == 3, pg * v2 / den, info)
    info_ref[...] = info


def _router(h, g, w_r, b_r):
    t, d = h.shape
    tm = _row_tile(t)
    row = lambda wd: pl.BlockSpec((tm, wd), lambda i: (i, 0))
    return pl.pallas_call(
        _router_kernel,
        grid=(t // tm,),
        in_specs=[row(d), _full((1, d)), _full((d, LANES)), _full((1, LANES))],
        out_specs=[row(d), row(LANES)],
        out_shape=[jax.ShapeDtypeStruct((t, d), BF16), jax.ShapeDtypeStruct((t, LANES), F32)],
        compiler_params=_cp("parallel"),
        name="router",
    )(h, g, w_r, b_r)


def _moe_ffn_kernel(be_ref, nu_ref, x_ref, w1_ref, w3_ref, w2_ref, o_ref, w1b, w3b, w2b):
    i = pl.program_id(0)

    @pl.when((i == 0) | (be_ref[i] != be_ref[jnp.maximum(i - 1, 0)]))
    def _():
        w1b[...] = w1_ref[0].astype(BF16)
        w3b[...] = w3_ref[0].astype(BF16)
        w2b[...] = w2_ref[0].astype(BF16)

    @pl.when(i < nu_ref[0])
    def _():
        x = x_ref[...]
        hdn = _silu(_dot(x, w1b[...])) * _dot(x, w3b[...])
        o_ref[...] = _dot(hdn.astype(BF16), w2b[...])

    @pl.when(i >= nu_ref[0])
    def _():
        o_ref[...] = jnp.zeros_like(o_ref)


def _moe_ffn(xp, blk_e, nused, w1, w3, w2, rb):
    n, d = xp.shape
    ff = w1.shape[2]
    nblk = n // rb
    gs = pltpu.PrefetchScalarGridSpec(
        num_scalar_prefetch=2,
        grid=(nblk,),
        in_specs=[pl.BlockSpec((rb, d), lambda i, be, nu: (i, 0)),
                  pl.BlockSpec((1, d, ff), lambda i, be, nu: (be[i], 0, 0)),
                  pl.BlockSpec((1, d, ff), lambda i, be, nu: (be[i], 0, 0)),
                  pl.BlockSpec((1, ff, d), lambda i, be, nu: (be[i], 0, 0))],
        out_specs=pl.BlockSpec((rb, d), lambda i, be, nu: (i, 0)),
        scratch_shapes=[pltpu.VMEM((d, ff), BF16), pltpu.VMEM((d, ff), BF16), pltpu.VMEM((ff, d), BF16)],
    )
    return pl.pallas_call(
        _moe_ffn_kernel,
        grid_spec=gs,
        out_shape=jax.ShapeDtypeStruct((n, d), F32),
        compiler_params=_cp("arbitrary"),
        name="moe_ffn",
    )(blk_e, nused, xp, w1, w3, w2)


def _moe(h, g, w_r, b_r, w1, w3, w2):
    t, d = h.shape
    xn, info = _router(h, g, w_r, b_r)
    eid = info[:, 0:2].astype(jnp.int32)
    gate = info[:, 2:4]
    ne = MOE_EXPERTS
    tk = 2 * t
    rb = 256 if t >= 4096 else 32
    flat = eid.reshape(-1)
    order = jnp.argsort(flat).astype(jnp.int32)
    rank = jnp.argsort(order).astype(jnp.int32)
    counts = jnp.sum((flat[:, None] == jnp.arange(ne)[None, :]).astype(jnp.int32), axis=0)
    padc = (counts + rb - 1) // rb * rb
    pend = jnp.cumsum(padc)
    pstart = pend - padc
    cstart = jnp.cumsum(counts) - counts
    slot = (pstart[flat] + rank - cstart[flat]).reshape(t, 2)
    nblk = -(-tk // rb) + ne
    blk_e = jnp.minimum(jnp.sum((pend[None, :] <= (jnp.arange(nblk) * rb)[:, None]).astype(jnp.int32), axis=1),
                        ne - 1).astype(jnp.int32)
    pe = jnp.repeat(blk_e, rb)
    off = jnp.arange(nblk * rb) - pstart[pe]
    rows = jnp.where(off < counts[pe], order[jnp.clip(off + cstart[pe], 0, tk - 1)] // 2, 0)
    xp = xn[rows]
    nused = (pend[-1:] // rb).astype(jnp.int32)
    yp = _moe_ffn(xp, blk_e, nused, w1, w3, w2, rb)
    return gate[:, 0:1] * yp[slot[:, 0]] + gate[:, 1:2] * yp[slot[:, 1]]


def _ple_kernel(h_ref, y_ref, p_ref, g_ref, wg_ref, wp_ref, fg_ref, o_ref, *, final):
    h = h_ref[...] + y_ref[...]
    gate = _sigmoid(_dot(_rms(h, g_ref[...]).astype(BF16), wg_ref[...]))
    out = h + gate * _dot(p_ref[...].astype(BF16), wp_ref[...])
    if final:
        out = _rms(out, fg_ref[...])
    o_ref[...] = out


def _ple(h, y, p, g, wg_bf, wp_bf, fg, final):
    t, d = h.shape
    pd = p.shape[1]
    tm = _row_tile(t)
    row = lambda wd: pl.BlockSpec((tm, wd), lambda i: (i, 0))
    return pl.pallas_call(
        functools.partial(_ple_kernel, final=final),
        grid=(t // tm,),
        in_specs=[row(d), row(d), row(pd), _full((1, d)), _full((d, d)), _full((pd, d)), _full((1, d))],
        out_specs=row(d),
        out_shape=jax.ShapeDtypeStruct((t, d), F32),
        compiler_params=_cp("parallel"),
        name="ple",
    )(h, y, p, g, wg_bf, wp_bf, fg)


_HD = HEAD_DIM
_C_NQ = NSA_HEADS * _HD
_C_KV = NSA_KV * _HD
_C_MQ = MOBA_HEADS * _HD
_C_MK = MOBA_KV * _HD
_O_NQ = 0
_O_CK = _O_NQ + _C_NQ
_O_CV = _O_CK + _C_KV
_O_SK = _O_CV + _C_KV
_O_SV = _O_SK + _C_KV
_O_WK = _O_SV + _C_KV
_O_WV = _O_WK + _C_KV
_O_MQ = _O_WV + _C_KV
_O_MK = _O_MQ + _C_MQ
_O_MV = _O_MK + _C_MK
_O_NG = _O_MV + _C_MK
_O_NQS = _O_NG + LANES
_O_SKS = _O_NQS + _C_NQ
_O_WKS = _O_SKS + _C_KV
_O_MQS = _O_WKS + _C_KV
_O_MKS = _O_MQS + _C_MQ
_C_TOTAL = _O_MKS + _C_MK


def _proj_c_kernel(x_ref, g_ref, w_ref, cos_ref, sin_ref, qu_ref, qr_ref, mq_ref, ng_ref, rows_ref, win_ref,
                   *flat_refs):
    xn = _rms(x_ref[...], g_ref[...]).astype(BF16)
    z = _dot(xn, w_ref[...])
    cs = cos_ref[...]
    sn = sin_ref[...]

    def rope(o, os, wd):
        reps = wd // LANES
        return z[:, o:o + wd] * jnp.tile(cs, (1, reps)) + z[:, os:os + wd] * jnp.tile(sn, (1, reps))

    qu_ref[...] = z[:, _O_NQ:_O_NQ + _C_NQ]
    qr_ref[...] = rope(_O_NQ, _O_NQS, _C_NQ)
    mq_ref[...] = rope(_O_MQ, _O_MQS, _C_MQ)
    ng_ref[...] = z[:, _O_NG:_O_NG + LANES]
    rows_ref[:, 0:2 * _C_KV] = z[:, _O_CK:_O_CK + 2 * _C_KV]
    rows_ref[:, 2 * _C_KV:3 * _C_KV] = rope(_O_SK, _O_SKS, _C_KV)
    rows_ref[:, 3 * _C_KV:4 * _C_KV] = z[:, _O_SV:_O_SV + _C_KV]
    rows_ref[:, 4 * _C_KV:4 * _C_KV + _C_MK] = rope(_O_MK, _O_MKS, _C_MK)
    rows_ref[:, 4 * _C_KV + _C_MK:] = z[:, _O_MV:_O_MV + _C_MK]
    win_ref[:, 0:_C_KV] = rope(_O_WK, _O_WKS, _C_KV)
    win_ref[:, _C_KV:] = z[:, _O_WV:_O_WV + _C_KV]
    if not flat_refs:
        return
    flat_ref, ck_scr, cv_scr = flat_refs
    ck_scr[...] = z[:, _O_CK:_O_CK + _C_KV]
    cv_scr[...] = z[:, _O_CV:_O_CV + _C_KV]
    nseg = flat_ref.shape[0]
    for i in range(NSA_CMP_STRIDE):
        c0 = i * 2 * _C_KV
        flat_ref[:, c0:c0 + _C_KV] = ck_scr[pl.ds(i, nseg, stride=NSA_CMP_STRIDE), :].astype(BF16)
        flat_ref[:, c0 + _C_KV:c0 + 2 * _C_KV] = cv_scr[pl.ds(i, nseg, stride=NSA_CMP_STRIDE), :].astype(BF16)


def _proj_c_weights(w):
    offs = np.concatenate([[0], np.cumsum(C_SPLITS)])
    nq, ck, cv, sk, sv, wk, wv, ng, mq, mk, mv = [w[:, offs[i]:offs[i + 1]] for i in range(len(C_SPLITS))]
    half = ROT_DIM // 2

    def swapped(m):
        d = m.shape[0]
        m3 = m.reshape(d, -1, _HD)
        out = jnp.concatenate([m3[..., half:ROT_DIM], m3[..., :half], jnp.zeros_like(m3[..., ROT_DIM:])], axis=-1)
        return out.reshape(d, -1)

    ngp = jnp.pad(ng, ((0, 0), (0, LANES - ng.shape[1])))
    return jnp.concatenate([nq, ck, cv, sk, sv, wk, wv, mq, mk, mv, ngp,
                            swapped(nq), swapped(sk), swapped(wk), swapped(mq), swapped(mk)], axis=1).astype(BF16)


def _rope_tables(pos):
    half = ROT_DIM // 2
    inv = ROPE_THETA ** (-jnp.arange(half, dtype=F32) / half)
    ang = pos.astype(F32)[:, None] * inv
    cos, sin = jnp.cos(ang), jnp.sin(ang)
    n = pos.shape[0]
    c64 = jnp.concatenate([cos, cos, jnp.ones((n, _HD - ROT_DIM), F32)], axis=1)
    s64 = jnp.concatenate([-sin, sin, jnp.zeros((n, _HD - ROT_DIM), F32)], axis=1)
    return jnp.tile(c64, (1, LANES // _HD)), jnp.tile(s64, (1, LANES // _HD))


def _proj_c(x, g, wc_bf, pos, seq, want_flat):
    t, d = x.shape
    tm = _row_tile(t)
    tr = max(seq, tm)
    assert tr % tm == 0 and tr % seq == 0
    cs, sn = _rope_tables(jnp.tile(pos, tr // seq))
    ntab = tr // tm
    row = lambda wd: pl.BlockSpec((tm, wd), lambda i: (i, 0))
    tab = pl.BlockSpec((tm, LANES), lambda i: (i % ntab, 0))
    widths = [_C_NQ, _C_NQ, _C_MQ, LANES, ROW_W, 2 * _C_KV]
    st = NSA_CMP_STRIDE
    fw = st * 2 * _C_KV
    out_specs = [row(wd) for wd in widths]
    out_shape = [jax.ShapeDtypeStruct((t, wd), F32) for wd in widths]
    scratch = []
    if want_flat:
        assert tm % (16 * st) == 0 and seq % st == 0
        out_specs.append(pl.BlockSpec((tm // st, fw), lambda i: (i, 0)))
        out_shape.append(jax.ShapeDtypeStruct((t // st, fw), BF16))
        scratch = [pltpu.VMEM((tm, _C_KV), F32), pltpu.VMEM((tm, _C_KV), F32)]
    return pl.pallas_call(
        _proj_c_kernel,
        grid=(t // tm,),
        in_specs=[row(d), _full((1, d)), _full((d, _C_TOTAL)), tab, tab],
        out_specs=out_specs,
        out_shape=out_shape,
        scratch_shapes=scratch,
        compiler_params=_cp("parallel"),
        name="proj_c",
    )(x, g, wc_bf, cs, sn)


def _mm_kernel(a_ref, w_ref, o_ref):
    o_ref[...] = _dot(a_ref[...], w_ref[...])


def _mm(a_bf, w_bf):
    m, k = a_bf.shape
    n = w_bf.shape[1]
    tm = _row_tile(m)
    return pl.pallas_call(
        _mm_kernel,
        grid=(m // tm,),
        in_specs=[pl.BlockSpec((tm, k), lambda i: (i, 0)), _full((k, n))],
        out_specs=pl.BlockSpec((tm, n), lambda i: (i, 0)),
        out_shape=jax.ShapeDtypeStruct((m, n), F32),
        compiler_params=_cp("parallel"),
        name="mm",
    )(a_bf, w_bf)


def _cmp_fin_kernel(a_ref, pb_ref, w2_ref, o_ref, *, nseg):
    a = a_ref[0]
    hw = a.shape[1] // 2
    bias = pb_ref[0:1, :hw] + pb_ref[1:2, hw:]
    pre = a[:, :hw] + pltpu.roll(a[:, hw:], nseg - 1, axis=0) + bias
    o_ref[0] = _dot(_gelu(pre).astype(BF16), w2_ref[...])


def _compress_weights(w1k, w1v, w2k, w2v, pek, pev):
    span = NSA_CMP_LEN // NSA_CMP_STRIDE
    st = NSA_CMP_STRIDE
    hid = w1k.shape[1]
    slots = 2 * NSA_KV
    kinds = [0] * NSA_KV + [1] * NSA_KV
    w1 = [w1k.reshape(span, st, _HD, hid), w1v.reshape(span, st, _HD, hid)]
    pe = [pek.reshape(span, st, _HD), pev.reshape(span, st, _HD)]
    wbig = jnp.zeros((st, slots, _HD, span, slots, hid), F32)
    pex = jnp.zeros((8, st, slots, _HD), F32)
    w2big = jnp.zeros((slots, hid, slots, _HD), F32)
    for s in range(slots):
        kd = kinds[s]
        wbig = wbig.at[:, s, :, :, s, :].set(w1[kd].transpose(1, 2, 0, 3))
        pex = pex.at[0:span, :, s, :].set(pe[kd])
        w2big = w2big.at[s, :, s, :].set([w2k, w2v][kd])
    return (wbig.reshape(st * slots * _HD, span * slots * hid).astype(BF16),
            pex.reshape(8, st * slots * _HD).astype(BF16),
            w2big.reshape(slots * hid, slots * _HD).astype(BF16))


def _compress(cmpflat_bf, wbig, pex, w2big):
    batch, nseg, kdim = cmpflat_bf.shape
    a = _mm(cmpflat_bf.reshape(batch * nseg, kdim), wbig).reshape(batch, nseg, -1)
    pb = _mm(pex, wbig)
    n2 = a.shape[2]
    ow = w2big.shape[1]
    return pl.pallas_call(
        functools.partial(_cmp_fin_kernel, nseg=nseg),
        grid=(batch,),
        in_specs=[pl.BlockSpec((1, nseg, n2), lambda b: (b, 0, 0)), _full(pb.shape), _full(w2big.shape)],
        out_specs=pl.BlockSpec((1, nseg, ow), lambda b: (b, 0, 0)),
        out_shape=jax.ShapeDtypeStruct((batch, nseg, ow), F32),
        compiler_params=_cp("parallel"),
        name="compress_fin",
    )(a, pb, w2big)


def _block_mean_kernel(k_ref, o_ref):
    o_ref[0, 0] = jnp.sum(k_ref[0], axis=0, keepdims=True) * (1.0 / MOBA_BLOCK)


def _block_mean(rows3, nbf, col_block):
    batch = rows3.shape[0]
    wd = _C_MK
    return pl.pallas_call(
        _block_mean_kernel,
        grid=(batch, nbf),
        in_specs=[pl.BlockSpec((1, MOBA_BLOCK, wd), lambda b, n: (b, n, col_block))],
        out_specs=pl.BlockSpec((1, 1, 1, wd), lambda b, n: (b, n, 0, 0)),
        out_shape=jax.ShapeDtypeStruct((batch, nbf, 1, wd), F32),
        compiler_params=_cp("parallel", "parallel"),
        name="block_mean",
    )(rows3)


def _top_rounds(score, row_id, k):
    far = 1 << 20
    sel = jnp.zeros(score.shape, F32)
    cur = score
    for _ in range(k):
        m = jnp.max(cur, axis=0, keepdims=True)
        idx = jnp.min(jnp.where(cur == m, row_id, far), axis=0, keepdims=True)
        pick = row_id == idx
        sel = jnp.where(pick, 1.0, sel)
        cur = jnp.where(pick, -jnp.inf, cur)
    return sel


def _softmax0(s, mask):
    s = jnp.where(mask, s, NEG)
    e = jnp.where(mask, jnp.exp(s - jnp.max(s, axis=0, keepdims=True)), 0.0)
    return e / jnp.maximum(jnp.sum(e, axis=0, keepdims=True), 1e-30)


def _tdot(a, b):
    return lax.dot_general(a, b, (((0,), (0,)), ((), ())), preferred_element_type=F32)


def _nsa_front(qu, kc, vc, mt, tpos, tq, nc, n_sel):
    nc_pad = kc.shape[0]
    nsel_pad = mt.shape[0]
    ci = lax.broadcasted_iota(jnp.int32, (nc_pad, 1), 0)
    bi = lax.broadcasted_iota(jnp.int32, (nsel_pad, tq), 0)
    cur = tpos[:, :tq] >> _SEL_SHIFT
    cmask = ((ci * NSA_CMP_STRIDE + (NSA_CMP_LEN - 1)) <= tpos) & (ci < nc)
    pc = _softmax0(_dot_hi(kc, qu), cmask)
    o_c = _tdot(vc.astype(BF16), pc.astype(BF16))
    imp = pc[:, 0:tq]
    for r in range(1, NSA_GRP):
        imp = imp + pc[:, r * tq:(r + 1) * tq]
    score = _dot_hi(mt, imp)
    forced = (bi == 0) | (bi == cur) | (bi == cur - 1)
    score = jnp.where(forced, BIG, score)
    score = jnp.where(bi <= cur, score, -BIG)
    score = jnp.where(bi < n_sel, score, -3e38)
    sel = _top_rounds(score, bi, min(NSA_TOPN, n_sel))
    return o_c, jnp.where(bi <= cur, sel, 0.0)


def _nsa_window(qr, kw, vw, tpos, wpos):
    wmask = (wpos <= tpos) & (wpos >= tpos - NSA_WINDOW) & (wpos >= 0)
    pw = _softmax0(_dot(kw.astype(BF16), qr), wmask).astype(BF16)
    return _tdot(vw.astype(BF16), pw)


def _online_softmax(m_ref, l_ref, acc_ref, s, mask, pv, guard=True):
    s = jnp.where(mask, s, NEG)
    m_old = m_ref[...]
    m_new = jnp.maximum(m_old, jnp.max(s, axis=0, keepdims=True))
    alpha = jnp.exp(m_old - m_new)
    p = jnp.exp(s - m_new)
    if guard:
        p = jnp.where(mask, p, 0.0)
    l_ref[...] = alpha * l_ref[...] + jnp.sum(p, axis=0, keepdims=True)
    acc_ref[...] = alpha * acc_ref[...] + pv(p.astype(BF16))
    m_ref[...] = m_new


def _heads_t(x, tq, groups, rep):
    heads = []
    for p in range(groups * rep // 2):
        xt = x[:, p * LANES:(p + 1) * LANES].T
        heads += [xt[0:_HD], xt[_HD:2 * _HD]]
    return [jnp.concatenate(heads[g * rep:(g + 1) * rep], axis=1) for g in range(groups)]


def _heads_untranspose(o_ref, per_group, tq, rep):
    heads = [og[:, r * tq:(r + 1) * tq] for og in per_group for r in range(rep)]
    for p in range(len(heads) // 2):
        o_ref[:, p * LANES:(p + 1) * LANES] = jnp.concatenate([heads[2 * p], heads[2 * p + 1]], axis=0).T


def _block_diag(qs):
    z = jnp.zeros_like(qs[0])
    return jnp.concatenate([jnp.concatenate([q if h == g else z for h in range(len(qs))], axis=1)
                            for g, q in enumerate(qs)], axis=0)


def _nsa_kernel(qu_ref, qr_ref, ng_ref, kcvc_ref, mt_ref, ksv_ref, win_ref,
                o_ref, sel_scr, m_scr, l_scr, acc_scr, *, tq, nc, n_sel, nkt_max, wl):
    qi = pl.program_id(1)
    n = NSA_GRP * tq
    tk = KEY_TILE
    kw_n = NSA_KV * _HD
    scale = _HD ** -0.5
    t0 = qi * tq
    lane = lax.broadcasted_iota(jnp.int32, (1, n), 1)
    tpos = t0 + (lane & (tq - 1))
    nkt = jnp.minimum((t0 + tq - 1) // tk + 1, nkt_max)
    wstart = pl.multiple_of(jnp.maximum(t0 - NSA_WINDOW, 0), LANES)
    wpos = wstart + lax.broadcasted_iota(jnp.int32, (wl, 1), 0)

    qus = _heads_t(qu_ref[...] * scale, tq, NSA_KV, NSA_GRP)
    qrs = [q.astype(BF16) for q in _heads_t(qr_ref[...] * scale, tq, NSA_KV, NSA_GRP)]
    gl_t = _sigmoid(ng_ref[...].T)
    gates = [[jnp.concatenate([gl_t[(g * NSA_GRP + r) * 3 + j:(g * NSA_GRP + r) * 3 + j + 1] for r in range(NSA_GRP)],
                              axis=1) for j in range(3)] for g in range(NSA_KV)]
    kcvc = kcvc_ref[0]
    win = win_ref[0, pl.ds(wstart, wl), :]

    o_cw = []
    for g in range(NSA_KV):
        hs = slice(g * _HD, (g + 1) * _HD)
        o_c, sel = _nsa_front(qus[g], kcvc[:, hs], kcvc[:, kw_n + g * _HD:kw_n + (g + 1) * _HD], mt_ref[...],
                              tpos, tq, nc, n_sel)
        sel_scr[g] = sel
        o_w = _nsa_window(qrs[g], win[:, hs], win[:, kw_n + g * _HD:kw_n + (g + 1) * _HD], tpos, wpos)
        o_cw.append(gates[g][0] * o_c + gates[g][2] * o_w)
    m_scr[...] = jnp.full(m_scr.shape, NEG, F32)
    l_scr[...] = jnp.zeros(l_scr.shape, F32)
    acc_scr[...] = jnp.zeros(acc_scr.shape, F32)
    qbd = _block_diag(qrs)

    def tile(j, causal):
        nb = tk // NSA_SEL_BLOCK
        rows = pl.ds(pl.multiple_of(j * tk, tk), tk)
        ksv = ksv_ref[0, rows, :]
        masks = []
        for g in range(NSA_KV):
            pieces = [jnp.broadcast_to(sel_scr[g, pl.ds(j * nb + u, 1), :], (NSA_SEL_BLOCK, tq)) for u in range(nb)]
            mk = jnp.concatenate(pieces, axis=0)
            masks.append(jnp.concatenate([mk] * NSA_GRP, axis=1) > 0.5)
        mask = jnp.concatenate(masks, axis=1)
        if causal:
            kpos = j * tk + lax.broadcasted_iota(jnp.int32, (tk, 1), 0)
            mask = mask & jnp.concatenate([kpos <= tpos] * NSA_KV, axis=1)
        vb = ksv[:, kw_n:].astype(BF16)
        _online_softmax(m_scr, l_scr, acc_scr, _dot(ksv[:, :kw_n].astype(BF16), qbd), mask,
                        lambda p: _tdot(vb, p), guard=False)

    def body(j, carry):
        tile(j, False)
        return carry

    lax.fori_loop(0, nkt - 1, body, 0)
    tile(nkt - 1, True)
    o_s = acc_scr[...] / jnp.maximum(l_scr[...], 1e-30)
    outs = [o_cw[g] + gates[g][1] * o_s[g * _HD:(g + 1) * _HD, g * n:(g + 1) * n] for g in range(NSA_KV)]
    _heads_untranspose(o_ref, outs, tq, NSA_GRP)


def _nsa_select_kernel(qu_ref, qr_ref, gl_ref, kc_ref, vc_ref, mt_ref, kw_ref, vw_ref,
                       o_ref, g1_ref, sel_ref, *, tq, nc, n_sel, qpos0, wpos0):
    n = NSA_GRP * tq
    lane = lax.broadcasted_iota(jnp.int32, (1, n), 1)
    tpos = qpos0 + (lane & (tq - 1))
    wpos = wpos0 + lax.broadcasted_iota(jnp.int32, (kw_ref.shape[2], 1), 0)
    for g in range(NSA_KV):
        qr = qr_ref[0, 0, g].astype(BF16)
        o_c, sel = _nsa_front(qu_ref[0, 0, g], kc_ref[0, g], vc_ref[0, g], mt_ref[...], tpos, tq, nc, n_sel)
        sel_ref[0, g] = sel
        o_w = _nsa_window(qr, kw_ref[0, g], vw_ref[0, g], tpos, wpos)
        gate = _sigmoid(gl_ref[0, 0, g])
        o_ref[0, g] = gate[0:1] * o_c + gate[2:3] * o_w
        g1_ref[0, g] = gate[1:2]


def _moba_select(q, kb, own, nbf):
    ni = lax.broadcasted_iota(jnp.int32, (kb.shape[0], q.shape[1]), 0)
    gs = _dot_hi(kb, q)
    gs = jnp.where(ni < own, gs, -BIG)
    gs = jnp.where(ni < nbf, gs, -3e38)
    sel = _top_rounds(gs, ni, min(MOBA_TOPK, nbf))
    return jnp.where((ni < own) & (ni < nbf), sel, 0.0)


def _moba_select_kernel(q_ref, kb_ref, sel_ref, *, own, nbf):
    for g in range(MOBA_KV):
        sel_ref[0, g] = _moba_select(q_ref[0, 0, g], kb_ref[0, g], own, nbf)


def _moba_kernel(q_ref, kb_ref, kmv_ref, o_ref, sel_scr, m_scr, l_scr, acc_scr, *, tq, nbf, nkt_max):
    qi = pl.program_id(1)
    n = MOBA_GRP * tq
    tk = KEY_TILE
    assert tk == MOBA_BLOCK
    t0 = qi * tq
    lane = lax.broadcasted_iota(jnp.int32, (1, MOBA_KV * n), 1)
    tpos = t0 + (lane & (tq - 1))
    own = t0 // MOBA_BLOCK
    scale = _HD ** -0.5

    qs = _heads_t(q_ref[...], tq, MOBA_KV, MOBA_GRP)
    for g in range(MOBA_KV):
        sel_scr[g] = _moba_select(qs[g], kb_ref[0, g], own, nbf)
    qbd = _block_diag([(q * scale).astype(BF16) for q in qs])
    m_scr[...] = jnp.full(m_scr.shape, NEG, F32)
    l_scr[...] = jnp.zeros(l_scr.shape, F32)
    acc_scr[...] = jnp.zeros(acc_scr.shape, F32)

    def tile(j, mask):
        kmv = kmv_ref[0, pl.ds(pl.multiple_of(j * tk, tk), tk), :]
        vb = kmv[:, _C_MK:].astype(BF16)
        _online_softmax(m_scr, l_scr, acc_scr, _dot(kmv[:, :_C_MK].astype(BF16), qbd), mask,
                        lambda p: _tdot(vb, p), guard=False)

    kpos = own * tk + lax.broadcasted_iota(jnp.int32, (tk, 1), 0)
    tile(own, kpos <= tpos)

    def body(j, carry):
        picked = jnp.concatenate([jnp.broadcast_to(sel_scr[g, pl.ds(j, 1), :], (tk, n)) for g in range(MOBA_KV)],
                                 axis=1)
        tile(j, picked > 0.5)
        return carry

    lax.fori_loop(0, jnp.minimum(own, nkt_max), body, 0)
    o = acc_scr[...] / jnp.maximum(l_scr[...], 1e-30)
    _heads_untranspose(o_ref, [o[g * _HD:(g + 1) * _HD, g * n:(g + 1) * n] for g in range(MOBA_KV)], tq, MOBA_GRP)


def _to_qt(x, batch, nt, tq, groups, rep):
    x = x.reshape(batch, nt, tq, groups, rep, _HD).transpose(0, 1, 3, 5, 4, 2)
    return x.reshape(batch, nt, groups, _HD, rep * tq)


def _from_qt(x, batch, nt, tq, groups, rep):
    x = x.reshape(batch, nt, groups, _HD, rep, tq).transpose(0, 1, 5, 2, 4, 3)
    return x.reshape(batch, nt * tq, groups * rep * _HD)


def _head_major(x, heads):
    b, l, _ = x.shape
    return x.reshape(b, l, heads, _HD).transpose(0, 2, 1, 3)


def _vt_tiles(x, heads, tile):
    b, l, _ = x.shape
    return x.reshape(b, l // tile, tile, heads, _HD).transpose(0, 3, 1, 4, 2)


def _sel_score_matrix(nsel_pad, nc_pad, nc):
    ratio = NSA_SEL_BLOCK // NSA_CMP_STRIDE
    span = NSA_CMP_LEN // NSA_CMP_STRIDE
    mt = np.zeros((nsel_pad, nc_pad), np.float32)
    for j in range(nsel_pad):
        for o in range(-(span - 1), ratio):
            c = j * ratio + o
            wt = sum(1 for m in range(ratio) for q in range(span) if m - q == o)
            if 0 <= c < nc:
                mt[j, c] = wt
    return jnp.asarray(mt)


def _mixer_c_attention(qu, qr, mq, ng, rows3, win3, kcvc, batch, seq):
    tk = KEY_TILE
    tq = LANES
    assert seq % tk == 0 and seq >= NSA_WINDOW + tq
    nt = seq // tq
    nkt_max = seq // tk
    nc = (seq - NSA_CMP_LEN) // NSA_CMP_STRIDE + 1
    nseg = kcvc.shape[1]
    n_sel = seq // NSA_SEL_BLOCK
    nsel_pad = -(-n_sel // 8) * 8
    nbf = seq // MOBA_BLOCK
    nbf_pad = -(-nbf // 8) * 8
    wl = NSA_WINDOW + tq
    mt = _sel_score_matrix(nsel_pad, nseg, nc)
    kw_n = NSA_KV * _HD

    kb = _block_mean(rows3, nbf, col_block=2)[:, :, 0, :]
    kb = _head_major(jnp.pad(kb, ((0, 0), (0, nbf_pad - nbf), (0, 0))), MOBA_KV)

    n_n = NSA_GRP * tq
    n_m = MOBA_GRP * tq
    tile = lambda wd: pl.BlockSpec((tq, wd), lambda b, i: (b * nt + i, 0))
    per_b = lambda shp, cb=0: pl.BlockSpec((1,) + shp, lambda b, i: (b,) + (0,) * (len(shp) - 1) + (cb,))
    o_n = pl.pallas_call(
        functools.partial(_nsa_kernel, tq=tq, nc=nc, n_sel=n_sel, nkt_max=nkt_max, wl=wl),
        grid=(batch, nt),
        in_specs=[tile(NSA_HEADS * _HD), tile(NSA_HEADS * _HD), tile(LANES), per_b((nseg, 2 * kw_n)), _full(mt.shape),
                  per_b((seq, 2 * kw_n), 1), per_b((seq, 2 * kw_n))],
        out_specs=tile(NSA_HEADS * _HD),
        out_shape=jax.ShapeDtypeStruct((batch * seq, NSA_HEADS * _HD), F32),
        scratch_shapes=[pltpu.VMEM((NSA_KV, nsel_pad, tq), F32), pltpu.VMEM((1, NSA_KV * n_n), F32),
                        pltpu.VMEM((1, NSA_KV * n_n), F32), pltpu.VMEM((kw_n, NSA_KV * n_n), F32)],
        compiler_params=_cp("parallel", "arbitrary"),
        name="nsa",
    )(qu, qr, ng, kcvc, mt, rows3, win3)

    o_m = pl.pallas_call(
        functools.partial(_moba_kernel, tq=tq, nbf=nbf, nkt_max=nkt_max),
        grid=(batch, nt),
        in_specs=[tile(MOBA_HEADS * _HD), per_b(kb.shape[1:]), per_b((seq, 2 * _C_MK), 1)],
        out_specs=tile(MOBA_HEADS * _HD),
        out_shape=jax.ShapeDtypeStruct((batch * seq, MOBA_HEADS * _HD), F32),
        scratch_shapes=[pltpu.VMEM((MOBA_KV, nbf_pad, n_m), F32), pltpu.VMEM((1, MOBA_KV * n_m), F32),
                        pltpu.VMEM((1, MOBA_KV * n_m), F32), pltpu.VMEM((_C_MK, MOBA_KV * n_m), F32)],
        compiler_params=_cp("parallel", "arbitrary"),
        name="moba",
    )(mq, kb, rows3)
    return o_n, o_m


PAGES_PER_STEP = 4
_SEG_PER_PAGE = PAGE_SIZE // NSA_CMP_STRIDE


def _repack_kernel(pt_ref, *refs):
    pps = PAGES_PER_STEP
    pg_refs = refs[:pps]
    flat_ref, kbar_ref, sel_ref, mob_ref = refs[pps:]
    nh = KV_ROW_HEADS
    wd = 2 * _C_KV

    def pair(ref, h):
        return jnp.concatenate([ref[0, pl.ds(h, PAGE_SIZE, stride=nh), :],
                                ref[0, pl.ds(h + 1, PAGE_SIZE, stride=nh), :]], axis=1)

    ksum = []
    for pg, ref in enumerate(pg_refs):
        r0 = pg * PAGE_SIZE
        sel_ref[0, r0:r0 + PAGE_SIZE, 0:_C_KV] = pair(ref, 2 * NSA_KV).astype(BF16)
        sel_ref[0, r0:r0 + PAGE_SIZE, _C_KV:wd] = pair(ref, 3 * NSA_KV).astype(BF16)
        sums = []
        for u in range(MOBA_KV):
            x = pair(ref, 4 * NSA_KV + 2 * u)
            mob_ref[0, r0:r0 + PAGE_SIZE, u * LANES:(u + 1) * LANES] = x.astype(BF16)
            if u < MOBA_KV // 2:
                sums.append(jnp.sum(x, axis=0, keepdims=True))
        ksum.append(jnp.concatenate(sums, axis=1))
    for pr in range(pps // 2):
        kbar_ref[0, pr] = (ksum[2 * pr] + ksum[2 * pr + 1]) * (1.0 / MOBA_BLOCK)
        for i in range(NSA_CMP_STRIDE):
            for u in range(0, 2 * NSA_KV, 2):
                def seg(ref, h):
                    return ref[0, pl.ds(i * nh + h, _SEG_PER_PAGE, stride=NSA_CMP_STRIDE * nh), :]
                x = jnp.concatenate([jnp.concatenate([seg(pg_refs[2 * pr + q], u), seg(pg_refs[2 * pr + q], u + 1)], axis=1)
                                     for q in range(2)], axis=0)
                c0 = i * wd + u * _HD
                flat_ref[0, 2 * pr * _SEG_PER_PAGE:(2 * pr + 2) * _SEG_PER_PAGE, c0:c0 + LANES] = x.astype(BF16)


def _repack_pages(pool, pt, layer):
    batch, npages = pt.shape
    n_layers = pool.shape[1]
    assert npages % PAGES_PER_STEP == 0 and MOBA_BLOCK == 2 * PAGE_SIZE
    pool3 = pool.reshape(pool.shape[0] * n_layers, PAGE_SIZE * KV_ROW_HEADS, _HD)
    ng = npages // PAGES_PER_STEP
    past = npages * PAGE_SIZE
    nseg = npages * _SEG_PER_PAGE
    nbf = npages // 2
    wd = 2 * _C_KV
    rows = PAGES_PER_STEP * PAGE_SIZE

    def page_spec(pg):
        return pl.BlockSpec((1, PAGE_SIZE * KV_ROW_HEADS, _HD),
                            lambda b, j, p: (p[b * npages + j * PAGES_PER_STEP + pg] * n_layers + layer, 0, 0))

    gs = pltpu.PrefetchScalarGridSpec(
        num_scalar_prefetch=1,
        grid=(batch, ng),
        in_specs=[page_spec(pg) for pg in range(PAGES_PER_STEP)],
        out_specs=[pl.BlockSpec((1, PAGES_PER_STEP * _SEG_PER_PAGE, NSA_CMP_STRIDE * wd), lambda b, j, p: (b, j, 0)),
                   pl.BlockSpec((1, PAGES_PER_STEP // 2, 1, _C_MK), lambda b, j, p: (b, j, 0, 0)),
                   pl.BlockSpec((1, rows, wd), lambda b, j, p: (b, j, 0)),
                   pl.BlockSpec((1, rows, 2 * _C_MK), lambda b, j, p: (b, j, 0))],
    )
    flat, kbar, selkv, mobkv = pl.pallas_call(
        _repack_kernel,
        grid_spec=gs,
        out_shape=[jax.ShapeDtypeStruct((batch, nseg, NSA_CMP_STRIDE * wd), BF16),
                   jax.ShapeDtypeStruct((batch, nbf, 1, _C_MK), F32),
                   jax.ShapeDtypeStruct((batch, past, wd), BF16),
                   jax.ShapeDtypeStruct((batch, past, 2 * _C_MK), BF16)],
        compiler_params=_cp("parallel", "arbitrary"),
        name="repack_pages",
    )(pt.reshape(-1), *([pool3] * PAGES_PER_STEP))
    return flat, kbar[:, :, 0, :], selkv, mobkv


def _flash_update(m_ref, l_ref, acc_ref, k, v, q, mask):
    vb = v.astype(BF16)
    _online_softmax(m_ref, l_ref, acc_ref, _dot(k.astype(BF16), q), mask, lambda p: _tdot(vb, p))


def _dec_attn_kernel(sel_ref, mob_ref, qn_ref, qm_ref, smask_ref, mmask_ref, nsel_ref, nmob_ref, opart_ref, g1_ref,
                     on_ref, om_ref, mn, ln, accn, mm, lm, accm, *, past, nsteps, seq_pad):
    j = pl.program_id(1)
    kw_n = NSA_KV * _HD
    kw_m = _C_MK

    @pl.when(j == 0)
    def _():
        mn[...] = jnp.full(mn.shape, NEG, F32)
        mm[...] = jnp.full(mm.shape, NEG, F32)
        ln[...] = jnp.zeros(ln.shape, F32)
        lm[...] = jnp.zeros(lm.shape, F32)
        accn[...] = jnp.zeros(accn.shape, F32)
        accm[...] = jnp.zeros(accm.shape, F32)

    qn = qn_ref[0]
    qm = qm_ref[0]
    rows = sel_ref.shape[1]
    ksv = sel_ref[0]
    nb = rows // NSA_SEL_BLOCK
    srow = smask_ref[0, pl.ds(pl.multiple_of(j * nb, nb), nb), :]
    smask = jnp.concatenate([jnp.broadcast_to(srow[u:u + 1], (NSA_SEL_BLOCK, LANES)) for u in range(nb)], axis=0)
    _flash_update(mn, ln, accn, ksv[:, :kw_n], ksv[:, kw_n:], qn, smask > 0.5)

    kmv = mob_ref[0]
    nbm = rows // MOBA_BLOCK
    mmask = jnp.concatenate([jnp.broadcast_to(mmask_ref[0, pl.ds(j * nbm + u, 1), :], (MOBA_BLOCK, LANES))
                             for u in range(nbm)], axis=0)
    _flash_update(mm, lm, accm, kmv[:, :kw_m], kmv[:, kw_m:], qm, mmask > 0.5)

    @pl.when(j == nsteps - 1)
    def _():
        lane = lax.broadcasted_iota(jnp.int32, (1, LANES), 1)
        tpos = past + (lane & (seq_pad - 1))
        new_rows = nsel_ref.shape[1]
        kpos = past + lax.broadcasted_iota(jnp.int32, (new_rows, 1), 0)
        causal = kpos <= tpos
        new_blk = past // NSA_SEL_BLOCK
        nrow = jnp.broadcast_to(smask_ref[0, new_blk:new_blk + 1, :], (new_rows, LANES)) > 0.5
        nsel = nsel_ref[0]
        _flash_update(mn, ln, accn, nsel[:, :kw_n], nsel[:, kw_n:], qn, causal & nrow)
        nmob = nmob_ref[0]
        _flash_update(mm, lm, accm, nmob[:, :kw_m], nmob[:, kw_m:], qm, causal)
        on_ref[0] = opart_ref[0] + g1_ref[0] * (accn[...] / jnp.maximum(ln[...], 1e-30))
        om_ref[0] = accm[...] / jnp.maximum(lm[...], 1e-30)


def _block_diag_q(x, batch, seq, groups, rep):
    x = x.reshape(batch, seq, groups, rep, _HD).transpose(0, 2, 4, 3, 1).reshape(batch, groups, _HD, rep * seq)
    bd = jnp.einsum('bgdn,gh->bgdhn', x, jnp.eye(groups, dtype=x.dtype))
    bd = bd.reshape(batch, groups * _HD, groups * rep * seq)
    return jnp.pad(bd, ((0, 0), (0, 0), (0, LANES - bd.shape[2])))


def _diag_blocks(o, batch, seq, groups, rep):
    o = o[:, :, :groups * rep * seq].reshape(batch, groups, _HD, groups, rep, seq)
    o = jnp.stack([o[:, g, :, g] for g in range(groups)], axis=1)
    return o.transpose(0, 4, 1, 3, 2).reshape(batch * seq, groups * rep * _HD)


def _decode_attention(qu, qr, mq, ng, rows3, win3, pool, pt, cache_win, layer, wbig, pex, w2big, batch, seq):
    npages = pt.shape[1]
    past = npages * PAGE_SIZE
    lt = past + seq
    assert seq <= 8 and seq < NSA_CMP_STRIDE and past % MOBA_BLOCK == 0
    seq_pad = 8
    tq = 64
    scale = _HD ** -0.5

    cmpflat, kbar, selkv, mobkv = _repack_pages(pool, pt, layer)
    kcvc = _compress(cmpflat, wbig, pex, w2big)
    nseg = kcvc.shape[1]
    nc = (lt - NSA_CMP_LEN) // NSA_CMP_STRIDE + 1
    assert nseg == nc + 1
    n_sel = -(-lt // NSA_SEL_BLOCK)
    nsel_pad = -(-n_sel // 8) * 8
    nbf = lt // MOBA_BLOCK
    nbf_pad = -(-(nbf + 1) // 8) * 8
    ng_steps = npages // PAGES_PER_STEP

    def padq(x):
        x = x.reshape(batch, seq, -1)
        return jnp.pad(x, ((0, 0), (0, tq - seq), (0, 0))).reshape(batch * tq, -1)

    qu_t = _to_qt(padq(qu) * scale, batch, 1, tq, NSA_KV, NSA_GRP)
    qr_t = _to_qt(padq(qr) * scale, batch, 1, tq, NSA_KV, NSA_GRP)
    mq_t = _to_qt(padq(mq), batch, 1, tq, MOBA_KV, MOBA_GRP)
    gl = padq(ng)[:, :NSA_HEADS * 3].reshape(batch, 1, tq, NSA_KV, NSA_GRP, 3).transpose(0, 1, 3, 5, 4, 2)
    gl = gl.reshape(batch, 1, NSA_KV, 3, NSA_GRP * tq)
    kc = _head_major(kcvc[:, :, 0:_C_KV], NSA_KV)
    vc = _head_major(kcvc[:, :, _C_KV:2 * _C_KV], NSA_KV)
    mt = _sel_score_matrix(nsel_pad, nseg, nc)
    wb = cache_win.reshape(batch, -1, 2 * _C_KV)
    wcat = jnp.concatenate([wb, win3], axis=1)
    wl = -(-wcat.shape[1] // LANES) * LANES
    wfull = jnp.pad(wcat, ((0, 0), (0, wl - wcat.shape[1]), (0, 0)))
    kw = _head_major(wfull[:, :, 0:_C_KV], NSA_KV).astype(BF16)
    vw = _head_major(wfull[:, :, _C_KV:], NSA_KV).astype(BF16)
    kb = _head_major(jnp.pad(kbar, ((0, 0), (0, nbf_pad - nbf), (0, 0))), MOBA_KV)

    n_n = NSA_GRP * tq
    n_m = MOBA_GRP * tq
    b1 = lambda shp: pl.BlockSpec((1,) + shp, lambda b: (b,) + (0,) * len(shp))
    o_part, g1, sel = pl.pallas_call(
        functools.partial(_nsa_select_kernel, tq=tq, nc=nc, n_sel=n_sel, qpos0=past, wpos0=past - wb.shape[1]),
        grid=(batch,),
        in_specs=[b1((1, NSA_KV, _HD, n_n)), b1((1, NSA_KV, _HD, n_n)), b1((1, NSA_KV, 3, n_n)),
                  b1(kc.shape[1:]), b1(vc.shape[1:]), _full(mt.shape), b1(kw.shape[1:]), b1(vw.shape[1:])],
        out_specs=[b1((NSA_KV, _HD, n_n)), b1((NSA_KV, 1, n_n)), b1((NSA_KV, nsel_pad, tq))],
        out_shape=[jax.ShapeDtypeStruct((batch, NSA_KV, _HD, n_n), F32),
                   jax.ShapeDtypeStruct((batch, NSA_KV, 1, n_n), F32),
                   jax.ShapeDtypeStruct((batch, NSA_KV, nsel_pad, tq), F32)],
        compiler_params=_cp("parallel"),
        name="nsa_select",
    )(qu_t, qr_t, gl, kc, vc, mt, kw, vw)
    msel = pl.pallas_call(
        functools.partial(_moba_select_kernel, own=past // MOBA_BLOCK, nbf=nbf),
        grid=(batch,),
        in_specs=[b1((1, MOBA_KV, _HD, n_m)), b1(kb.shape[1:])],
        out_specs=b1((MOBA_KV, nbf_pad, n_m)),
        out_shape=jax.ShapeDtypeStruct((batch, MOBA_KV, nbf_pad, n_m), F32),
        compiler_params=_cp("parallel"),
        name="moba_select",
    )(mq_t, kb)

    def lanes(x):
        return jnp.pad(x, [(0, 0)] * (x.ndim - 1) + [(0, LANES - x.shape[-1])])

    pad_s = lambda x: jnp.pad(x.reshape(batch, seq, -1), ((0, 0), (0, seq_pad - seq), (0, 0)))
    qn_bd = _block_diag_q(pad_s(qr).reshape(batch * seq_pad, -1) * scale, batch, seq_pad, NSA_KV, NSA_GRP).astype(BF16)
    qm_bd = _block_diag_q(pad_s(mq).reshape(batch * seq_pad, -1) * scale, batch, seq_pad, MOBA_KV, MOBA_GRP).astype(BF16)
    smask = jnp.broadcast_to(sel[:, :, :, None, :seq_pad], (batch, NSA_KV, nsel_pad, NSA_GRP, seq_pad))
    smask = lanes(smask.transpose(0, 2, 1, 3, 4).reshape(batch, nsel_pad, -1))
    mmask = msel.reshape(batch, MOBA_KV, nbf_pad, MOBA_GRP, tq)[..., :seq_pad]
    mmask = lanes(mmask.transpose(0, 2, 1, 3, 4).reshape(batch, nbf_pad, -1))
    op = o_part.reshape(batch, NSA_KV, _HD, NSA_GRP, tq)[..., :seq_pad]
    op_bd = jnp.einsum('bgdrt,gh->bgdhrt', op, jnp.eye(NSA_KV, dtype=F32))
    op_bd = lanes(op_bd.reshape(batch, NSA_KV * _HD, -1))
    g1l = g1.reshape(batch, NSA_KV, NSA_GRP, tq)[..., :seq_pad].reshape(batch, 1, -1)
    g1l = lanes(g1l)
    new_pad = 16
    pad_n = lambda x: jnp.pad(x, ((0, 0), (0, new_pad - seq), (0, 0)))
    new_sel = pad_n(rows3[:, :, 2 * _C_KV:4 * _C_KV])
    new_mob = pad_n(rows3[:, :, 4 * _C_KV:])

    kw_n = NSA_KV * _HD
    rows = PAGES_PER_STEP * PAGE_SIZE
    bj = lambda shp: pl.BlockSpec((1,) + shp, lambda b, j: (b,) + (0,) * len(shp))
    o_n, o_m = pl.pallas_call(
        functools.partial(_dec_attn_kernel, past=past, nsteps=ng_steps, seq_pad=seq_pad),
        grid=(batch, ng_steps),
        in_specs=[pl.BlockSpec((1, rows, 2 * kw_n), lambda b, j: (b, j, 0)),
                  pl.BlockSpec((1, rows, 2 * _C_MK), lambda b, j: (b, j, 0)),
                  bj((kw_n, LANES)), bj((_C_MK, LANES)), bj((nsel_pad, LANES)), bj((nbf_pad, LANES)),
                  bj((new_pad, 2 * kw_n)), bj((new_pad, 2 * _C_MK)), bj((kw_n, LANES)), bj((1, LANES))],
        out_specs=[bj((kw_n, LANES)), bj((_C_MK, LANES))],
        out_shape=[jax.ShapeDtypeStruct((batch, kw_n, LANES), F32), jax.ShapeDtypeStruct((batch, _C_MK, LANES), F32)],
        scratch_shapes=[pltpu.VMEM((1, LANES), F32), pltpu.VMEM((1, LANES), F32), pltpu.VMEM((kw_n, LANES), F32),
                        pltpu.VMEM((1, LANES), F32), pltpu.VMEM((1, LANES), F32), pltpu.VMEM((_C_MK, LANES), F32)],
        compiler_params=_cp("parallel", "arbitrary"),
        name="decode_attn",
    )(selkv, mobkv, qn_bd, qm_bd, smask, mmask, new_sel, new_mob, op_bd, g1l)
    o_n = _diag_blocks(o_n, batch, seq_pad, NSA_KV, NSA_GRP).reshape(batch, seq_pad, -1)[:, :seq]
    o_m = _diag_blocks(o_m, batch, seq_pad, MOBA_KV, MOBA_GRP).reshape(batch, seq_pad, -1)[:, :seq]
    return o_n.reshape(batch * seq, -1), o_m.reshape(batch * seq, -1)


def _prep_weights(prm):
    w = {}
    w['w_in_a'] = prm['w_in_a'].astype(BF16)
    w['w_out_a'] = prm['w_out_a'].astype(BF16)
    w['w_out_c'] = prm['w_out_c'].astype(BF16)
    w['w_in_c'] = [_proj_c_weights(prm['w_in_c'][c]) for c in range(prm['w_in_c'].shape[0])]
    w['cmp'] = [_compress_weights(prm['cmp_w1_k'][c], prm['cmp_w1_v'][c], prm['cmp_w2_k'][c], prm['cmp_w2_v'][c],
                                  prm['cmp_pe_k'][c], prm['cmp_pe_v'][c]) for c in range(prm['w_in_c'].shape[0])]
    depth, d, _ = prm['router_c_w'].shape
    wr = jnp.concatenate([prm['router_c_w'], prm['router_f_w'].transpose(0, 2, 1, 3).reshape(depth, d, -1)], axis=2)
    w['router_w'] = jnp.pad(wr, ((0, 0), (0, 0), (0, LANES - wr.shape[2])))
    br = jnp.concatenate([prm['router_c_b'], prm['router_f_b'].reshape(depth, -1)], axis=1)
    w['router_b'] = jnp.pad(br, ((0, 0), (0, LANES - br.shape[1])))[:, None, :]
    w['moe_w1'] = prm['moe_w1']
    w['moe_w3'] = prm['moe_w3']
    w['moe_w2'] = prm['moe_w2']
    w['ple_w'] = prm['ple_w'].astype(BF16)
    w['ple_gate_w'] = prm['ple_gate_w'].astype(BF16)
    w['lb'] = jnp.cumsum(jax.nn.softmax(prm['hgrn_lb'].astype(F32), axis=0), axis=0)
    return w


def _forward(x, p, pos, prm, w, ctx):
    batch, seq, d = x.shape
    t = batch * seq
    depth = p.shape[0]
    h = x.reshape(t, d)
    outs = {}
    for i in range(depth):
        gmix = prm['norm_mix'][i][None, :]
        if i % 2 == 0:
            a = i // 2
            lb512 = jnp.tile(w['lb'][a], HG_HEADS)[None, :]
            q, k, lf, iv, sg, gu, vn = _proj_a(h, gmix, w['w_in_a'][a], lb512, prm['cm_vnorm'][a][None, :])
            if ctx is None:
                st0 = jnp.zeros((batch, HG_WIDTH, HG_WIDTH), F32)
            else:
                s0 = ctx['state_hgrn'][a].astype(F32)
                eye = jnp.eye(HG_HEADS, dtype=F32)
                st0 = jnp.einsum('bhde,hg->bhegd', s0, eye).reshape(batch, HG_WIDTH, HG_WIDTH)
            o, st = _hgrn(q, k, lf, iv, sg, st0, prm['hgrn_onorm'][a][None, :], batch, seq)
            st5 = st.reshape(batch, HG_HEADS, HG_DK, HG_HEADS, HG_DK)
            s_new = jnp.stack([st5[:, hh, :, hh, :] for hh in range(HG_HEADS)], axis=1).transpose(0, 1, 3, 2)
            cm = _cmix(vn, gu, prm['cm_ws'][a], prm['cm_bs'][a].T, batch, seq)
            h = _out_proj(h, o, cm, w['w_out_a'][a])
            outs.setdefault('hg', []).append(s_new)
            outs.setdefault('cm', []).append(vn.reshape(batch, seq, CM_WIDTH))
        else:
            c = i // 2
            qu, qr, mq, ng, rows, win, *flat = _proj_c(h, gmix, w['w_in_c'][c], pos, seq, want_flat=ctx is None)
            rows3 = rows.reshape(batch, seq, ROW_W)
            win3 = win.reshape(batch, seq, 2 * _C_KV)
            wbig, pex, w2big = w['cmp'][c]
            if ctx is None:
                assert seq % KEY_TILE == 0
                nseg = seq // NSA_CMP_STRIDE
                kcvc = _compress(flat[0].reshape(batch, nseg, -1), wbig, pex, w2big)
                o_n, o_m = _mixer_c_attention(qu, qr, mq, ng, rows3, win3, kcvc, batch, seq)
                new_win = win3[:, seq - min(NSA_WINDOW, seq):]
            else:
                o_n, o_m = _decode_attention(qu, qr, mq, ng, rows3, win3, ctx['cache_kv'], ctx['page_table'],
                                             ctx['cache_win'][c], c, wbig, pex, w2big, batch, seq)
                new_win = win3
            h = _out_proj(h, o_n, o_m, w['w_out_c'][c])
            outs.setdefault('kv', []).append(rows3.reshape(batch, seq, KV_ROW_HEADS, _HD))
            outs.setdefault('win', []).append(new_win.reshape(batch, -1, 2 * NSA_KV, _HD))
        y = _moe(h, prm['norm_ffn'][i][None, :], w['router_w'][i], w['router_b'][i],
                 w['moe_w1'][i], w['moe_w3'][i], w['moe_w2'][i])
        h = _ple(h, y, p[i].reshape(t, -1), prm['norm_ple'][i][None, :], w['ple_gate_w'][i], w['ple_w'][i],
                 prm['final_norm'][None, :], final=(i == depth - 1))
    return h.reshape(batch, seq, d), outs


def kernel(x_prompt, x_sample, cache_kv, cache_win, state_hgrn, page_table, p_prompt, p_sample,
           norm_mix, norm_ffn, norm_ple, final_norm, w_in_a, w_out_a, hgrn_lb, hgrn_onorm, cm_vnorm,
           cm_ws, cm_bs, w_in_c, w_out_c, cmp_pe_k, cmp_w1_k, cmp_w2_k, cmp_pe_v, cmp_w1_v, cmp_w2_v,
           router_c_w, router_c_b, router_f_w, router_f_b, moe_w1, moe_w3, moe_w2, ple_w, ple_gate_w):
    prm = dict(norm_mix=norm_mix, norm_ffn=norm_ffn, norm_ple=norm_ple, final_norm=final_norm,
               w_in_a=w_in_a, w_out_a=w_out_a, hgrn_lb=hgrn_lb, hgrn_onorm=hgrn_onorm, cm_vnorm=cm_vnorm,
               cm_ws=cm_ws, cm_bs=cm_bs, w_in_c=w_in_c, w_out_c=w_out_c, cmp_pe_k=cmp_pe_k,
               cmp_w1_k=cmp_w1_k, cmp_w2_k=cmp_w2_k, cmp_pe_v=cmp_pe_v, cmp_w1_v=cmp_w1_v, cmp_w2_v=cmp_w2_v,
               router_c_w=router_c_w, router_c_b=router_c_b, router_f_w=router_f_w, router_f_b=router_f_b,
               moe_w1=moe_w1, moe_w3=moe_w3, moe_w2=moe_w2, ple_w=ple_w, ple_gate_w=ple_gate_w)
    w = _prep_weights(prm)
    past = page_table.shape[1] * PAGE_SIZE
    y_p, o_p = _forward(x_prompt, p_prompt, jnp.arange(x_prompt.shape[1]), prm, w, None)
    ctx = dict(cache_kv=cache_kv, cache_win=cache_win, state_hgrn=state_hgrn, page_table=page_table)
    y_s, o_s = _forward(x_sample, p_sample, past + jnp.arange(x_sample.shape[1]), prm, w, ctx)
    return (y_p, y_s,
            jnp.stack(o_p['kv'], axis=1), jnp.stack(o_s['kv'], axis=1),
            jnp.stack(o_p['win'], axis=0), jnp.stack(o_s['win'], axis=0),
            jnp.stack(o_p['hg'], axis=0), jnp.stack(o_s['hg'], axis=0),
            jnp.stack(o_s['cm'], axis=0))
```

```python
import functools
import math

import numpy as np
import jax
import jax.numpy as jnp
from jax import lax
from jax.experimental import pallas as pl
from jax.experimental.pallas import tpu as pltpu

F32 = jnp.float32
BF16 = jnp.bfloat16
HI = lax.Precision.HIGHEST

PAGE_SIZE = 128
HEAD_DIM = 64
ROT_DIM = HEAD_DIM // 4
ROPE_THETA = 500000.0
HG_HEADS = 8
HG_DK = 64
HG_WIDTH = HG_HEADS * HG_DK
HG_CHUNK = 64
CM_GROUPS = 4
CM_GW = 128
CM_WIDTH = CM_GROUPS * CM_GW
CM_CHUNK = 128
NSA_HEADS = 8
NSA_KV = 2
NSA_GRP = NSA_HEADS // NSA_KV
NSA_CMP_LEN = 32
NSA_CMP_STRIDE = 16
NSA_SEL_BLOCK = 64
NSA_TOPN = 16
NSA_WINDOW = 512
MOBA_HEADS = 8
MOBA_KV = 4
MOBA_GRP = MOBA_HEADS // MOBA_KV
MOBA_BLOCK = 256
MOBA_TOPK = 3
MOE_GROUPS = 4
MOE_EPG = 8
MOE_EXPERTS = MOE_GROUPS * MOE_EPG
KV_ROW_HEADS = 16
ROW_W = KV_ROW_HEADS * HEAD_DIM
A_SPLITS = [HG_WIDTH] * 4 + [CM_WIDTH] * 2
C_SPLITS = [NSA_HEADS * HEAD_DIM] + [NSA_KV * HEAD_DIM] * 6 + [NSA_HEADS * 3, MOBA_HEADS * HEAD_DIM,
                                                              MOBA_KV * HEAD_DIM, MOBA_KV * HEAD_DIM]
NEG = -1e30
BIG = 1e9
EPS = 1e-6
KEY_TILE = 256
LANES = 128
_MOBA_SHIFT = MOBA_BLOCK.bit_length() - 1
_SEL_SHIFT = NSA_SEL_BLOCK.bit_length() - 1
VMEM_LIMIT = 56 * 1024 * 1024


def _cp(*sem):
    return pltpu.CompilerParams(dimension_semantics=sem, vmem_limit_bytes=VMEM_LIMIT)


def _sigmoid(x):
    return 1.0 / (1.0 + jnp.exp(-x))


def _silu(x):
    return x * _sigmoid(x)


def _gelu(x):
    return 0.5 * x * (1.0 + jnp.tanh(math.sqrt(2.0 / math.pi) * (x + 0.044715 * (x * x * x))))


def _rms(x, g):
    return x * lax.rsqrt(jnp.mean(x * x, axis=-1, keepdims=True) + EPS) * g


def _dot(a, b):
    return jnp.dot(a, b, preferred_element_type=F32)


def _dot_hi(a, b):
    return jnp.dot(a, b, precision=HI, preferred_element_type=F32)


def _dot3(a, b):
    a_hi = a.astype(BF16)
    b_hi = b.astype(BF16)
    a_lo = (a - a_hi.astype(F32)).astype(BF16)
    b_lo = (b - b_hi.astype(F32)).astype(BF16)
    return _dot(a_hi, b_hi) + (_dot(a_hi, b_lo) + _dot(a_lo, b_hi))


def _full(shape):
    n = len(shape)
    return pl.BlockSpec(shape, lambda *_: (0,) * n)


def _row_tile(t):
    for tm in (256, 128, 64, 32, 16, 8):
        if t % tm == 0:
            return tm
    raise ValueError(f"token count {t} is not a multiple of 8")


def _proj_a_kernel(x_ref, g_ref, w_ref, lb_ref, vg_ref, q_ref, k_ref, lf_ref, iv_ref, sg_ref, gu_ref, vn_ref):
    xn = _rms(x_ref[...], g_ref[...]).astype(BF16)
    z = _dot(xn, w_ref[...])
    w = HG_WIDTH
    q_ref[...] = _silu(z[:, 0:w])
    lb = lb_ref[...]
    f = lb + (1.0 - lb) * _sigmoid(z[:, w:2 * w])
    k_ref[...] = 1.0 - f
    lf_ref[...] = jnp.log(f)
    iv_ref[...] = z[:, 2 * w:3 * w]
    sg_ref[...] = _silu(z[:, 3 * w:4 * w])
    gu_ref[...] = _gelu(z[:, 4 * w:4 * w + CM_WIDTH])
    v = _gelu(z[:, 4 * w + CM_WIDTH:])
    for gi in range(CM_GROUPS):
        sl = slice(gi * CM_GW, (gi + 1) * CM_GW)
        vn_ref[:, sl] = _rms(v[:, sl], vg_ref[:, sl])


def _proj_a(x, g, w_bf, lb512, vgain):
    t, d = x.shape
    tm = _row_tile(t)
    n = w_bf.shape[1]
    row = lambda wd: pl.BlockSpec((tm, wd), lambda i: (i, 0))
    outs = [jax.ShapeDtypeStruct((t, HG_WIDTH), F32)] * 5 + [jax.ShapeDtypeStruct((t, CM_WIDTH), F32)] * 2
    return pl.pallas_call(
        _proj_a_kernel,
        grid=(t // tm,),
        in_specs=[row(d), _full((1, d)), _full((d, n)), _full((1, HG_WIDTH)), _full((1, CM_WIDTH))],
        out_specs=[row(HG_WIDTH)] * 5 + [row(CM_WIDTH)] * 2,
        out_shape=outs,
        compiler_params=_cp("parallel"),
        name="proj_a",
    )(x, g, w_bf, lb512, vgain)


def _hgrn_kernel(q_ref, k_ref, lf_ref, v_ref, sg_ref, s0_ref, gain_ref, bones_ref, bmask_ref, tri_ref,
                 o_ref, sout_ref, st_scr, p_scr, *, chunk):
    c = pl.program_id(1)
    w = HG_WIDTH
    hw = w // 2

    @pl.when(c == 0)
    def _():
        st_scr[...] = s0_ref[0]

    q = q_ref[0]
    k = k_ref[0]
    v = v_ref[0]
    b = _dot_hi(tri_ref[...], lf_ref[0])
    bones = bones_ref[...]

    def head_sum(x):
        xb = x.astype(BF16)
        return jnp.concatenate([_dot(xb[:, :hw], bones), _dot(xb[:, hw:], bones)], axis=1)

    o_rows = []
    for blk in range(chunk // 8):
        s_len = 8 * (blk + 1)
        bs = b[:s_len]
        ks = k[:s_len]
        vs = v[:s_len]
        row_id = lax.broadcasted_iota(jnp.int32, (s_len, w), 0)
        for t in range(8):
            r = 8 * blk + t
            diff = jnp.where(row_id <= r, b[r:r + 1, :] - bs, NEG)
            p_scr[t * s_len:(t + 1) * s_len, :] = jnp.exp(diff) * q[r:r + 1, :] * ks
        att = head_sum(p_scr[0:8 * s_len, :])
        for t in range(8):
            o_rows.append(jnp.sum(att[t * s_len:(t + 1) * s_len] * vs, axis=0, keepdims=True))
    o_intra = jnp.concatenate(o_rows, axis=0)

    st = st_scr[...]
    qe = (q * jnp.exp(b)).astype(BF16)
    o = o_intra + lax.dot_general(qe, st.astype(BF16), (((1,), (1,)), ((), ())), preferred_element_type=F32)

    b_end = b[chunk - 1:chunk, :]
    kd = (k * jnp.exp(b_end - b)).astype(BF16)
    upd = lax.dot_general(v.astype(BF16), kd, (((0,), (0,)), ((), ())), preferred_element_type=F32)
    st_new = st * jnp.exp(b_end) + upd * bmask_ref[...]
    st_scr[...] = st_new

    o2 = o * o
    hi = o2.astype(BF16).astype(F32)
    ms = (head_sum(hi) + head_sum(o2 - hi)) * (1.0 / HG_DK)
    o_ref[0] = o * lax.rsqrt(ms + EPS) * gain_ref[...] * sg_ref[0]

    @pl.when(c == pl.num_programs(1) - 1)
    def _():
        sout_ref[0] = st_new


def _block_ones(n, blk, dtype):
    i = np.arange(n) // blk
    return jnp.asarray((i[:, None] == i[None, :]).astype(np.float32), dtype)


def _hgrn(q, k, lf, v, sg, st0, gain, batch, seq):
    w = HG_WIDTH
    chunk = math.gcd(seq, HG_CHUNK)
    nch = seq // chunk
    r3 = lambda a: a.reshape(batch, seq, w)
    tile = pl.BlockSpec((1, chunk, w), lambda b, c: (b, c, 0))
    state = pl.BlockSpec((1, w, w), lambda b, c: (b, 0, 0))
    tri = jnp.asarray(np.tril(np.ones((chunk, chunk), np.float32)))
    o, st = pl.pallas_call(
        functools.partial(_hgrn_kernel, chunk=chunk),
        grid=(batch, nch),
        in_specs=[tile] * 5 + [state, _full((1, w)), _full((w // 2, w // 2)), _full((w, w)), _full((chunk, chunk))],
        out_specs=[tile, state],
        out_shape=[jax.ShapeDtypeStruct((batch, seq, w), F32), jax.ShapeDtypeStruct((batch, w, w), F32)],
        scratch_shapes=[pltpu.VMEM((w, w), F32), pltpu.VMEM((8 * chunk, w), F32)],
        compiler_params=_cp("parallel", "arbitrary"),
        name="hgrn",
    )(r3(q), r3(k), r3(lf), r3(v), r3(sg), st0, gain, _block_ones(w // 2, HG_DK, BF16),
      _block_ones(w, HG_DK, F32), tri)
    return o.reshape(batch * seq, w), st


def _cmix_kernel(vn_ref, gu_ref, ws_ref, bst_ref, o_ref, *, tl):
    ri = lax.broadcasted_iota(jnp.int32, (tl, tl), 0)
    ci = lax.broadcasted_iota(jnp.int32, (tl, tl), 1)
    for gi in range(CM_GROUPS):
        sl = slice(gi * CM_GW, (gi + 1) * CM_GW)
        wg = jnp.where(ci <= ri, ws_ref[gi, 0:tl, 0:tl], 0.0).astype(BF16)
        mix = _dot(wg, vn_ref[0, :, sl].astype(BF16)) + bst_ref[0:tl, gi:gi + 1]
        o_ref[0, :, sl] = gu_ref[0, :, sl] * mix


def _cmix(vn, gu, ws, bst, batch, seq):
    tl = min(seq, CM_CHUNK)
    assert seq % tl == 0 and tl % 8 == 0
    w = CM_WIDTH
    tile = pl.BlockSpec((1, tl, w), lambda b, c: (b, c, 0))
    out = pl.pallas_call(
        functools.partial(_cmix_kernel, tl=tl),
        grid=(batch, seq // tl),
        in_specs=[tile, tile, _full(ws.shape), _full(bst.shape)],
        out_specs=tile,
        out_shape=jax.ShapeDtypeStruct((batch, seq, w), F32),
        compiler_params=_cp("parallel", "parallel"),
        name="cmix",
    )(vn.reshape(batch, seq, w), gu.reshape(batch, seq, w), ws, bst)
    return out.reshape(batch * seq, w)


def _out_proj_kernel(res_ref, a1_ref, a2_ref, w1_ref, w2_ref, o_ref):
    o_ref[...] = (res_ref[...] + _dot(a1_ref[...].astype(BF16), w1_ref[...])
                  + _dot(a2_ref[...].astype(BF16), w2_ref[...]))


def _out_proj(res, a1, a2, w_bf):
    t, d = res.shape
    k1, k2 = a1.shape[1], a2.shape[1]
    tm = _row_tile(t)
    row = lambda wd: pl.BlockSpec((tm, wd), lambda i: (i, 0))
    return pl.pallas_call(
        _out_proj_kernel,
        grid=(t // tm,),
        in_specs=[row(d), row(k1), row(k2), _full((k1, d)), _full((k2, d))],
        out_specs=row(d),
        out_shape=jax.ShapeDtypeStruct((t, d), F32),
        compiler_params=_cp("parallel"),
        name="out_proj",
    )(res, a1, a2, w_bf[:k1], w_bf[k1:])


def _router_kernel(h_ref, g_ref, w_ref, b_ref, xn_ref, info_ref):
    xn = _rms(h_ref[...], g_ref[...])
    xn_ref[...] = xn.astype(BF16)
    logit = _dot3(xn, w_ref[...]) + b_ref[...]
    lane = lax.broadcasted_iota(jnp.int32, logit.shape, 1)
    far = 1 << 20

    def first_max(vals, mask):
        m = jnp.max(jnp.where(mask, vals, -1.0), axis=-1, keepdims=True)
        idx = jnp.min(jnp.where(mask & (vals == m), lane, far), axis=-1, keepdims=True)
        return m, idx

    def softmax(mask):
        m = jnp.max(jnp.where(mask, logit, NEG), axis=-1, keepdims=True)
        e = jnp.where(mask, jnp.exp(logit - m), 0.0)
        return e / jnp.sum(e, axis=-1, keepdims=True)

    cmask = lane < MOE_GROUPS
    pg, grp = first_max(softmax(cmask), cmask)
    lo = MOE_GROUPS + grp * MOE_EPG
    fmask = (lane >= lo) & (lane < lo + MOE_EPG)
    pf = softmax(fmask)
    v1, i1 = first_max(pf, fmask)
    v2, i2 = first_max(pf, fmask & (lane != i1))
    den = v1 + v2
    info = jnp.where(lane == 0, (i1 - MOE_GROUPS).astype(F32), 0.0)
    info = jnp.where(lane == 1, (i2 - MOE_GROUPS).astype(F32), info)
    info = jnp.where(lane == 2, pg * v1 / den, info)
    info = jnp.where(lane == 3, pg * v2 / den, info)
    info_ref[...] = info


def _router(h, g, w_r, b_r):
    t, d = h.shape
    tm = _row_tile(t)
    row = lambda wd: pl.BlockSpec((tm, wd), lambda i: (i, 0))
    return pl.pallas_call(
        _router_kernel,
        grid=(t // tm,),
        in_specs=[row(d), _full((1, d)), _full((d, LANES)), _full((1, LANES))],
        out_specs=[row(d), row(LANES)],
        out_shape=[jax.ShapeDtypeStruct((t, d), BF16), jax.ShapeDtypeStruct((t, LANES), F32)],
        compiler_params=_cp("parallel"),
        name="router",
    )(h, g, w_r, b_r)


def _moe_ffn_kernel(be_ref, nu_ref, x_ref, w1_ref, w3_ref, w2_ref, o_ref, w1b, w3b, w2b):
    i = pl.program_id(0)

    @pl.when((i == 0) | (be_ref[i] != be_ref[jnp.maximum(i - 1, 0)]))
    def _():
        w1b[...] = w1_ref[0].astype(BF16)
        w3b[...] = w3_ref[0].astype(BF16)
        w2b[...] = w2_ref[0].astype(BF16)

    @pl.when(i < nu_ref[0])
    def _():
        x = x_ref[...]
        hdn = _silu(_dot(x, w1b[...])) * _dot(x, w3b[...])
        o_ref[...] = _dot(hdn.astype(BF16), w2b[...])

    @pl.when(i >= nu_ref[0])
    def _():
        o_ref[...] = jnp.zeros_like(o_ref)


def _moe_ffn(xp, blk_e, nused, w1, w3, w2, rb):
    n, d = xp.shape
    ff = w1.shape[2]
    nblk = n // rb
    gs = pltpu.PrefetchScalarGridSpec(
        num_scalar_prefetch=2,
        grid=(nblk,),
        in_specs=[pl.BlockSpec((rb, d), lambda i, be, nu: (i, 0)),
                  pl.BlockSpec((1, d, ff), lambda i, be, nu: (be[i], 0, 0)),
                  pl.BlockSpec((1, d, ff), lambda i, be, nu: (be[i], 0, 0)),
                  pl.BlockSpec((1, ff, d), lambda i, be, nu: (be[i], 0, 0))],
        out_specs=pl.BlockSpec((rb, d), lambda i, be, nu: (i, 0)),
        scratch_shapes=[pltpu.VMEM((d, ff), BF16), pltpu.VMEM((d, ff), BF16), pltpu.VMEM((ff, d), BF16)],
    )
    return pl.pallas_call(
        _moe_ffn_kernel,
        grid_spec=gs,
        out_shape=jax.ShapeDtypeStruct((n, d), F32),
        compiler_params=_cp("arbitrary"),
        name="moe_ffn",
    )(blk_e, nused, xp, w1, w3, w2)


def _moe(h, g, w_r, b_r, w1, w3, w2):
    t, d = h.shape
    xn, info = _router(h, g, w_r, b_r)
    eid = info[:, 0:2].astype(jnp.int32)
    gate = info[:, 2:4]
    ne = MOE_EXPERTS
    tk = 2 * t
    rb = 256 if t >= 4096 else 32
    flat = eid.reshape(-1)
    order = jnp.argsort(flat).astype(jnp.int32)
    rank = jnp.argsort(order).astype(jnp.int32)
    counts = jnp.sum((flat[:, None] == jnp.arange(ne)[None, :]).astype(jnp.int32), axis=0)
    padc = (counts + rb - 1) // rb * rb
    pend = jnp.cumsum(padc)
    pstart = pend - padc
    cstart = jnp.cumsum(counts) - counts
    slot = (pstart[flat] + rank - cstart[flat]).reshape(t, 2)
    nblk = -(-tk // rb) + ne
    blk_e = jnp.minimum(jnp.sum((pend[None, :] <= (jnp.arange(nblk) * rb)[:, None]).astype(jnp.int32), axis=1),
                        ne - 1).astype(jnp.int32)
    pe = jnp.repeat(blk_e, rb)
    off = jnp.arange(nblk * rb) - pstart[pe]
    rows = jnp.where(off < counts[pe], order[jnp.clip(off + cstart[pe], 0, tk - 1)] // 2, 0)
    xp = xn[rows]
    nused = (pend[-1:] // rb).astype(jnp.int32)
    yp = _moe_ffn(xp, blk_e, nused, w1, w3, w2, rb)
    return gate[:, 0:1] * yp[slot[:, 0]] + gate[:, 1:2] * yp[slot[:, 1]]


def _ple_kernel(h_ref, y_ref, p_ref, g_ref, wg_ref, wp_ref, fg_ref, o_ref, *, final):
    h = h_ref[...] + y_ref[...]
    gate = _sigmoid(_dot(_rms(h, g_ref[...]).astype(BF16), wg_ref[...]))
    out = h + gate * _dot(p_ref[...].astype(BF16), wp_ref[...])
    if final:
        out = _rms(out, fg_ref[...])
    o_ref[...] = out


def _ple(h, y, p, g, wg_bf, wp_bf, fg, final):
    t, d = h.shape
    pd = p.shape[1]
    tm = _row_tile(t)
    row = lambda wd: pl.BlockSpec((tm, wd), lambda i: (i, 0))
    return pl.pallas_call(
        functools.partial(_ple_kernel, final=final),
        grid=(t // tm,),
        in_specs=[row(d), row(d), row(pd), _full((1, d)), _full((d, d)), _full((pd, d)), _full((1, d))],
        out_specs=row(d),
        out_shape=jax.ShapeDtypeStruct((t, d), F32),
        compiler_params=_cp("parallel"),
        name="ple",
    )(h, y, p, g, wg_bf, wp_bf, fg)


_HD = HEAD_DIM
_C_NQ = NSA_HEADS * _HD
_C_KV = NSA_KV * _HD
_C_MQ = MOBA_HEADS * _HD
_C_MK = MOBA_KV * _HD
_O_NQ = 0
_O_CK = _O_NQ + _C_NQ
_O_CV = _O_CK + _C_KV
_O_SK = _O_CV + _C_KV
_O_SV = _O_SK + _C_KV
_O_WK = _O_SV + _C_KV
_O_WV = _O_WK + _C_KV
_O_MQ = _O_WV + _C_KV
_O_MK = _O_MQ + _C_MQ
_O_MV = _O_MK + _C_MK
_O_NG = _O_MV + _C_MK
_O_NQS = _O_NG + LANES
_O_SKS = _O_NQS + _C_NQ
_O_WKS = _O_SKS + _C_KV
_O_MQS = _O_WKS + _C_KV
_O_MKS = _O_MQS + _C_MQ
_C_TOTAL = _O_MKS + _C_MK


def _proj_c_kernel(x_ref, g_ref, w_ref, cos_ref, sin_ref, qu_ref, qr_ref, mq_ref, ng_ref, rows_ref, win_ref,
                   *flat_refs):
    xn = _rms(x_ref[...], g_ref[...]).astype(BF16)
    z = _dot(xn, w_ref[...])
    cs = cos_ref[...]
    sn = sin_ref[...]

    def rope(o, os, wd):
        reps = wd // LANES
        return z[:, o:o + wd] * jnp.tile(cs, (1, reps)) + z[:, os:os + wd] * jnp.tile(sn, (1, reps))

    qu_ref[...] = z[:, _O_NQ:_O_NQ + _C_NQ]
    qr_ref[...] = rope(_O_NQ, _O_NQS, _C_NQ)
    mq_ref[...] = rope(_O_MQ, _O_MQS, _C_MQ)
    ng_ref[...] = z[:, _O_NG:_O_NG + LANES]
    rows_ref[:, 0:2 * _C_KV] = z[:, _O_CK:_O_CK + 2 * _C_KV]
    rows_ref[:, 2 * _C_KV:3 * _C_KV] = rope(_O_SK, _O_SKS, _C_KV)
    rows_ref[:, 3 * _C_KV:4 * _C_KV] = z[:, _O_SV:_O_SV + _C_KV]
    rows_ref[:, 4 * _C_KV:4 * _C_KV + _C_MK] = rope(_O_MK, _O_MKS, _C_MK)
    rows_ref[:, 4 * _C_KV + _C_MK:] = z[:, _O_MV:_O_MV + _C_MK]
    win_ref[:, 0:_C_KV] = rope(_O_WK, _O_WKS, _C_KV)
    win_ref[:, _C_KV:] = z[:, _O_WV:_O_WV + _C_KV]
    if not flat_refs:
        return
    flat_ref, ck_scr, cv_scr = flat_refs
    ck_scr[...] = z[:, _O_CK:_O_CK + _C_KV]
    cv_scr[...] = z[:, _O_CV:_O_CV + _C_KV]
    nseg = flat_ref.shape[0]
    for i in range(NSA_CMP_STRIDE):
        c0 = i * 2 * _C_KV
        flat_ref[:, c0:c0 + _C_KV] = ck_scr[pl.ds(i, nseg, stride=NSA_CMP_STRIDE), :].astype(BF16)
        flat_ref[:, c0 + _C_KV:c0 + 2 * _C_KV] = cv_scr[pl.ds(i, nseg, stride=NSA_CMP_STRIDE), :].astype(BF16)


def _proj_c_weights(w):
    offs = np.concatenate([[0], np.cumsum(C_SPLITS)])
    nq, ck, cv, sk, sv, wk, wv, ng, mq, mk, mv = [w[:, offs[i]:offs[i + 1]] for i in range(len(C_SPLITS))]
    half = ROT_DIM // 2

    def swapped(m):
        d = m.shape[0]
        m3 = m.reshape(d, -1, _HD)
        out = jnp.concatenate([m3[..., half:ROT_DIM], m3[..., :half], jnp.zeros_like(m3[..., ROT_DIM:])], axis=-1)
        return out.reshape(d, -1)

    ngp = jnp.pad(ng, ((0, 0), (0, LANES - ng.shape[1])))
    return jnp.concatenate([nq, ck, cv, sk, sv, wk, wv, mq, mk, mv, ngp,
                            swapped(nq), swapped(sk), swapped(wk), swapped(mq), swapped(mk)], axis=1).astype(BF16)


def _rope_tables(pos):
    half = ROT_DIM // 2
    inv = ROPE_THETA ** (-jnp.arange(half, dtype=F32) / half)
    ang = pos.astype(F32)[:, None] * inv
    cos, sin = jnp.cos(ang), jnp.sin(ang)
    n = pos.shape[0]
    c64 = jnp.concatenate([cos, cos, jnp.ones((n, _HD - ROT_DIM), F32)], axis=1)
    s64 = jnp.concatenate([-sin, sin, jnp.zeros((n, _HD - ROT_DIM), F32)], axis=1)
    return jnp.tile(c64, (1, LANES // _HD)), jnp.tile(s64, (1, LANES // _HD))


def _proj_c(x, g, wc_bf, pos, seq, want_flat):
    t, d = x.shape
    tm = _row_tile(t)
    tr = max(seq, tm)
    assert tr % tm == 0 and tr % seq == 0
    cs, sn = _rope_tables(jnp.tile(pos, tr // seq))
    ntab = tr // tm
    row = lambda wd: pl.BlockSpec((tm, wd), lambda i: (i, 0))
    tab = pl.BlockSpec((tm, LANES), lambda i: (i % ntab, 0))
    widths = [_C_NQ, _C_NQ, _C_MQ, LANES, ROW_W, 2 * _C_KV]
    st = NSA_CMP_STRIDE
    fw = st * 2 * _C_KV
    out_specs = [row(wd) for wd in widths]
    out_shape = [jax.ShapeDtypeStruct((t, wd), F32) for wd in widths]
    scratch = []
    if want_flat:
        assert tm % (16 * st) == 0 and seq % st == 0
        out_specs.append(pl.BlockSpec((tm // st, fw), lambda i: (i, 0)))
        out_shape.append(jax.ShapeDtypeStruct((t // st, fw), BF16))
        scratch = [pltpu.VMEM((tm, _C_KV), F32), pltpu.VMEM((tm, _C_KV), F32)]
    return pl.pallas_call(
        _proj_c_kernel,
        grid=(t // tm,),
        in_specs=[row(d), _full((1, d)), _full((d, _C_TOTAL)), tab, tab],
        out_specs=out_specs,
        out_shape=out_shape,
        scratch_shapes=scratch,
        compiler_params=_cp("parallel"),
        name="proj_c",
    )(x, g, wc_bf, cs, sn)


def _mm_kernel(a_ref, w_ref, o_ref):
    o_ref[...] = _dot(a_ref[...], w_ref[...])


def _mm(a_bf, w_bf):
    m, k = a_bf.shape
    n = w_bf.shape[1]
    tm = _row_tile(m)
    return pl.pallas_call(
        _mm_kernel,
        grid=(m // tm,),
        in_specs=[pl.BlockSpec((tm, k), lambda i: (i, 0)), _full((k, n))],
        out_specs=pl.BlockSpec((tm, n), lambda i: (i, 0)),
        out_shape=jax.ShapeDtypeStruct((m, n), F32),
        compiler_params=_cp("parallel"),
        name="mm",
    )(a_bf, w_bf)


def _cmp_fin_kernel(a_ref, pb_ref, w2_ref, o_ref, *, nseg):
    a = a_ref[0]
    hw = a.shape[1] // 2
    bias = pb_ref[0:1, :hw] + pb_ref[1:2, hw:]
    pre = a[:, :hw] + pltpu.roll(a[:, hw:], nseg - 1, axis=0) + bias
    o_ref[0] = _dot(_gelu(pre).astype(BF16), w2_ref[...])


def _compress_weights(w1k, w1v, w2k, w2v, pek, pev):
    span = NSA_CMP_LEN // NSA_CMP_STRIDE
    st = NSA_CMP_STRIDE
    hid = w1k.shape[1]
    slots = 2 * NSA_KV
    eye = jnp.eye(slots, dtype=F32)
    per_slot = lambda k, v: jnp.stack([k] * NSA_KV + [v] * NSA_KV)
    w1 = per_slot(w1k.reshape(span, st, _HD, hid), w1v.reshape(span, st, _HD, hid))
    wbig = jnp.einsum('smide,st->isdmte', w1, eye)
    pe = per_slot(pek.reshape(span, st, _HD), pev.reshape(span, st, _HD))
    pex = jnp.pad(pe.transpose(1, 2, 0, 3), ((0, 8 - span), (0, 0), (0, 0), (0, 0)))
    w2big = jnp.einsum('sed,st->setd', per_slot(w2k, w2v), eye)
    return (wbig.reshape(st * slots * _HD, span * slots * hid).astype(BF16),
            pex.reshape(8, st * slots * _HD).astype(BF16),
            w2big.reshape(slots * hid, slots * _HD).astype(BF16))


def _compress(cmpflat_bf, wbig, pex, w2big):
    batch, nseg, kdim = cmpflat_bf.shape
    a = _mm(cmpflat_bf.reshape(batch * nseg, kdim), wbig).reshape(batch, nseg, -1)
    pb = _mm(pex, wbig)
    n2 = a.shape[2]
    ow = w2big.shape[1]
    return pl.pallas_call(
        functools.partial(_cmp_fin_kernel, nseg=nseg),
        grid=(batch,),
        in_specs=[pl.BlockSpec((1, nseg, n2), lambda b: (b, 0, 0)), _full(pb.shape), _full(w2big.shape)],
        out_specs=pl.BlockSpec((1, nseg, ow), lambda b: (b, 0, 0)),
        out_shape=jax.ShapeDtypeStruct((batch, nseg, ow), F32),
        compiler_params=_cp("parallel"),
        name="compress_fin",
    )(a, pb, w2big)


def _block_mean_kernel(k_ref, o_ref):
    o_ref[0, 0] = jnp.sum(k_ref[0], axis=0, keepdims=True) * (1.0 / MOBA_BLOCK)


def _block_mean(rows3, nbf, col_block):
    batch = rows3.shape[0]
    wd = _C_MK
    return pl.pallas_call(
        _block_mean_kernel,
        grid=(batch, nbf),
        in_specs=[pl.BlockSpec((1, MOBA_BLOCK, wd), lambda b, n: (b, n, col_block))],
        out_specs=pl.BlockSpec((1, 1, 1, wd), lambda b, n: (b, n, 0, 0)),
        out_shape=jax.ShapeDtypeStruct((batch, nbf, 1, wd), F32),
        compiler_params=_cp("parallel", "parallel"),
        name="block_mean",
    )(rows3)


def _top_rounds(score, row_id, k):
    far = 1 << 20
    sel = jnp.zeros(score.shape, F32)
    cur = score
    for _ in range(k):
        m = jnp.max(cur, axis=0, keepdims=True)
        idx = jnp.min(jnp.where(cur == m, row_id, far), axis=0, keepdims=True)
        pick = row_id == idx
        sel = jnp.where(pick, 1.0, sel)
        cur = jnp.where(pick, -jnp.inf, cur)
    return sel


def _softmax0(s, mask):
    s = jnp.where(mask, s, NEG)
    e = jnp.where(mask, jnp.exp(s - jnp.max(s, axis=0, keepdims=True)), 0.0)
    return e / jnp.maximum(jnp.sum(e, axis=0, keepdims=True), 1e-30)


def _tdot(a, b):
    return lax.dot_general(a, b, (((0,), (0,)), ((), ())), preferred_element_type=F32)


def _nsa_front(qu, kc, vc, mt, tpos, tq, nc, n_sel):
    nc_pad = kc.shape[0]
    nsel_pad = mt.shape[0]
    ci = lax.broadcasted_iota(jnp.int32, (nc_pad, 1), 0)
    bi = lax.broadcasted_iota(jnp.int32, (nsel_pad, tq), 0)
    cur = tpos[:, :tq] >> _SEL_SHIFT
    cmask = ((ci * NSA_CMP_STRIDE + (NSA_CMP_LEN - 1)) <= tpos) & (ci < nc)
    pc = _softmax0(_dot_hi(kc, qu), cmask)
    o_c = _tdot(vc.astype(BF16), pc.astype(BF16))
    imp = pc[:, 0:tq]
    for r in range(1, NSA_GRP):
        imp = imp + pc[:, r * tq:(r + 1) * tq]
    score = _dot_hi(mt, imp)
    forced = (bi == 0) | (bi == cur) | (bi == cur - 1)
    score = jnp.where(forced, BIG, score)
    score = jnp.where(bi <= cur, score, -BIG)
    score = jnp.where(bi < n_sel, score, -3e38)
    sel = _top_rounds(score, bi, min(NSA_TOPN, n_sel))
    return o_c, jnp.where(bi <= cur, sel, 0.0)


def _nsa_window(qr, kw, vw, tpos, wpos):
    wmask = (wpos <= tpos) & (wpos >= tpos - NSA_WINDOW) & (wpos >= 0)
    pw = _softmax0(_dot(kw.astype(BF16), qr), wmask).astype(BF16)
    return _tdot(vw.astype(BF16), pw)


def _online_softmax(m_ref, l_ref, acc_ref, s, mask, pv, guard=True):
    s = jnp.where(mask, s, NEG)
    m_old = m_ref[...]
    m_new = jnp.maximum(m_old, jnp.max(s, axis=0, keepdims=True))
    alpha = jnp.exp(m_old - m_new)
    p = jnp.exp(s - m_new)
    if guard:
        p = jnp.where(mask, p, 0.0)
    l_ref[...] = alpha * l_ref[...] + jnp.sum(p, axis=0, keepdims=True)
    acc_ref[...] = alpha * acc_ref[...] + pv(p.astype(BF16))
    m_ref[...] = m_new


def _heads_t(x, tq, groups, rep):
    heads = []
    for p in range(groups * rep // 2):
        xt = x[:, p * LANES:(p + 1) * LANES].T
        heads += [xt[0:_HD], xt[_HD:2 * _HD]]
    return [jnp.concatenate(heads[g * rep:(g + 1) * rep], axis=1) for g in range(groups)]


def _heads_untranspose(o_ref, per_group, tq, rep):
    heads = [og[:, r * tq:(r + 1) * tq] for og in per_group for r in range(rep)]
    for p in range(len(heads) // 2):
        o_ref[:, p * LANES:(p + 1) * LANES] = jnp.concatenate([heads[2 * p], heads[2 * p + 1]], axis=0).T


def _block_diag(qs):
    z = jnp.zeros_like(qs[0])
    return jnp.concatenate([jnp.concatenate([q if h == g else z for h in range(len(qs))], axis=1)
                            for g, q in enumerate(qs)], axis=0)


def _nsa_kernel(qu_ref, qr_ref, ng_ref, kcvc_ref, mt_ref, ksv_ref, win_ref,
                o_ref, sel_scr, m_scr, l_scr, acc_scr, *, tq, nc, n_sel, nkt_max, wl):
    qi = pl.program_id(1)
    n = NSA_GRP * tq
    tk = KEY_TILE
    kw_n = NSA_KV * _HD
    scale = _HD ** -0.5
    t0 = qi * tq
    lane = lax.broadcasted_iota(jnp.int32, (1, n), 1)
    tpos = t0 + (lane & (tq - 1))
    nkt = jnp.minimum((t0 + tq - 1) // tk + 1, nkt_max)
    wstart = pl.multiple_of(jnp.maximum(t0 - NSA_WINDOW, 0), LANES)
    wpos = wstart + lax.broadcasted_iota(jnp.int32, (wl, 1), 0)

    qus = _heads_t(qu_ref[...] * scale, tq, NSA_KV, NSA_GRP)
    qrs = [q.astype(BF16) for q in _heads_t(qr_ref[...] * scale, tq, NSA_KV, NSA_GRP)]
    gl_t = _sigmoid(ng_ref[...].T)
    gates = [[jnp.concatenate([gl_t[(g * NSA_GRP + r) * 3 + j:(g * NSA_GRP + r) * 3 + j + 1] for r in range(NSA_GRP)],
                              axis=1) for j in range(3)] for g in range(NSA_KV)]
    kcvc = kcvc_ref[0]
    win = win_ref[0, pl.ds(wstart, wl), :]

    o_cw = []
    for g in range(NSA_KV):
        hs = slice(g * _HD, (g + 1) * _HD)
        o_c, sel = _nsa_front(qus[g], kcvc[:, hs], kcvc[:, kw_n + g * _HD:kw_n + (g + 1) * _HD], mt_ref[...],
                              tpos, tq, nc, n_sel)
        sel_scr[g] = sel
        o_w = _nsa_window(qrs[g], win[:, hs], win[:, kw_n + g * _HD:kw_n + (g + 1) * _HD], tpos, wpos)
        o_cw.append(gates[g][0] * o_c + gates[g][2] * o_w)
    m_scr[...] = jnp.full(m_scr.shape, NEG, F32)
    l_scr[...] = jnp.zeros(l_scr.shape, F32)
    acc_scr[...] = jnp.zeros(acc_scr.shape, F32)
    qbd = _block_diag(qrs)

    def tile(j, causal):
        nb = tk // NSA_SEL_BLOCK
        rows = pl.ds(pl.multiple_of(j * tk, tk), tk)
        ksv = ksv_ref[0, rows, :]
        masks = []
        for g in range(NSA_KV):
            pieces = [jnp.broadcast_to(sel_scr[g, pl.ds(j * nb + u, 1), :], (NSA_SEL_BLOCK, tq)) for u in range(nb)]
            mk = jnp.concatenate(pieces, axis=0)
            masks.append(jnp.concatenate([mk] * NSA_GRP, axis=1) > 0.5)
        mask = jnp.concatenate(masks, axis=1)
        if causal:
            kpos = j * tk + lax.broadcasted_iota(jnp.int32, (tk, 1), 0)
            mask = mask & jnp.concatenate([kpos <= tpos] * NSA_KV, axis=1)
        vb = ksv[:, kw_n:].astype(BF16)
        _online_softmax(m_scr, l_scr, acc_scr, _dot(ksv[:, :kw_n].astype(BF16), qbd), mask,
                        lambda p: _tdot(vb, p), guard=False)

    def body(j, carry):
        tile(j, False)
        return carry

    lax.fori_loop(0, nkt - 1, body, 0)
    tile(nkt - 1, True)
    o_s = acc_scr[...] / jnp.maximum(l_scr[...], 1e-30)
    outs = [o_cw[g] + gates[g][1] * o_s[g * _HD:(g + 1) * _HD, g * n:(g + 1) * n] for g in range(NSA_KV)]
    _heads_untranspose(o_ref, outs, tq, NSA_GRP)


def _nsa_select_kernel(qu_ref, qr_ref, gl_ref, kc_ref, vc_ref, mt_ref, kw_ref, vw_ref,
                       o_ref, g1_ref, sel_ref, *, tq, nc, n_sel, qpos0, wpos0):
    n = NSA_GRP * tq
    lane = lax.broadcasted_iota(jnp.int32, (1, n), 1)
    tpos = qpos0 + (lane & (tq - 1))
    wpos = wpos0 + lax.broadcasted_iota(jnp.int32, (kw_ref.shape[2], 1), 0)
    for g in range(NSA_KV):
        qr = qr_ref[0, 0, g].astype(BF16)
        o_c, sel = _nsa_front(qu_ref[0, 0, g], kc_ref[0, g], vc_ref[0, g], mt_ref[...], tpos, tq, nc, n_sel)
        sel_ref[0, g] = sel
        o_w = _nsa_window(qr, kw_ref[0, g], vw_ref[0, g], tpos, wpos)
        gate = _sigmoid(gl_ref[0, 0, g])
        o_ref[0, g] = gate[0:1] * o_c + gate[2:3] * o_w
        g1_ref[0, g] = gate[1:2]


def _moba_select(q, kb, own, nbf):
    ni = lax.broadcasted_iota(jnp.int32, (kb.shape[0], q.shape[1]), 0)
    gs = _dot_hi(kb, q)
    gs = jnp.where(ni < own, gs, -BIG)
    gs = jnp.where(ni < nbf, gs, -3e38)
    sel = _top_rounds(gs, ni, min(MOBA_TOPK, nbf))
    return jnp.where((ni < own) & (ni < nbf), sel, 0.0)


def _moba_select_kernel(q_ref, kb_ref, sel_ref, *, own, nbf):
    for g in range(MOBA_KV):
        sel_ref[0, g] = _moba_select(q_ref[0, 0, g], kb_ref[0, g], own, nbf)


def _moba_kernel(q_ref, kb_ref, kmv_ref, o_ref, sel_scr, m_scr, l_scr, acc_scr, *, tq, nbf, nkt_max):
    qi = pl.program_id(1)
    n = MOBA_GRP * tq
    tk = KEY_TILE
    assert tk == MOBA_BLOCK
    t0 = qi * tq
    lane = lax.broadcasted_iota(jnp.int32, (1, MOBA_KV * n), 1)
    tpos = t0 + (lane & (tq - 1))
    own = t0 // MOBA_BLOCK
    scale = _HD ** -0.5

    qs = _heads_t(q_ref[...], tq, MOBA_KV, MOBA_GRP)
    for g in range(MOBA_KV):
        sel_scr[g] = _moba_select(qs[g], kb_ref[0, g], own, nbf)
    qbd = _block_diag([(q * scale).astype(BF16) for q in qs])
    m_scr[...] = jnp.full(m_scr.shape, NEG, F32)
    l_scr[...] = jnp.zeros(l_scr.shape, F32)
    acc_scr[...] = jnp.zeros(acc_scr.shape, F32)

    def tile(j, mask):
        kmv = kmv_ref[0, pl.ds(pl.multiple_of(j * tk, tk), tk), :]
        vb = kmv[:, _C_MK:].astype(BF16)
        _online_softmax(m_scr, l_scr, acc_scr, _dot(kmv[:, :_C_MK].astype(BF16), qbd), mask,
                        lambda p: _tdot(vb, p), guard=False)

    kpos = own * tk + lax.broadcasted_iota(jnp.int32, (tk, 1), 0)
    tile(own, kpos <= tpos)

    def body(j, carry):
        picked = jnp.concatenate([jnp.broadcast_to(sel_scr[g, pl.ds(j, 1), :], (tk, n)) for g in range(MOBA_KV)],
                                 axis=1)
        tile(j, picked > 0.5)
        return carry

    lax.fori_loop(0, jnp.minimum(own, nkt_max), body, 0)
    o = acc_scr[...] / jnp.maximum(l_scr[...], 1e-30)
    _heads_untranspose(o_ref, [o[g * _HD:(g + 1) * _HD, g * n:(g + 1) * n] for g in range(MOBA_KV)], tq, MOBA_GRP)


def _to_qt(x, batch, nt, tq, groups, rep):
    x = x.reshape(batch, nt, tq, groups, rep, _HD).transpose(0, 1, 3, 5, 4, 2)
    return x.reshape(batch, nt, groups, _HD, rep * tq)


def _from_qt(x, batch, nt, tq, groups, rep):
    x = x.reshape(batch, nt, groups, _HD, rep, tq).transpose(0, 1, 5, 2, 4, 3)
    return x.reshape(batch, nt * tq, groups * rep * _HD)


def _head_major(x, heads):
    b, l, _ = x.shape
    return x.reshape(b, l, heads, _HD).transpose(0, 2, 1, 3)


def _vt_tiles(x, heads, tile):
    b, l, _ = x.shape
    return x.reshape(b, l // tile, tile, heads, _HD).transpose(0, 3, 1, 4, 2)


def _sel_score_matrix(nsel_pad, nc_pad, nc):
    ratio = NSA_SEL_BLOCK // NSA_CMP_STRIDE
    span = NSA_CMP_LEN // NSA_CMP_STRIDE
    mt = np.zeros((nsel_pad, nc_pad), np.float32)
    for j in range(nsel_pad):
        for o in range(-(span - 1), ratio):
            c = j * ratio + o
            wt = sum(1 for m in range(ratio) for q in range(span) if m - q == o)
            if 0 <= c < nc:
                mt[j, c] = wt
    return jnp.asarray(mt)


def _mixer_c_attention(qu, qr, mq, ng, rows3, win3, kcvc, batch, seq):
    tk = KEY_TILE
    tq = LANES
    assert seq % tk == 0 and seq >= NSA_WINDOW + tq
    nt = seq // tq
    nkt_max = seq // tk
    nc = (seq - NSA_CMP_LEN) // NSA_CMP_STRIDE + 1
    nseg = kcvc.shape[1]
    n_sel = seq // NSA_SEL_BLOCK
    nsel_pad = -(-n_sel // 8) * 8
    nbf = seq // MOBA_BLOCK
    nbf_pad = -(-nbf // 8) * 8
    wl = NSA_WINDOW + tq
    mt = _sel_score_matrix(nsel_pad, nseg, nc)
    kw_n = NSA_KV * _HD

    kb = _block_mean(rows3, nbf, col_block=2)[:, :, 0, :]
    kb = _head_major(jnp.pad(kb, ((0, 0), (0, nbf_pad - nbf), (0, 0))), MOBA_KV)

    n_n = NSA_GRP * tq
    n_m = MOBA_GRP * tq
    tile = lambda wd: pl.BlockSpec((tq, wd), lambda b, i: (b * nt + i, 0))
    per_b = lambda shp, cb=0: pl.BlockSpec((1,) + shp, lambda b, i: (b,) + (0,) * (len(shp) - 1) + (cb,))
    o_n = pl.pallas_call(
        functools.partial(_nsa_kernel, tq=tq, nc=nc, n_sel=n_sel, nkt_max=nkt_max, wl=wl),
        grid=(batch, nt),
        in_specs=[tile(NSA_HEADS * _HD), tile(NSA_HEADS * _HD), tile(LANES), per_b((nseg, 2 * kw_n)), _full(mt.shape),
                  per_b((seq, 2 * kw_n), 1), per_b((seq, 2 * kw_n))],
        out_specs=tile(NSA_HEADS * _HD),
        out_shape=jax.ShapeDtypeStruct((batch * seq, NSA_HEADS * _HD), F32),
        scratch_shapes=[pltpu.VMEM((NSA_KV, nsel_pad, tq), F32), pltpu.VMEM((1, NSA_KV * n_n), F32),
                        pltpu.VMEM((1, NSA_KV * n_n), F32), pltpu.VMEM((kw_n, NSA_KV * n_n), F32)],
        compiler_params=_cp("parallel", "arbitrary"),
        name="nsa",
    )(qu, qr, ng, kcvc, mt, rows3, win3)

    o_m = pl.pallas_call(
        functools.partial(_moba_kernel, tq=tq, nbf=nbf, nkt_max=nkt_max),
        grid=(batch, nt),
        in_specs=[tile(MOBA_HEADS * _HD), per_b(kb.shape[1:]), per_b((seq, 2 * _C_MK), 1)],
        out_specs=tile(MOBA_HEADS * _HD),
        out_shape=jax.ShapeDtypeStruct((batch * seq, MOBA_HEADS * _HD), F32),
        scratch_shapes=[pltpu.VMEM((MOBA_KV, nbf_pad, n_m), F32), pltpu.VMEM((1, MOBA_KV * n_m), F32),
                        pltpu.VMEM((1, MOBA_KV * n_m), F32), pltpu.VMEM((_C_MK, MOBA_KV * n_m), F32)],
        compiler_params=_cp("parallel", "arbitrary"),
        name="moba",
    )(mq, kb, rows3)
    return o_n, o_m


PAGES_PER_STEP = 4
_SEG_PER_PAGE = PAGE_SIZE // NSA_CMP_STRIDE


def _repack_kernel(pt_ref, *refs):
    pps = PAGES_PER_STEP
    pg_refs = refs[:pps]
    flat_ref, kbar_ref, sel_ref, mob_ref = refs[pps:]
    nh = KV_ROW_HEADS
    wd = 2 * _C_KV

    def pair(ref, h):
        return jnp.concatenate([ref[0, 0, :, h, :], ref[0, 0, :, h + 1, :]], axis=1)

    ksum = []
    for pg, ref in enumerate(pg_refs):
        r0 = pg * PAGE_SIZE
        sel_ref[0, r0:r0 + PAGE_SIZE, 0:_C_KV] = pair(ref, 2 * NSA_KV).astype(BF16)
        sel_ref[0, r0:r0 + PAGE_SIZE, _C_KV:wd] = pair(ref, 3 * NSA_KV).astype(BF16)
        sums = []
        for u in range(MOBA_KV):
            x = pair(ref, 4 * NSA_KV + 2 * u)
            mob_ref[0, r0:r0 + PAGE_SIZE, u * LANES:(u + 1) * LANES] = x.astype(BF16)
            if u < MOBA_KV // 2:
                sums.append(jnp.sum(x, axis=0, keepdims=True))
        ksum.append(jnp.concatenate(sums, axis=1))
    for pr in range(pps // 2):
        kbar_ref[0, pr] = (ksum[2 * pr] + ksum[2 * pr + 1]) * (1.0 / MOBA_BLOCK)
        for i in range(NSA_CMP_STRIDE):
            for u in range(0, 2 * NSA_KV, 2):
                def seg(ref, h):
                    return ref[0, 0, pl.ds(i, _SEG_PER_PAGE, stride=NSA_CMP_STRIDE), h, :]
                x = jnp.concatenate([jnp.concatenate([seg(pg_refs[2 * pr + q], u), seg(pg_refs[2 * pr + q], u + 1)], axis=1)
                                     for q in range(2)], axis=0)
                c0 = i * wd + u * _HD
                flat_ref[0, 2 * pr * _SEG_PER_PAGE:(2 * pr + 2) * _SEG_PER_PAGE, c0:c0 + LANES] = x.astype(BF16)


def _repack_pages(pool, pt, layer):
    batch, npages = pt.shape
    n_layers = pool.shape[1]
    assert npages % PAGES_PER_STEP == 0 and MOBA_BLOCK == 2 * PAGE_SIZE
    ng = npages // PAGES_PER_STEP
    past = npages * PAGE_SIZE
    nseg = npages * _SEG_PER_PAGE
    nbf = npages // 2
    wd = 2 * _C_KV
    rows = PAGES_PER_STEP * PAGE_SIZE

    def page_spec(pg):
        return pl.BlockSpec((1, 1, PAGE_SIZE, KV_ROW_HEADS, _HD),
                            lambda b, j, p: (p[b * npages + j * PAGES_PER_STEP + pg], layer, 0, 0, 0))

    gs = pltpu.PrefetchScalarGridSpec(
        num_scalar_prefetch=1,
        grid=(batch, ng),
        in_specs=[page_spec(pg) for pg in range(PAGES_PER_STEP)],
        out_specs=[pl.BlockSpec((1, PAGES_PER_STEP * _SEG_PER_PAGE, NSA_CMP_STRIDE * wd), lambda b, j, p: (b, j, 0)),
                   pl.BlockSpec((1, PAGES_PER_STEP // 2, 1, _C_MK), lambda b, j, p: (b, j, 0, 0)),
                   pl.BlockSpec((1, rows, wd), lambda b, j, p: (b, j, 0)),
                   pl.BlockSpec((1, rows, 2 * _C_MK), lambda b, j, p: (b, j, 0))],
    )
    flat, kbar, selkv, mobkv = pl.pallas_call(
        _repack_kernel,
        grid_spec=gs,
        out_shape=[jax.ShapeDtypeStruct((batch, nseg, NSA_CMP_STRIDE * wd), BF16),
                   jax.ShapeDtypeStruct((batch, nbf, 1, _C_MK), F32),
                   jax.ShapeDtypeStruct((batch, past, wd), BF16),
                   jax.ShapeDtypeStruct((batch, past, 2 * _C_MK), BF16)],
        compiler_params=_cp("parallel", "arbitrary"),
        name="repack_pages",
    )(pt.reshape(-1), *([pool] * PAGES_PER_STEP))
    return flat, kbar[:, :, 0, :], selkv, mobkv


def _flash_update(m_ref, l_ref, acc_ref, k, v, q, mask):
    vb = v.astype(BF16)
    _online_softmax(m_ref, l_ref, acc_ref, _dot(k.astype(BF16), q), mask, lambda p: _tdot(vb, p))


def _dec_attn_kernel(sel_ref, mob_ref, qn_ref, qm_ref, smask_ref, mmask_ref, nsel_ref, nmob_ref, opart_ref, g1_ref,
                     on_ref, om_ref, mn, ln, accn, mm, lm, accm, *, past, nsteps, seq_pad):
    j = pl.program_id(1)
    kw_n = NSA_KV * _HD
    kw_m = _C_MK

    @pl.when(j == 0)
    def _():
        mn[...] = jnp.full(mn.shape, NEG, F32)
        mm[...] = jnp.full(mm.shape, NEG, F32)
        ln[...] = jnp.zeros(ln.shape, F32)
        lm[...] = jnp.zeros(lm.shape, F32)
        accn[...] = jnp.zeros(accn.shape, F32)
        accm[...] = jnp.zeros(accm.shape, F32)

    qn = qn_ref[0]
    qm = qm_ref[0]
    rows = sel_ref.shape[1]
    ksv = sel_ref[0]
    nb = rows // NSA_SEL_BLOCK
    srow = smask_ref[0, pl.ds(pl.multiple_of(j * nb, nb), nb), :]
    smask = jnp.concatenate([jnp.broadcast_to(srow[u:u + 1], (NSA_SEL_BLOCK, LANES)) for u in range(nb)], axis=0)
    _flash_update(mn, ln, accn, ksv[:, :kw_n], ksv[:, kw_n:], qn, smask > 0.5)

    kmv = mob_ref[0]
    nbm = rows // MOBA_BLOCK
    mmask = jnp.concatenate([jnp.broadcast_to(mmask_ref[0, pl.ds(j * nbm + u, 1), :], (MOBA_BLOCK, LANES))
                             for u in range(nbm)], axis=0)
    _flash_update(mm, lm, accm, kmv[:, :kw_m], kmv[:, kw_m:], qm, mmask > 0.5)

    @pl.when(j == nsteps - 1)
    def _():
        lane = lax.broadcasted_iota(jnp.int32, (1, LANES), 1)
        tpos = past + (lane & (seq_pad - 1))
        new_rows = nsel_ref.shape[1]
        kpos = past + lax.broadcasted_iota(jnp.int32, (new_rows, 1), 0)
        causal = kpos <= tpos
        new_blk = past // NSA_SEL_BLOCK
        nrow = jnp.broadcast_to(smask_ref[0, new_blk:new_blk + 1, :], (new_rows, LANES)) > 0.5
        nsel = nsel_ref[0]
        _flash_update(mn, ln, accn, nsel[:, :kw_n], nsel[:, kw_n:], qn, causal & nrow)
        nmob = nmob_ref[0]
        _flash_update(mm, lm, accm, nmob[:, :kw_m], nmob[:, kw_m:], qm, causal)
        on_ref[0] = opart_ref[0] + g1_ref[0] * (accn[...] / jnp.maximum(ln[...], 1e-30))
        om_ref[0] = accm[...] / jnp.maximum(lm[...], 1e-30)


def _block_diag_q(x, batch, seq, groups, rep):
    x = x.reshape(batch, seq, groups, rep, _HD).transpose(0, 2, 4, 3, 1).reshape(batch, groups, _HD, rep * seq)
    bd = jnp.einsum('bgdn,gh->bgdhn', x, jnp.eye(groups, dtype=x.dtype))
    bd = bd.reshape(batch, groups * _HD, groups * rep * seq)
    return jnp.pad(bd, ((0, 0), (0, 0), (0, LANES - bd.shape[2])))


def _diag_blocks(o, batch, seq, groups, rep):
    o = o[:, :, :groups * rep * seq].reshape(batch, groups, _HD, groups, rep, seq)
    o = jnp.stack([o[:, g, :, g] for g in range(groups)], axis=1)
    return o.transpose(0, 4, 1, 3, 2).reshape(batch * seq, groups * rep * _HD)


def _decode_attention(qu, qr, mq, ng, rows3, win3, pool, pt, cache_win, layer, wbig, pex, w2big, batch, seq):
    npages = pt.shape[1]
    past = npages * PAGE_SIZE
    lt = past + seq
    assert seq <= 8 and seq < NSA_CMP_STRIDE and past % MOBA_BLOCK == 0
    seq_pad = 8
    tq = 64
    scale = _HD ** -0.5

    cmpflat, kbar, selkv, mobkv = _repack_pages(pool, pt, layer)
    kcvc = _compress(cmpflat, wbig, pex, w2big)
    nseg = kcvc.shape[1]
    nc = (lt - NSA_CMP_LEN) // NSA_CMP_STRIDE + 1
    assert nseg == nc + 1
    n_sel = -(-lt // NSA_SEL_BLOCK)
    nsel_pad = -(-n_sel // 8) * 8
    nbf = lt // MOBA_BLOCK
    nbf_pad = -(-(nbf + 1) // 8) * 8
    ng_steps = npages // PAGES_PER_STEP

    def padq(x):
        x = x.reshape(batch, seq, -1)
        return jnp.pad(x, ((0, 0), (0, tq - seq), (0, 0))).reshape(batch * tq, -1)

    qu_t = _to_qt(padq(qu) * scale, batch, 1, tq, NSA_KV, NSA_GRP)
    qr_t = _to_qt(padq(qr) * scale, batch, 1, tq, NSA_KV, NSA_GRP)
    mq_t = _to_qt(padq(mq), batch, 1, tq, MOBA_KV, MOBA_GRP)
    gl = padq(ng)[:, :NSA_HEADS * 3].reshape(batch, 1, tq, NSA_KV, NSA_GRP, 3).transpose(0, 1, 3, 5, 4, 2)
    gl = gl.reshape(batch, 1, NSA_KV, 3, NSA_GRP * tq)
    kc = _head_major(kcvc[:, :, 0:_C_KV], NSA_KV)
    vc = _head_major(kcvc[:, :, _C_KV:2 * _C_KV], NSA_KV)
    mt = _sel_score_matrix(nsel_pad, nseg, nc)
    wb = cache_win.reshape(batch, -1, 2 * _C_KV)
    wcat = jnp.concatenate([wb, win3], axis=1)
    wl = -(-wcat.shape[1] // LANES) * LANES
    wfull = jnp.pad(wcat, ((0, 0), (0, wl - wcat.shape[1]), (0, 0)))
    kw = _head_major(wfull[:, :, 0:_C_KV], NSA_KV).astype(BF16)
    vw = _head_major(wfull[:, :, _C_KV:], NSA_KV).astype(BF16)
    kb = _head_major(jnp.pad(kbar, ((0, 0), (0, nbf_pad - nbf), (0, 0))), MOBA_KV)

    n_n = NSA_GRP * tq
    n_m = MOBA_GRP * tq
    b1 = lambda shp: pl.BlockSpec((1,) + shp, lambda b: (b,) + (0,) * len(shp))
    o_part, g1, sel = pl.pallas_call(
        functools.partial(_nsa_select_kernel, tq=tq, nc=nc, n_sel=n_sel, qpos0=past, wpos0=past - wb.shape[1]),
        grid=(batch,),
        in_specs=[b1((1, NSA_KV, _HD, n_n)), b1((1, NSA_KV, _HD, n_n)), b1((1, NSA_KV, 3, n_n)),
                  b1(kc.shape[1:]), b1(vc.shape[1:]), _full(mt.shape), b1(kw.shape[1:]), b1(vw.shape[1:])],
        out_specs=[b1((NSA_KV, _HD, n_n)), b1((NSA_KV, 1, n_n)), b1((NSA_KV, nsel_pad, tq))],
        out_shape=[jax.ShapeDtypeStruct((batch, NSA_KV, _HD, n_n), F32),
                   jax.ShapeDtypeStruct((batch, NSA_KV, 1, n_n), F32),
                   jax.ShapeDtypeStruct((batch, NSA_KV, nsel_pad, tq), F32)],
        compiler_params=_cp("parallel"),
        name="nsa_select",
    )(qu_t, qr_t, gl, kc, vc, mt, kw, vw)
    msel = pl.pallas_call(
        functools.partial(_moba_select_kernel, own=past // MOBA_BLOCK, nbf=nbf),
        grid=(batch,),
        in_specs=[b1((1, MOBA_KV, _HD, n_m)), b1(kb.shape[1:])],
        out_specs=b1((MOBA_KV, nbf_pad, n_m)),
        out_shape=jax.ShapeDtypeStruct((batch, MOBA_KV, nbf_pad, n_m), F32),
        compiler_params=_cp("parallel"),
        name="moba_select",
    )(mq_t, kb)

    def lanes(x):
        return jnp.pad(x, [(0, 0)] * (x.ndim - 1) + [(0, LANES - x.shape[-1])])

    pad_s = lambda x: jnp.pad(x.reshape(batch, seq, -1), ((0, 0), (0, seq_pad - seq), (0, 0)))
    qn_bd = _block_diag_q(pad_s(qr).reshape(batch * seq_pad, -1) * scale, batch, seq_pad, NSA_KV, NSA_GRP).astype(BF16)
    qm_bd = _block_diag_q(pad_s(mq).reshape(batch * seq_pad, -1) * scale, batch, seq_pad, MOBA_KV, MOBA_GRP).astype(BF16)
    smask = jnp.broadcast_to(sel[:, :, :, None, :seq_pad], (batch, NSA_KV, nsel_pad, NSA_GRP, seq_pad))
    smask = lanes(smask.transpose(0, 2, 1, 3, 4).reshape(batch, nsel_pad, -1))
    mmask = msel.reshape(batch, MOBA_KV, nbf_pad, MOBA_GRP, tq)[..., :seq_pad]
    mmask = lanes(mmask.transpose(0, 2, 1, 3, 4).reshape(batch, nbf_pad, -1))
    op = o_part.reshape(batch, NSA_KV, _HD, NSA_GRP, tq)[..., :seq_pad]
    op_bd = jnp.einsum('bgdrt,gh->bgdhrt', op, jnp.eye(NSA_KV, dtype=F32))
    op_bd = lanes(op_bd.reshape(batch, NSA_KV * _HD, -1))
    g1l = g1.reshape(batch, NSA_KV, NSA_GRP, tq)[..., :seq_pad].reshape(batch, 1, -1)
    g1l = lanes(g1l)
    new_pad = 16
    pad_n = lambda x: jnp.pad(x, ((0, 0), (0, new_pad - seq), (0, 0)))
    new_sel = pad_n(rows3[:, :, 2 * _C_KV:4 * _C_KV])
    new_mob = pad_n(rows3[:, :, 4 * _C_KV:])

    kw_n = NSA_KV * _HD
    rows = PAGES_PER_STEP * PAGE_SIZE
    bj = lambda shp: pl.BlockSpec((1,) + shp, lambda b, j: (b,) + (0,) * len(shp))
    o_n, o_m = pl.pallas_call(
        functools.partial(_dec_attn_kernel, past=past, nsteps=ng_steps, seq_pad=seq_pad),
        grid=(batch, ng_steps),
        in_specs=[pl.BlockSpec((1, rows, 2 * kw_n), lambda b, j: (b, j, 0)),
                  pl.BlockSpec((1, rows, 2 * _C_MK), lambda b, j: (b, j, 0)),
                  bj((kw_n, LANES)), bj((_C_MK, LANES)), bj((nsel_pad, LANES)), bj((nbf_pad, LANES)),
                  bj((new_pad, 2 * kw_n)), bj((new_pad, 2 * _C_MK)), bj((kw_n, LANES)), bj((1, LANES))],
        out_specs=[bj((kw_n, LANES)), bj((_C_MK, LANES))],
        out_shape=[jax.ShapeDtypeStruct((batch, kw_n, LANES), F32), jax.ShapeDtypeStruct((batch, _C_MK, LANES), F32)],
        scratch_shapes=[pltpu.VMEM((1, LANES), F32), pltpu.VMEM((1, LANES), F32), pltpu.VMEM((kw_n, LANES), F32),
                        pltpu.VMEM((1, LANES), F32), pltpu.VMEM((1, LANES), F32), pltpu.VMEM((_C_MK, LANES), F32)],
        compiler_params=_cp("parallel", "arbitrary"),
        name="decode_attn",
    )(selkv, mobkv, qn_bd, qm_bd, smask, mmask, new_sel, new_mob, op_bd, g1l)
    o_n = _diag_blocks(o_n, batch, seq_pad, NSA_KV, NSA_GRP).reshape(batch, seq_pad, -1)[:, :seq]
    o_m = _diag_blocks(o_m, batch, seq_pad, MOBA_KV, MOBA_GRP).reshape(batch, seq_pad, -1)[:, :seq]
    return o_n.reshape(batch * seq, -1), o_m.reshape(batch * seq, -1)


def _prep_weights(prm):
    w = {}
    w['w_in_a'] = prm['w_in_a'].astype(BF16)
    w['w_out_a'] = prm['w_out_a'].astype(BF16)
    w['w_out_c'] = prm['w_out_c'].astype(BF16)
    w['w_in_c'] = [_proj_c_weights(prm['w_in_c'][c]) for c in range(prm['w_in_c'].shape[0])]
    w['cmp'] = [_compress_weights(prm['cmp_w1_k'][c], prm['cmp_w1_v'][c], prm['cmp_w2_k'][c], prm['cmp_w2_v'][c],
                                  prm['cmp_pe_k'][c], prm['cmp_pe_v'][c]) for c in range(prm['w_in_c'].shape[0])]
    depth, d, _ = prm['router_c_w'].shape
    wr = jnp.concatenate([prm['router_c_w'], prm['router_f_w'].transpose(0, 2, 1, 3).reshape(depth, d, -1)], axis=2)
    w['router_w'] = jnp.pad(wr, ((0, 0), (0, 0), (0, LANES - wr.shape[2])))
    br = jnp.concatenate([prm['router_c_b'], prm['router_f_b'].reshape(depth, -1)], axis=1)
    w['router_b'] = jnp.pad(br, ((0, 0), (0, LANES - br.shape[1])))[:, None, :]
    w['moe_w1'] = prm['moe_w1']
    w['moe_w3'] = prm['moe_w3']
    w['moe_w2'] = prm['moe_w2']
    w['ple_w'] = prm['ple_w'].astype(BF16)
    w['ple_gate_w'] = prm['ple_gate_w'].astype(BF16)
    w['lb'] = jnp.cumsum(jax.nn.softmax(prm['hgrn_lb'].astype(F32), axis=0), axis=0)
    return w


def _forward(x, p, pos, prm, w, ctx):
    batch, seq, d = x.shape
    t = batch * seq
    depth = p.shape[0]
    h = x.reshape(t, d)
    outs = {}
    for i in range(depth):
        gmix = prm['norm_mix'][i][None, :]
        if i % 2 == 0:
            a = i // 2
            lb512 = jnp.tile(w['lb'][a], HG_HEADS)[None, :]
            q, k, lf, iv, sg, gu, vn = _proj_a(h, gmix, w['w_in_a'][a], lb512, prm['cm_vnorm'][a][None, :])
            if ctx is None:
                st0 = jnp.zeros((batch, HG_WIDTH, HG_WIDTH), F32)
            else:
                s0 = ctx['state_hgrn'][a].astype(F32)
                eye = jnp.eye(HG_HEADS, dtype=F32)
                st0 = jnp.einsum('bhde,hg->bhegd', s0, eye).reshape(batch, HG_WIDTH, HG_WIDTH)
            o, st = _hgrn(q, k, lf, iv, sg, st0, prm['hgrn_onorm'][a][None, :], batch, seq)
            st5 = st.reshape(batch, HG_HEADS, HG_DK, HG_HEADS, HG_DK)
            s_new = jnp.stack([st5[:, hh, :, hh, :] for hh in range(HG_HEADS)], axis=1).transpose(0, 1, 3, 2)
            cm = _cmix(vn, gu, prm['cm_ws'][a], prm['cm_bs'][a].T, batch, seq)
            h = _out_proj(h, o, cm, w['w_out_a'][a])
            outs.setdefault('hg', []).append(s_new)
            outs.setdefault('cm', []).append(vn.reshape(batch, seq, CM_WIDTH))
        else:
            c = i // 2
            qu, qr, mq, ng, rows, win, *flat = _proj_c(h, gmix, w['w_in_c'][c], pos, seq, want_flat=ctx is None)
            rows3 = rows.reshape(batch, seq, ROW_W)
            win3 = win.reshape(batch, seq, 2 * _C_KV)
            wbig, pex, w2big = w['cmp'][c]
            if ctx is None:
                assert seq % KEY_TILE == 0
                nseg = seq // NSA_CMP_STRIDE
                kcvc = _compress(flat[0].reshape(batch, nseg, -1), wbig, pex, w2big)
                o_n, o_m = _mixer_c_attention(qu, qr, mq, ng, rows3, win3, kcvc, batch, seq)
                new_win = win3[:, seq - min(NSA_WINDOW, seq):]
            else:
                o_n, o_m = _decode_attention(qu, qr, mq, ng, rows3, win3, ctx['cache_kv'], ctx['page_table'],
                                             ctx['cache_win'][c], c, wbig, pex, w2big, batch, seq)
                new_win = win3
            h = _out_proj(h, o_n, o_m, w['w_out_c'][c])
            outs.setdefault('kv', []).append(rows3.reshape(batch, seq, KV_ROW_HEADS, _HD))
            outs.setdefault('win', []).append(new_win.reshape(batch, -1, 2 * NSA_KV, _HD))
        y = _moe(h, prm['norm_ffn'][i][None, :], w['router_w'][i], w['router_b'][i],
                 w['moe_w1'][i], w['moe_w3'][i], w['moe_w2'][i])
        h = _ple(h, y, p[i].reshape(t, -1), prm['norm_ple'][i][None, :], w['ple_gate_w'][i], w['ple_w'][i],
                 prm['final_norm'][None, :], final=(i == depth - 1))
    return h.reshape(batch, seq, d), outs


def kernel(x_prompt, x_sample, cache_kv, cache_win, state_hgrn, page_table, p_prompt, p_sample,
           norm_mix, norm_ffn, norm_ple, final_norm, w_in_a, w_out_a, hgrn_lb, hgrn_onorm, cm_vnorm,
           cm_ws, cm_bs, w_in_c, w_out_c, cmp_pe_k, cmp_w1_k, cmp_w2_k, cmp_pe_v, cmp_w1_v, cmp_w2_v,
           router_c_w, router_c_b, router_f_w, router_f_b, moe_w1, moe_w3, moe_w2, ple_w, ple_gate_w):
    prm = dict(norm_mix=norm_mix, norm_ffn=norm_ffn, norm_ple=norm_ple, final_norm=final_norm,
               w_in_a=w_in_a, w_out_a=w_out_a, hgrn_lb=hgrn_lb, hgrn_onorm=hgrn_onorm, cm_vnorm=cm_vnorm,
               cm_ws=cm_ws, cm_bs=cm_bs, w_in_c=w_in_c, w_out_c=w_out_c, cmp_pe_k=cmp_pe_k,
               cmp_w1_k=cmp_w1_k, cmp_w2_k=cmp_w2_k, cmp_pe_v=cmp_pe_v, cmp_w1_v=cmp_w1_v, cmp_w2_v=cmp_w2_v,
               router_c_w=router_c_w, router_c_b=router_c_b, router_f_w=router_f_w, router_f_b=router_f_b,
               moe_w1=moe_w1, moe_w3=moe_w3, moe_w2=moe_w2, ple_w=ple_w, ple_gate_w=ple_gate_w)
    w = _prep_weights(prm)
    past = page_table.shape[1] * PAGE_SIZE
    y_p, o_p = _forward(x_prompt, p_prompt, jnp.arange(x_prompt.shape[1]), prm, w, None)
    ctx = dict(cache_kv=cache_kv, cache_win=cache_win, state_hgrn=state_hgrn, page_table=page_table)
    y_s, o_s = _forward(x_sample, p_sample, past + jnp.arange(x_sample.shape[1]), prm, w, ctx)
    return (y_p, y_s,
            jnp.stack(o_p['kv'], axis=1), jnp.stack(o_s['kv'], axis=1),
            jnp.stack(o_p['win'], axis=0), jnp.stack(o_s['win'], axis=0),
            jnp.stack(o_p['hg'], axis=0), jnp.stack(o_s['hg'], axis=0),
            jnp.stack(o_s['cm'], axis=0))
```

```python
import functools
import math

import numpy as np
import jax
import jax.numpy as jnp
from jax import lax
from jax.experimental import pallas as pl
from jax.experimental.pallas import tpu as pltpu

F32 = jnp.float32
BF16 = jnp.bfloat16
HI = lax.Precision.HIGHEST

PAGE_SIZE = 128
HEAD_DIM = 64
ROT_DIM = HEAD_DIM // 4
ROPE_THETA = 500000.0
HG_HEADS = 8
HG_DK = 64
HG_WIDTH = HG_HEADS * HG_DK
HG_CHUNK = 64
CM_GROUPS = 4
CM_GW = 128
CM_WIDTH = CM_GROUPS * CM_GW
CM_CHUNK = 128
NSA_HEADS = 8
NSA_KV = 2
NSA_GRP = NSA_HEADS // NSA_KV
NSA_CMP_LEN = 32
NSA_CMP_STRIDE = 16
NSA_SEL_BLOCK = 64
NSA_TOPN = 16
NSA_WINDOW = 512
MOBA_HEADS = 8
MOBA_KV = 4
MOBA_GRP = MOBA_HEADS // MOBA_KV
MOBA_BLOCK = 256
MOBA_TOPK = 3
MOE_GROUPS = 4
MOE_EPG = 8
MOE_EXPERTS = MOE_GROUPS * MOE_EPG
KV_ROW_HEADS = 16
ROW_W = KV_ROW_HEADS * HEAD_DIM
A_SPLITS = [HG_WIDTH] * 4 + [CM_WIDTH] * 2
C_SPLITS = [NSA_HEADS * HEAD_DIM] + [NSA_KV * HEAD_DIM] * 6 + [NSA_HEADS * 3, MOBA_HEADS * HEAD_DIM,
                                                              MOBA_KV * HEAD_DIM, MOBA_KV * HEAD_DIM]
NEG = -1e30
BIG = 1e9
EPS = 1e-6
KEY_TILE = 256
LANES = 128
_MOBA_SHIFT = MOBA_BLOCK.bit_length() - 1
_SEL_SHIFT = NSA_SEL_BLOCK.bit_length() - 1
VMEM_LIMIT = 56 * 1024 * 1024


def _cp(*sem):
    return pltpu.CompilerParams(dimension_semantics=sem, vmem_limit_bytes=VMEM_LIMIT)


def _sigmoid(x):
    return 1.0 / (1.0 + jnp.exp(-x))


def _silu(x):
    return x * _sigmoid(x)


def _gelu(x):
    return 0.5 * x * (1.0 + jnp.tanh(math.sqrt(2.0 / math.pi) * (x + 0.044715 * (x * x * x))))


def _rms(x, g):
    return x * lax.rsqrt(jnp.mean(x * x, axis=-1, keepdims=True) + EPS) * g


def _dot(a, b):
    return jnp.dot(a, b, preferred_element_type=F32)


def _dot_hi(a, b):
    return jnp.dot(a, b, precision=HI, preferred_element_type=F32)


def _dot3(a, b):
    a_hi = a.astype(BF16)
    b_hi = b.astype(BF16)
    a_lo = (a - a_hi.astype(F32)).astype(BF16)
    b_lo = (b - b_hi.astype(F32)).astype(BF16)
    return _dot(a_hi, b_hi) + (_dot(a_hi, b_lo) + _dot(a_lo, b_hi))


def _full(shape):
    n = len(shape)
    return pl.BlockSpec(shape, lambda *_: (0,) * n)


def _row_tile(t):
    for tm in (256, 128, 64, 32, 16, 8):
        if t % tm == 0:
            return tm
    raise ValueError(f"token count {t} is not a multiple of 8")


def _proj_a_kernel(x_ref, g_ref, w_ref, lb_ref, vg_ref, q_ref, k_ref, lf_ref, iv_ref, sg_ref, gu_ref, vn_ref):
    xn = _rms(x_ref[...], g_ref[...]).astype(BF16)
    z = _dot(xn, w_ref[...])
    w = HG_WIDTH
    q_ref[...] = _silu(z[:, 0:w])
    lb = lb_ref[...]
    f = lb + (1.0 - lb) * _sigmoid(z[:, w:2 * w])
    k_ref[...] = 1.0 - f
    lf_ref[...] = jnp.log(f)
    iv_ref[...] = z[:, 2 * w:3 * w]
    sg_ref[...] = _silu(z[:, 3 * w:4 * w])
    gu_ref[...] = _gelu(z[:, 4 * w:4 * w + CM_WIDTH])
    v = _gelu(z[:, 4 * w + CM_WIDTH:])
    for gi in range(CM_GROUPS):
        sl = slice(gi * CM_GW, (gi + 1) * CM_GW)
        vn_ref[:, sl] = _rms(v[:, sl], vg_ref[:, sl])


def _proj_a(x, g, w_bf, lb512, vgain):
    t, d = x.shape
    tm = _row_tile(t)
    n = w_bf.shape[1]
    row = lambda wd: pl.BlockSpec((tm, wd), lambda i: (i, 0))
    outs = [jax.ShapeDtypeStruct((t, HG_WIDTH), F32)] * 5 + [jax.ShapeDtypeStruct((t, CM_WIDTH), F32)] * 2
    return pl.pallas_call(
        _proj_a_kernel,
        grid=(t // tm,),
        in_specs=[row(d), _full((1, d)), _full((d, n)), _full((1, HG_WIDTH)), _full((1, CM_WIDTH))],
        out_specs=[row(HG_WIDTH)] * 5 + [row(CM_WIDTH)] * 2,
        out_shape=outs,
        compiler_params=_cp("parallel"),
        name="proj_a",
    )(x, g, w_bf, lb512, vgain)


def _hgrn_kernel(q_ref, k_ref, lf_ref, v_ref, sg_ref, s0_ref, gain_ref, bones_ref, bmask_ref, tri_ref,
                 o_ref, sout_ref, st_scr, p_scr, *, chunk):
    c = pl.program_id(1)
    w = HG_WIDTH
    hw = w // 2

    @pl.when(c == 0)
    def _():
        st_scr[...] = s0_ref[0]

    q = q_ref[0]
    k = k_ref[0]
    v = v_ref[0]
    b = _dot_hi(tri_ref[...], lf_ref[0])
    bones = bones_ref[...]

    def head_sum(x):
        xb = x.astype(BF16)
        return jnp.concatenate([_dot(xb[:, :hw], bones), _dot(xb[:, hw:], bones)], axis=1)

    o_rows = []
    for blk in range(chunk // 8):
        s_len = 8 * (blk + 1)
        bs = b[:s_len]
        ks = k[:s_len]
        vs = v[:s_len]
        row_id = lax.broadcasted_iota(jnp.int32, (s_len, w), 0)
        for t in range(8):
            r = 8 * blk + t
            diff = jnp.where(row_id <= r, b[r:r + 1, :] - bs, NEG)
            p_scr[t * s_len:(t + 1) * s_len, :] = jnp.exp(diff) * q[r:r + 1, :] * ks
        att = head_sum(p_scr[0:8 * s_len, :])
        for t in range(8):
            o_rows.append(jnp.sum(att[t * s_len:(t + 1) * s_len] * vs, axis=0, keepdims=True))
    o_intra = jnp.concatenate(o_rows, axis=0)

    st = st_scr[...]
    qe = (q * jnp.exp(b)).astype(BF16)
    o = o_intra + lax.dot_general(qe, st.astype(BF16), (((1,), (1,)), ((), ())), preferred_element_type=F32)

    b_end = b[chunk - 1:chunk, :]
    kd = (k * jnp.exp(b_end - b)).astype(BF16)
    upd = lax.dot_general(v.astype(BF16), kd, (((0,), (0,)), ((), ())), preferred_element_type=F32)
    st_new = st * jnp.exp(b_end) + upd * bmask_ref[...]
    st_scr[...] = st_new

    o2 = o * o
    hi = o2.astype(BF16).astype(F32)
    ms = (head_sum(hi) + head_sum(o2 - hi)) * (1.0 / HG_DK)
    o_ref[0] = o * lax.rsqrt(ms + EPS) * gain_ref[...] * sg_ref[0]

    @pl.when(c == pl.num_programs(1) - 1)
    def _():
        sout_ref[0] = st_new


def _block_ones(n, blk, dtype):
    i = np.arange(n) // blk
    return jnp.asarray((i[:, None] == i[None, :]).astype(np.float32), dtype)


def _hgrn(q, k, lf, v, sg, st0, gain, batch, seq):
    w = HG_WIDTH
    chunk = math.gcd(seq, HG_CHUNK)
    nch = seq // chunk
    r3 = lambda a: a.reshape(batch, seq, w)
    tile = pl.BlockSpec((1, chunk, w), lambda b, c: (b, c, 0))
    state = pl.BlockSpec((1, w, w), lambda b, c: (b, 0, 0))
    tri = jnp.asarray(np.tril(np.ones((chunk, chunk), np.float32)))
    o, st = pl.pallas_call(
        functools.partial(_hgrn_kernel, chunk=chunk),
        grid=(batch, nch),
        in_specs=[tile] * 5 + [state, _full((1, w)), _full((w // 2, w // 2)), _full((w, w)), _full((chunk, chunk))],
        out_specs=[tile, state],
        out_shape=[jax.ShapeDtypeStruct((batch, seq, w), F32), jax.ShapeDtypeStruct((batch, w, w), F32)],
        scratch_shapes=[pltpu.VMEM((w, w), F32), pltpu.VMEM((8 * chunk, w), F32)],
        compiler_params=_cp("parallel", "arbitrary"),
        name="hgrn",
    )(r3(q), r3(k), r3(lf), r3(v), r3(sg), st0, gain, _block_ones(w // 2, HG_DK, BF16),
      _block_ones(w, HG_DK, F32), tri)
    return o.reshape(batch * seq, w), st


def _cmix_kernel(vn_ref, gu_ref, ws_ref, bst_ref, o_ref, *, tl):
    ri = lax.broadcasted_iota(jnp.int32, (tl, tl), 0)
    ci = lax.broadcasted_iota(jnp.int32, (tl, tl), 1)
    for gi in range(CM_GROUPS):
        sl = slice(gi * CM_GW, (gi + 1) * CM_GW)
        wg = jnp.where(ci <= ri, ws_ref[gi, 0:tl, 0:tl], 0.0).astype(BF16)
        mix = _dot(wg, vn_ref[0, :, sl].astype(BF16)) + bst_ref[0:tl, gi:gi + 1]
        o_ref[0, :, sl] = gu_ref[0, :, sl] * mix


def _cmix(vn, gu, ws, bst, batch, seq):
    tl = min(seq, CM_CHUNK)
    assert seq % tl == 0 and tl % 8 == 0
    w = CM_WIDTH
    tile = pl.BlockSpec((1, tl, w), lambda b, c: (b, c, 0))
    out = pl.pallas_call(
        functools.partial(_cmix_kernel, tl=tl),
        grid=(batch, seq // tl),
        in_specs=[tile, tile, _full(ws.shape), _full(bst.shape)],
        out_specs=tile,
        out_shape=jax.ShapeDtypeStruct((batch, seq, w), F32),
        compiler_params=_cp("parallel", "parallel"),
        name="cmix",
    )(vn.reshape(batch, seq, w), gu.reshape(batch, seq, w), ws, bst)
    return out.reshape(batch * seq, w)


def _out_proj_kernel(res_ref, a1_ref, a2_ref, w1_ref, w2_ref, o_ref):
    o_ref[...] = (res_ref[...] + _dot(a1_ref[...].astype(BF16), w1_ref[...])
                  + _dot(a2_ref[...].astype(BF16), w2_ref[...]))


def _out_proj(res, a1, a2, w_bf):
    t, d = res.shape
    k1, k2 = a1.shape[1], a2.shape[1]
    tm = _row_tile(t)
    row = lambda wd: pl.BlockSpec((tm, wd), lambda i: (i, 0))
    return pl.pallas_call(
        _out_proj_kernel,
        grid=(t // tm,),
        in_specs=[row(d), row(k1), row(k2), _full((k1, d)), _full((k2, d))],
        out_specs=row(d),
        out_shape=jax.ShapeDtypeStruct((t, d), F32),
        compiler_params=_cp("parallel"),
        name="out_proj",
    )(res, a1, a2, w_bf[:k1], w_bf[k1:])


def _router_kernel(h_ref, g_ref, w_ref, b_ref, xn_ref, info_ref):
    xn = _rms(h_ref[...], g_ref[...])
    xn_ref[...] = xn.astype(BF16)
    logit = _dot3(xn, w_ref[...]) + b_ref[...]
    lane = lax.broadcasted_iota(jnp.int32, logit.shape, 1)
    far = 1 << 20

    def first_max(vals, mask):
        m = jnp.max(jnp.where(mask, vals, -1.0), axis=-1, keepdims=True)
        idx = jnp.min(jnp.where(mask & (vals == m), lane, far), axis=-1, keepdims=True)
        return m, idx

    def softmax(mask):
        m = jnp.max(jnp.where(mask, logit, NEG), axis=-1, keepdims=True)
        e = jnp.where(mask, jnp.exp(logit - m), 0.0)
        return e / jnp.sum(e, axis=-1, keepdims=True)

    cmask = lane < MOE_GROUPS
    pg, grp = first_max(softmax(cmask), cmask)
    lo = MOE_GROUPS + grp * MOE_EPG
    fmask = (lane >= lo) & (lane < lo + MOE_EPG)
    pf = softmax(fmask)
    v1, i1 = first_max(pf, fmask)
    v2, i2 = first_max(pf, fmask & (lane != i1))
    den = v1 + v2
    info = jnp.where(lane == 0, (i1 - MOE_GROUPS).astype(F32), 0.0)
    info = jnp.where(lane == 1, (i2 - MOE_GROUPS).astype(F32), info)
    info = jnp.where(lane == 2, pg * v1 / den, info)
    info = jnp.where(lane == 3, pg * v2 / den, info)
    info_ref[...] = info


def _router(h, g, w_r, b_r):
    t, d = h.shape
    tm = _row_tile(t)
    row = lambda wd: pl.BlockSpec((tm, wd), lambda i: (i, 0))
    return pl.pallas_call(
        _router_kernel,
        grid=(t // tm,),
        in_specs=[row(d), _full((1, d)), _full((d, LANES)), _full((1, LANES))],
        out_specs=[row(d), row(LANES)],
        out_shape=[jax.ShapeDtypeStruct((t, d), BF16), jax.ShapeDtypeStruct((t, LANES), F32)],
        compiler_params=_cp("parallel"),
        name="router",
    )(h, g, w_r, b_r)


def _moe_ffn_kernel(be_ref, nu_ref, x_ref, w1_ref, w3_ref, w2_ref, o_ref, w1b, w3b, w2b):
    i = pl.program_id(0)

    @pl.when((i == 0) | (be_ref[i] != be_ref[jnp.maximum(i - 1, 0)]))
    def _():
        w1b[...] = w1_ref[0].astype(BF16)
        w3b[...] = w3_ref[0].astype(BF16)
        w2b[...] = w2_ref[0].astype(BF16)

    @pl.when(i < nu_ref[0])
    def _():
        x = x_ref[...]
        hdn = _silu(_dot(x, w1b[...])) * _dot(x, w3b[...])
        o_ref[...] = _dot(hdn.astype(BF16), w2b[...])

    @pl.when(i >= nu_ref[0])
    def _():
        o_ref[...] = jnp.zeros_like(o_ref)


def _moe_ffn(xp, blk_e, nused, w1, w3, w2, rb):
    n, d = xp.shape
    ff = w1.shape[2]
    nblk = n // rb
    gs = pltpu.PrefetchScalarGridSpec(
        num_scalar_prefetch=2,
        grid=(nblk,),
        in_specs=[pl.BlockSpec((rb, d), lambda i, be, nu: (i, 0)),
                  pl.BlockSpec((1, d, ff), lambda i, be, nu: (be[i], 0, 0)),
                  pl.BlockSpec((1, d, ff), lambda i, be, nu: (be[i], 0, 0)),
                  pl.BlockSpec((1, ff, d), lambda i, be, nu: (be[i], 0, 0))],
        out_specs=pl.BlockSpec((rb, d), lambda i, be, nu: (i, 0)),
        scratch_shapes=[pltpu.VMEM((d, ff), BF16), pltpu.VMEM((d, ff), BF16), pltpu.VMEM((ff, d), BF16)],
    )
    return pl.pallas_call(
        _moe_ffn_kernel,
        grid_spec=gs,
        out_shape=jax.ShapeDtypeStruct((n, d), F32),
        compiler_params=_cp("arbitrary"),
        name="moe_ffn",
    )(blk_e, nused, xp, w1, w3, w2)


def _moe(h, g, w_r, b_r, w1, w3, w2):
    t, d = h.shape
    xn, info = _router(h, g, w_r, b_r)
    eid = info[:, 0:2].astype(jnp.int32)
    gate = info[:, 2:4]
    ne = MOE_EXPERTS
    tk = 2 * t
    rb = 256 if t >= 4096 else 32
    flat = eid.reshape(-1)
    order = jnp.argsort(flat).astype(jnp.int32)
    rank = jnp.argsort(order).astype(jnp.int32)
    counts = jnp.sum((flat[:, None] == jnp.arange(ne)[None, :]).astype(jnp.int32), axis=0)
    padc = (counts + rb - 1) // rb * rb
    pend = jnp.cumsum(padc)
    pstart = pend - padc
    cstart = jnp.cumsum(counts) - counts
    slot = (pstart[flat] + rank - cstart[flat]).reshape(t, 2)
    nblk = -(-tk // rb) + ne
    blk_e = jnp.minimum(jnp.sum((pend[None, :] <= (jnp.arange(nblk) * rb)[:, None]).astype(jnp.int32), axis=1),
                        ne - 1).astype(jnp.int32)
    pe = jnp.repeat(blk_e, rb)
    off = jnp.arange(nblk * rb) - pstart[pe]
    rows = jnp.where(off < counts[pe], order[jnp.clip(off + cstart[pe], 0, tk - 1)] // 2, 0)
    xp = xn[rows]
    nused = (pend[-1:] // rb).astype(jnp.int32)
    yp = _moe_ffn(xp, blk_e, nused, w1, w3, w2, rb)
    return gate[:, 0:1] * yp[slot[:, 0]] + gate[:, 1:2] * yp[slot[:, 1]]


def _ple_kernel(h_ref, y_ref, p_ref, g_ref, wg_ref, wp_ref, fg_ref, o_ref, *, final):
    h = h_ref[...] + y_ref[...]
    gate = _sigmoid(_dot(_rms(h, g_ref[...]).astype(BF16), wg_ref[...]))
    out = h + gate * _dot(p_ref[...].astype(BF16), wp_ref[...])
    if final:
        out = _rms(out, fg_ref[...])
    o_ref[...] = out


def _ple(h, y, p, g, wg_bf, wp_bf, fg, final):
    t, d = h.shape
    pd = p.shape[1]
    tm = _row_tile(t)
    row = lambda wd: pl.BlockSpec((tm, wd), lambda i: (i, 0))
    return pl.pallas_call(
        functools.partial(_ple_kernel, final=final),
        grid=(t // tm,),
        in_specs=[row(d), row(d), row(pd), _full((1, d)), _full((d, d)), _full((pd, d)), _full((1, d))],
        out_specs=row(d),
        out_shape=jax.ShapeDtypeStruct((t, d), F32),
        compiler_params=_cp("parallel"),
        name="ple",
    )(h, y, p, g, wg_bf, wp_bf, fg)


_HD = HEAD_DIM
_C_NQ = NSA_HEADS * _HD
_C_KV = NSA_KV * _HD
_C_MQ = MOBA_HEADS * _HD
_C_MK = MOBA_KV * _HD
_O_NQ = 0
_O_CK = _O_NQ + _C_NQ
_O_CV = _O_CK + _C_KV
_O_SK = _O_CV + _C_KV
_O_SV = _O_SK + _C_KV
_O_WK = _O_SV + _C_KV
_O_WV = _O_WK + _C_KV
_O_MQ = _O_WV + _C_KV
_O_MK = _O_MQ + _C_MQ
_O_MV = _O_MK + _C_MK
_O_NG = _O_MV + _C_MK
_O_NQS = _O_NG + LANES
_O_SKS = _O_NQS + _C_NQ
_O_WKS = _O_SKS + _C_KV
_O_MQS = _O_WKS + _C_KV
_O_MKS = _O_MQS + _C_MQ
_C_TOTAL = _O_MKS + _C_MK


def _proj_c_kernel(x_ref, g_ref, w_ref, cos_ref, sin_ref, qu_ref, qr_ref, mq_ref, ng_ref, rows_ref, win_ref,
                   *flat_refs):
    xn = _rms(x_ref[...], g_ref[...]).astype(BF16)
    z = _dot(xn, w_ref[...])
    cs = cos_ref[...]
    sn = sin_ref[...]

    def rope(o, os, wd):
        reps = wd // LANES
        return z[:, o:o + wd] * jnp.tile(cs, (1, reps)) + z[:, os:os + wd] * jnp.tile(sn, (1, reps))

    qu_ref[...] = z[:, _O_NQ:_O_NQ + _C_NQ]
    qr_ref[...] = rope(_O_NQ, _O_NQS, _C_NQ)
    mq_ref[...] = rope(_O_MQ, _O_MQS, _C_MQ)
    ng_ref[...] = z[:, _O_NG:_O_NG + LANES]
    rows_ref[:, 0:2 * _C_KV] = z[:, _O_CK:_O_CK + 2 * _C_KV]
    rows_ref[:, 2 * _C_KV:3 * _C_KV] = rope(_O_SK, _O_SKS, _C_KV)
    rows_ref[:, 3 * _C_KV:4 * _C_KV] = z[:, _O_SV:_O_SV + _C_KV]
    rows_ref[:, 4 * _C_KV:4 * _C_KV + _C_MK] = rope(_O_MK, _O_MKS, _C_MK)
    rows_ref[:, 4 * _C_KV + _C_MK:] = z[:, _O_MV:_O_MV + _C_MK]
    win_ref[:, 0:_C_KV] = rope(_O_WK, _O_WKS, _C_KV)
    win_ref[:, _C_KV:] = z[:, _O_WV:_O_WV + _C_KV]
    if not flat_refs:
        return
    flat_ref, rows_t_ref, ck_scr, cv_scr = flat_refs
    rows_t_ref[0] = rows_ref[...].T
    ck_scr[...] = z[:, _O_CK:_O_CK + _C_KV]
    cv_scr[...] = z[:, _O_CV:_O_CV + _C_KV]
    nseg = flat_ref.shape[0]
    for i in range(NSA_CMP_STRIDE):
        c0 = i * 2 * _C_KV
        flat_ref[:, c0:c0 + _C_KV] = ck_scr[pl.ds(i, nseg, stride=NSA_CMP_STRIDE), :].astype(BF16)
        flat_ref[:, c0 + _C_KV:c0 + 2 * _C_KV] = cv_scr[pl.ds(i, nseg, stride=NSA_CMP_STRIDE), :].astype(BF16)


def _proj_c_weights(w):
    offs = np.concatenate([[0], np.cumsum(C_SPLITS)])
    nq, ck, cv, sk, sv, wk, wv, ng, mq, mk, mv = [w[:, offs[i]:offs[i + 1]] for i in range(len(C_SPLITS))]
    half = ROT_DIM // 2

    def swapped(m):
        d = m.shape[0]
        m3 = m.reshape(d, -1, _HD)
        out = jnp.concatenate([m3[..., half:ROT_DIM], m3[..., :half], jnp.zeros_like(m3[..., ROT_DIM:])], axis=-1)
        return out.reshape(d, -1)

    ngp = jnp.pad(ng, ((0, 0), (0, LANES - ng.shape[1])))
    return jnp.concatenate([nq, ck, cv, sk, sv, wk, wv, mq, mk, mv, ngp,
                            swapped(nq), swapped(sk), swapped(wk), swapped(mq), swapped(mk)], axis=1).astype(BF16)


def _rope_tables(pos):
    half = ROT_DIM // 2
    inv = ROPE_THETA ** (-jnp.arange(half, dtype=F32) / half)
    ang = pos.astype(F32)[:, None] * inv
    cos, sin = jnp.cos(ang), jnp.sin(ang)
    n = pos.shape[0]
    c64 = jnp.concatenate([cos, cos, jnp.ones((n, _HD - ROT_DIM), F32)], axis=1)
    s64 = jnp.concatenate([-sin, sin, jnp.zeros((n, _HD - ROT_DIM), F32)], axis=1)
    return jnp.tile(c64, (1, LANES // _HD)), jnp.tile(s64, (1, LANES // _HD))


def _proj_c(x, g, wc_bf, pos, seq, want_flat):
    t, d = x.shape
    tm = _row_tile(t)
    tr = max(seq, tm)
    assert tr % tm == 0 and tr % seq == 0
    cs, sn = _rope_tables(jnp.tile(pos, tr // seq))
    ntab = tr // tm
    row = lambda wd: pl.BlockSpec((tm, wd), lambda i: (i, 0))
    tab = pl.BlockSpec((tm, LANES), lambda i: (i % ntab, 0))
    widths = [_C_NQ, _C_NQ, _C_MQ, LANES, ROW_W, 2 * _C_KV]
    st = NSA_CMP_STRIDE
    fw = st * 2 * _C_KV
    out_specs = [row(wd) for wd in widths]
    out_shape = [jax.ShapeDtypeStruct((t, wd), F32) for wd in widths]
    scratch = []
    if want_flat:
        assert tm % (16 * st) == 0 and seq % tm == 0
        tpb = seq // tm
        out_specs.append(pl.BlockSpec((tm // st, fw), lambda i: (i, 0)))
        out_shape.append(jax.ShapeDtypeStruct((t // st, fw), BF16))
        out_specs.append(pl.BlockSpec((1, ROW_W, tm), lambda i: (i // tpb, 0, i % tpb)))
        out_shape.append(jax.ShapeDtypeStruct((t // seq, ROW_W, seq), F32))
        scratch = [pltpu.VMEM((tm, _C_KV), F32), pltpu.VMEM((tm, _C_KV), F32)]
    return pl.pallas_call(
        _proj_c_kernel,
        grid=(t // tm,),
        in_specs=[row(d), _full((1, d)), _full((d, _C_TOTAL)), tab, tab],
        out_specs=out_specs,
        out_shape=out_shape,
        scratch_shapes=scratch,
        compiler_params=_cp("parallel"),
        name="proj_c",
    )(x, g, wc_bf, cs, sn)


def _mm_kernel(a_ref, w_ref, o_ref):
    o_ref[...] = _dot(a_ref[...], w_ref[...])


def _mm(a_bf, w_bf):
    m, k = a_bf.shape
    n = w_bf.shape[1]
    tm = _row_tile(m)
    return pl.pallas_call(
        _mm_kernel,
        grid=(m // tm,),
        in_specs=[pl.BlockSpec((tm, k), lambda i: (i, 0)), _full((k, n))],
        out_specs=pl.BlockSpec((tm, n), lambda i: (i, 0)),
        out_shape=jax.ShapeDtypeStruct((m, n), F32),
        compiler_params=_cp("parallel"),
        name="mm",
    )(a_bf, w_bf)


def _cmp_fin_kernel(a_ref, pb_ref, w2_ref, o_ref, *, nseg):
    a = a_ref[0]
    hw = a.shape[1] // 2
    bias = pb_ref[0:1, :hw] + pb_ref[1:2, hw:]
    pre = a[:, :hw] + pltpu.roll(a[:, hw:], nseg - 1, axis=0) + bias
    o_ref[0] = _dot(_gelu(pre).astype(BF16), w2_ref[...])


def _compress_weights(w1k, w1v, w2k, w2v, pek, pev):
    span = NSA_CMP_LEN // NSA_CMP_STRIDE
    st = NSA_CMP_STRIDE
    hid = w1k.shape[1]
    slots = 2 * NSA_KV
    eye = jnp.eye(slots, dtype=F32)
    per_slot = lambda k, v: jnp.stack([k] * NSA_KV + [v] * NSA_KV)
    w1 = per_slot(w1k.reshape(span, st, _HD, hid), w1v.reshape(span, st, _HD, hid))
    wbig = jnp.einsum('smide,st->isdmte', w1, eye)
    pe = per_slot(pek.reshape(span, st, _HD), pev.reshape(span, st, _HD))
    pex = jnp.pad(pe.transpose(1, 2, 0, 3), ((0, 8 - span), (0, 0), (0, 0), (0, 0)))
    w2big = jnp.einsum('sed,st->setd', per_slot(w2k, w2v), eye)
    return (wbig.reshape(st * slots * _HD, span * slots * hid).astype(BF16),
            pex.reshape(8, st * slots * _HD).astype(BF16),
            w2big.reshape(slots * hid, slots * _HD).astype(BF16))


def _compress(cmpflat_bf, wbig, pex, w2big):
    batch, nseg, kdim = cmpflat_bf.shape
    a = _mm(cmpflat_bf.reshape(batch * nseg, kdim), wbig).reshape(batch, nseg, -1)
    pb = _mm(pex, wbig)
    n2 = a.shape[2]
    ow = w2big.shape[1]
    return pl.pallas_call(
        functools.partial(_cmp_fin_kernel, nseg=nseg),
        grid=(batch,),
        in_specs=[pl.BlockSpec((1, nseg, n2), lambda b: (b, 0, 0)), _full(pb.shape), _full(w2big.shape)],
        out_specs=pl.BlockSpec((1, nseg, ow), lambda b: (b, 0, 0)),
        out_shape=jax.ShapeDtypeStruct((batch, nseg, ow), F32),
        compiler_params=_cp("parallel"),
        name="compress_fin",
    )(a, pb, w2big)


def _block_mean_kernel(k_ref, o_ref):
    o_ref[0, 0] = jnp.sum(k_ref[0], axis=0, keepdims=True) * (1.0 / MOBA_BLOCK)


def _block_mean(rows3, nbf, col_block):
    batch = rows3.shape[0]
    wd = _C_MK
    return pl.pallas_call(
        _block_mean_kernel,
        grid=(batch, nbf),
        in_specs=[pl.BlockSpec((1, MOBA_BLOCK, wd), lambda b, n: (b, n, col_block))],
        out_specs=pl.BlockSpec((1, 1, 1, wd), lambda b, n: (b, n, 0, 0)),
        out_shape=jax.ShapeDtypeStruct((batch, nbf, 1, wd), F32),
        compiler_params=_cp("parallel", "parallel"),
        name="block_mean",
    )(rows3)


def _top_rounds(score, row_id, k):
    far = 1 << 20
    sel = jnp.zeros(score.shape, F32)
    cur = score
    for _ in range(k):
        m = jnp.max(cur, axis=0, keepdims=True)
        idx = jnp.min(jnp.where(cur == m, row_id, far), axis=0, keepdims=True)
        pick = row_id == idx
        sel = jnp.where(pick, 1.0, sel)
        cur = jnp.where(pick, -jnp.inf, cur)
    return sel


def _softmax0(s, mask):
    s = jnp.where(mask, s, NEG)
    e = jnp.where(mask, jnp.exp(s - jnp.max(s, axis=0, keepdims=True)), 0.0)
    return e / jnp.maximum(jnp.sum(e, axis=0, keepdims=True), 1e-30)


def _tdot(a, b):
    return lax.dot_general(a, b, (((0,), (0,)), ((), ())), preferred_element_type=F32)


def _nsa_front(qu, kc, vc, mt, tpos, tq, nc, n_sel):
    nc_pad = kc.shape[0]
    nsel_pad = mt.shape[0]
    ci = lax.broadcasted_iota(jnp.int32, (nc_pad, 1), 0)
    bi = lax.broadcasted_iota(jnp.int32, (nsel_pad, tq), 0)
    cur = tpos[:, :tq] >> _SEL_SHIFT
    cmask = ((ci * NSA_CMP_STRIDE + (NSA_CMP_LEN - 1)) <= tpos) & (ci < nc)
    pc = _softmax0(_dot_hi(kc, qu), cmask)
    o_c = _tdot(vc.astype(BF16), pc.astype(BF16))
    imp = pc[:, 0:tq]
    for r in range(1, NSA_GRP):
        imp = imp + pc[:, r * tq:(r + 1) * tq]
    score = _dot_hi(mt, imp)
    forced = (bi == 0) | (bi == cur) | (bi == cur - 1)
    score = jnp.where(forced, BIG, score)
    score = jnp.where(bi <= cur, score, -BIG)
    score = jnp.where(bi < n_sel, score, -3e38)
    sel = _top_rounds(score, bi, min(NSA_TOPN, n_sel))
    return o_c, jnp.where(bi <= cur, sel, 0.0)


def _nsa_window(qr, kw, vw, tpos, wpos):
    wmask = (wpos <= tpos) & (wpos >= tpos - NSA_WINDOW) & (wpos >= 0)
    pw = _softmax0(_dot(kw.astype(BF16), qr), wmask).astype(BF16)
    return _tdot(vw.astype(BF16), pw)


def _online_softmax(m_ref, l_ref, acc_ref, s, mask, pv, guard=True):
    s = jnp.where(mask, s, NEG)
    m_old = m_ref[...]
    m_new = jnp.maximum(m_old, jnp.max(s, axis=0, keepdims=True))
    alpha = jnp.exp(m_old - m_new)
    p = jnp.exp(s - m_new)
    if guard:
        p = jnp.where(mask, p, 0.0)
    l_ref[...] = alpha * l_ref[...] + jnp.sum(p, axis=0, keepdims=True)
    acc_ref[...] = alpha * acc_ref[...] + pv(p.astype(BF16))
    m_ref[...] = m_new


def _heads_t(x, tq, groups, rep):
    heads = []
    for p in range(groups * rep // 2):
        xt = x[:, p * LANES:(p + 1) * LANES].T
        heads += [xt[0:_HD], xt[_HD:2 * _HD]]
    return [jnp.concatenate(heads[g * rep:(g + 1) * rep], axis=1) for g in range(groups)]


def _heads_untranspose(o_ref, per_group, tq, rep):
    heads = [og[:, r * tq:(r + 1) * tq] for og in per_group for r in range(rep)]
    for p in range(len(heads) // 2):
        o_ref[:, p * LANES:(p + 1) * LANES] = jnp.concatenate([heads[2 * p], heads[2 * p + 1]], axis=0).T


def _block_diag(qs):
    z = jnp.zeros_like(qs[0])
    return jnp.concatenate([jnp.concatenate([q if h == g else z for h in range(len(qs))], axis=1)
                            for g, q in enumerate(qs)], axis=0)


def _nsa_kernel(qu_ref, qr_ref, ng_ref, kcvc_ref, mt_ref, ksv_ref, win_ref,
                o_ref, sel_scr, m_scr, l_scr, acc_scr, *, tq, nc, n_sel, nkt_max, wl):
    qi = pl.program_id(1)
    n = NSA_GRP * tq
    tk = KEY_TILE
    kw_n = NSA_KV * _HD
    scale = _HD ** -0.5
    t0 = qi * tq
    lane = lax.broadcasted_iota(jnp.int32, (1, n), 1)
    tpos = t0 + (lane & (tq - 1))
    nkt = jnp.minimum((t0 + tq - 1) // tk + 1, nkt_max)
    wstart = pl.multiple_of(jnp.maximum(t0 - NSA_WINDOW, 0), LANES)
    wpos = wstart + lax.broadcasted_iota(jnp.int32, (wl, 1), 0)

    qus = _heads_t(qu_ref[...] * scale, tq, NSA_KV, NSA_GRP)
    qrs = [q.astype(BF16) for q in _heads_t(qr_ref[...] * scale, tq, NSA_KV, NSA_GRP)]
    gl_t = _sigmoid(ng_ref[...].T)
    gates = [[jnp.concatenate([gl_t[(g * NSA_GRP + r) * 3 + j:(g * NSA_GRP + r) * 3 + j + 1] for r in range(NSA_GRP)],
                              axis=1) for j in range(3)] for g in range(NSA_KV)]
    kcvc = kcvc_ref[0]
    win = win_ref[0, pl.ds(wstart, wl), :]

    o_cw = []
    for g in range(NSA_KV):
        hs = slice(g * _HD, (g + 1) * _HD)
        o_c, sel = _nsa_front(qus[g], kcvc[:, hs], kcvc[:, kw_n + g * _HD:kw_n + (g + 1) * _HD], mt_ref[...],
                              tpos, tq, nc, n_sel)
        sel_scr[g] = sel
        o_w = _nsa_window(qrs[g], win[:, hs], win[:, kw_n + g * _HD:kw_n + (g + 1) * _HD], tpos, wpos)
        o_cw.append(gates[g][0] * o_c + gates[g][2] * o_w)
    m_scr[...] = jnp.full(m_scr.shape, NEG, F32)
    l_scr[...] = jnp.zeros(l_scr.shape, F32)
    acc_scr[...] = jnp.zeros(acc_scr.shape, F32)
    qbd = _block_diag(qrs)

    def tile(j, causal):
        nb = tk // NSA_SEL_BLOCK
        rows = pl.ds(pl.multiple_of(j * tk, tk), tk)
        ksv = ksv_ref[0, rows, :]
        masks = []
        for g in range(NSA_KV):
            pieces = [jnp.broadcast_to(sel_scr[g, pl.ds(j * nb + u, 1), :], (NSA_SEL_BLOCK, tq)) for u in range(nb)]
            mk = jnp.concatenate(pieces, axis=0)
            masks.append(jnp.concatenate([mk] * NSA_GRP, axis=1) > 0.5)
        mask = jnp.concatenate(masks, axis=1)
        if causal:
            kpos = j * tk + lax.broadcasted_iota(jnp.int32, (tk, 1), 0)
            mask = mask & jnp.concatenate([kpos <= tpos] * NSA_KV, axis=1)
        vb = ksv[:, kw_n:].astype(BF16)
        _online_softmax(m_scr, l_scr, acc_scr, _dot(ksv[:, :kw_n].astype(BF16), qbd), mask,
                        lambda p: _tdot(vb, p), guard=False)

    def body(j, carry):
        tile(j, False)
        return carry

    lax.fori_loop(0, nkt - 1, body, 0)
    tile(nkt - 1, True)
    o_s = acc_scr[...] / jnp.maximum(l_scr[...], 1e-30)
    outs = [o_cw[g] + gates[g][1] * o_s[g * _HD:(g + 1) * _HD, g * n:(g + 1) * n] for g in range(NSA_KV)]
    _heads_untranspose(o_ref, outs, tq, NSA_GRP)


def _nsa_select_kernel(qu_ref, qr_ref, gl_ref, kc_ref, vc_ref, mt_ref, kw_ref, vw_ref,
                       o_ref, g1_ref, sel_ref, *, tq, nc, n_sel, qpos0, wpos0):
    n = NSA_GRP * tq
    lane = lax.broadcasted_iota(jnp.int32, (1, n), 1)
    tpos = qpos0 + (lane & (tq - 1))
    wpos = wpos0 + lax.broadcasted_iota(jnp.int32, (kw_ref.shape[2], 1), 0)
    for g in range(NSA_KV):
        qr = qr_ref[0, 0, g].astype(BF16)
        o_c, sel = _nsa_front(qu_ref[0, 0, g], kc_ref[0, g], vc_ref[0, g], mt_ref[...], tpos, tq, nc, n_sel)
        sel_ref[0, g] = sel
        o_w = _nsa_window(qr, kw_ref[0, g], vw_ref[0, g], tpos, wpos)
        gate = _sigmoid(gl_ref[0, 0, g])
        o_ref[0, g] = gate[0:1] * o_c + gate[2:3] * o_w
        g1_ref[0, g] = gate[1:2]


def _moba_select(q, kb, own, nbf):
    ni = lax.broadcasted_iota(jnp.int32, (kb.shape[0], q.shape[1]), 0)
    gs = _dot_hi(kb, q)
    gs = jnp.where(ni < own, gs, -BIG)
    gs = jnp.where(ni < nbf, gs, -3e38)
    sel = _top_rounds(gs, ni, min(MOBA_TOPK, nbf))
    return jnp.where((ni < own) & (ni < nbf), sel, 0.0)


def _moba_select_kernel(q_ref, kb_ref, sel_ref, *, own, nbf):
    for g in range(MOBA_KV):
        sel_ref[0, g] = _moba_select(q_ref[0, 0, g], kb_ref[0, g], own, nbf)


def _moba_kernel(q_ref, kb_ref, kmv_ref, o_ref, sel_scr, m_scr, l_scr, acc_scr, *, tq, nbf, nkt_max):
    qi = pl.program_id(1)
    n = MOBA_GRP * tq
    tk = KEY_TILE
    assert tk == MOBA_BLOCK
    t0 = qi * tq
    lane = lax.broadcasted_iota(jnp.int32, (1, MOBA_KV * n), 1)
    tpos = t0 + (lane & (tq - 1))
    own = t0 // MOBA_BLOCK
    scale = _HD ** -0.5

    qs = _heads_t(q_ref[...], tq, MOBA_KV, MOBA_GRP)
    for g in range(MOBA_KV):
        sel_scr[g] = _moba_select(qs[g], kb_ref[0, g], own, nbf)
    qbd = _block_diag([(q * scale).astype(BF16) for q in qs])
    m_scr[...] = jnp.full(m_scr.shape, NEG, F32)
    l_scr[...] = jnp.zeros(l_scr.shape, F32)
    acc_scr[...] = jnp.zeros(acc_scr.shape, F32)

    def tile(j, mask):
        kmv = kmv_ref[0, pl.ds(pl.multiple_of(j * tk, tk), tk), :]
        vb = kmv[:, _C_MK:].astype(BF16)
        _online_softmax(m_scr, l_scr, acc_scr, _dot(kmv[:, :_C_MK].astype(BF16), qbd), mask,
                        lambda p: _tdot(vb, p), guard=False)

    kpos = own * tk + lax.broadcasted_iota(jnp.int32, (tk, 1), 0)
    tile(own, kpos <= tpos)

    def body(j, carry):
        picked = jnp.concatenate([jnp.broadcast_to(sel_scr[g, pl.ds(j, 1), :], (tk, n)) for g in range(MOBA_KV)],
                                 axis=1)
        tile(j, picked > 0.5)
        return carry

    lax.fori_loop(0, jnp.minimum(own, nkt_max), body, 0)
    o = acc_scr[...] / jnp.maximum(l_scr[...], 1e-30)
    _heads_untranspose(o_ref, [o[g * _HD:(g + 1) * _HD, g * n:(g + 1) * n] for g in range(MOBA_KV)], tq, MOBA_GRP)


def _to_qt(x, batch, nt, tq, groups, rep):
    x = x.reshape(batch, nt, tq, groups, rep, _HD).transpose(0, 1, 3, 5, 4, 2)
    return x.reshape(batch, nt, groups, _HD, rep * tq)


def _from_qt(x, batch, nt, tq, groups, rep):
    x = x.reshape(batch, nt, groups, _HD, rep, tq).transpose(0, 1, 5, 2, 4, 3)
    return x.reshape(batch, nt * tq, groups * rep * _HD)


def _head_major(x, heads):
    b, l, _ = x.shape
    return x.reshape(b, l, heads, _HD).transpose(0, 2, 1, 3)


def _vt_tiles(x, heads, tile):
    b, l, _ = x.shape
    return x.reshape(b, l // tile, tile, heads, _HD).transpose(0, 3, 1, 4, 2)


def _sel_score_matrix(nsel_pad, nc_pad, nc):
    ratio = NSA_SEL_BLOCK // NSA_CMP_STRIDE
    span = NSA_CMP_LEN // NSA_CMP_STRIDE
    mt = np.zeros((nsel_pad, nc_pad), np.float32)
    for j in range(nsel_pad):
        for o in range(-(span - 1), ratio):
            c = j * ratio + o
            wt = sum(1 for m in range(ratio) for q in range(span) if m - q == o)
            if 0 <= c < nc:
                mt[j, c] = wt
    return jnp.asarray(mt)


def _mixer_c_attention(qu, qr, mq, ng, rows3, win3, kcvc, batch, seq):
    tk = KEY_TILE
    tq = LANES
    assert seq % tk == 0 and seq >= NSA_WINDOW + tq
    nt = seq // tq
    nkt_max = seq // tk
    nc = (seq - NSA_CMP_LEN) // NSA_CMP_STRIDE + 1
    nseg = kcvc.shape[1]
    n_sel = seq // NSA_SEL_BLOCK
    nsel_pad = -(-n_sel // 8) * 8
    nbf = seq // MOBA_BLOCK
    nbf_pad = -(-nbf // 8) * 8
    wl = NSA_WINDOW + tq
    mt = _sel_score_matrix(nsel_pad, nseg, nc)
    kw_n = NSA_KV * _HD

    kb = _block_mean(rows3, nbf, col_block=2)[:, :, 0, :]
    kb = _head_major(jnp.pad(kb, ((0, 0), (0, nbf_pad - nbf), (0, 0))), MOBA_KV)

    n_n = NSA_GRP * tq
    n_m = MOBA_GRP * tq
    tile = lambda wd: pl.BlockSpec((tq, wd), lambda b, i: (b * nt + i, 0))
    per_b = lambda shp, cb=0: pl.BlockSpec((1,) + shp, lambda b, i: (b,) + (0,) * (len(shp) - 1) + (cb,))
    o_n = pl.pallas_call(
        functools.partial(_nsa_kernel, tq=tq, nc=nc, n_sel=n_sel, nkt_max=nkt_max, wl=wl),
        grid=(batch, nt),
        in_specs=[tile(NSA_HEADS * _HD), tile(NSA_HEADS * _HD), tile(LANES), per_b((nseg, 2 * kw_n)), _full(mt.shape),
                  per_b((seq, 2 * kw_n), 1), per_b((seq, 2 * kw_n))],
        out_specs=tile(NSA_HEADS * _HD),
        out_shape=jax.ShapeDtypeStruct((batch * seq, NSA_HEADS * _HD), F32),
        scratch_shapes=[pltpu.VMEM((NSA_KV, nsel_pad, tq), F32), pltpu.VMEM((1, NSA_KV * n_n), F32),
                        pltpu.VMEM((1, NSA_KV * n_n), F32), pltpu.VMEM((kw_n, NSA_KV * n_n), F32)],
        compiler_params=_cp("parallel", "arbitrary"),
        name="nsa",
    )(qu, qr, ng, kcvc, mt, rows3, win3)

    o_m = pl.pallas_call(
        functools.partial(_moba_kernel, tq=tq, nbf=nbf, nkt_max=nkt_max),
        grid=(batch, nt),
        in_specs=[tile(MOBA_HEADS * _HD), per_b(kb.shape[1:]), per_b((seq, 2 * _C_MK), 1)],
        out_specs=tile(MOBA_HEADS * _HD),
        out_shape=jax.ShapeDtypeStruct((batch * seq, MOBA_HEADS * _HD), F32),
        scratch_shapes=[pltpu.VMEM((MOBA_KV, nbf_pad, n_m), F32), pltpu.VMEM((1, MOBA_KV * n_m), F32),
                        pltpu.VMEM((1, MOBA_KV * n_m), F32), pltpu.VMEM((_C_MK, MOBA_KV * n_m), F32)],
        compiler_params=_cp("parallel", "arbitrary"),
        name="moba",
    )(mq, kb, rows3)
    return o_n, o_m


PAGES_PER_STEP = 4
_SEG_PER_PAGE = PAGE_SIZE // NSA_CMP_STRIDE


def _pool_view(pool):
    n_pool, n_layers = pool.shape[0], pool.shape[1]
    return jnp.transpose(pool, (0, 1, 3, 4, 2)).reshape(n_pool * n_layers, ROW_W, PAGE_SIZE)


def _page_specs(npages, n_layers, layer, rows, row_block):
    def spec(pg):
        return pl.BlockSpec(
            (1, rows, PAGE_SIZE),
            lambda b, j, p: (p[b * npages + j * PAGES_PER_STEP + pg] * n_layers + layer, row_block, 0))
    return [spec(pg) for pg in range(PAGES_PER_STEP)]


def _scan_kernel(pt_ref, *refs):
    pps = PAGES_PER_STEP
    ck_refs, cv_refs, mk_refs = refs[0:pps], refs[pps:2 * pps], refs[2 * pps:3 * pps]
    flat_ref, kbar_ref, ck_scr, cv_scr = refs[3 * pps:]
    wd = 2 * _C_KV
    for pr in range(pps // 2):
        for q in range(2):
            ck_scr[q * PAGE_SIZE:(q + 1) * PAGE_SIZE, :] = ck_refs[2 * pr + q][0].T
            cv_scr[q * PAGE_SIZE:(q + 1) * PAGE_SIZE, :] = cv_refs[2 * pr + q][0].T
        r0 = 2 * pr * _SEG_PER_PAGE
        for i in range(NSA_CMP_STRIDE):
            c0 = i * wd
            flat_ref[0, r0:r0 + 2 * _SEG_PER_PAGE, c0:c0 + _C_KV] = \
                ck_scr[pl.ds(i, 2 * _SEG_PER_PAGE, stride=NSA_CMP_STRIDE), :].astype(BF16)
            flat_ref[0, r0:r0 + 2 * _SEG_PER_PAGE, c0 + _C_KV:c0 + wd] = \
                cv_scr[pl.ds(i, 2 * _SEG_PER_PAGE, stride=NSA_CMP_STRIDE), :].astype(BF16)
        ksum = (jnp.sum(mk_refs[2 * pr][0].T, axis=0, keepdims=True)
                + jnp.sum(mk_refs[2 * pr + 1][0].T, axis=0, keepdims=True))
        kbar_ref[0, pr] = ksum * (1.0 / MOBA_BLOCK)


def _scan_pages(poolv, pt, layer, n_layers):
    batch, npages = pt.shape
    assert npages % PAGES_PER_STEP == 0 and MOBA_BLOCK == 2 * PAGE_SIZE
    ng = npages // PAGES_PER_STEP
    nseg = npages * _SEG_PER_PAGE
    nbf = npages // 2
    wd = 2 * _C_KV
    gs = pltpu.PrefetchScalarGridSpec(
        num_scalar_prefetch=1,
        grid=(batch, ng),
        in_specs=(_page_specs(npages, n_layers, layer, _C_KV, 0) + _page_specs(npages, n_layers, layer, _C_KV, 1)
                  + _page_specs(npages, n_layers, layer, _C_MK, 2)),
        out_specs=[pl.BlockSpec((1, PAGES_PER_STEP * _SEG_PER_PAGE, NSA_CMP_STRIDE * wd), lambda b, j, p: (b, j, 0)),
                   pl.BlockSpec((1, PAGES_PER_STEP // 2, 1, _C_MK), lambda b, j, p: (b, j, 0, 0))],
        scratch_shapes=[pltpu.VMEM((2 * PAGE_SIZE, _C_KV), F32), pltpu.VMEM((2 * PAGE_SIZE, _C_KV), F32)],
    )
    flat, kbar = pl.pallas_call(
        _scan_kernel,
        grid_spec=gs,
        out_shape=[jax.ShapeDtypeStruct((batch, nseg, NSA_CMP_STRIDE * wd), BF16),
                   jax.ShapeDtypeStruct((batch, nbf, 1, _C_MK), F32)],
        compiler_params=_cp("parallel", "arbitrary"),
        name="scan_pages",
    )(pt.reshape(-1), *([poolv] * (3 * PAGES_PER_STEP)))
    return flat, kbar[:, :, 0, :]


def _flash_update(m_ref, l_ref, acc_ref, k, v, q, mask):
    vb = v.astype(BF16)
    _online_softmax(m_ref, l_ref, acc_ref, _dot(k.astype(BF16), q), mask, lambda p: _tdot(vb, p))


def _dec_attn_kernel(pt_ref, *refs, past, nsteps, seq_pad):
    pps = PAGES_PER_STEP
    sel_refs = refs[0:pps]
    mob_refs = refs[pps:2 * pps]
    (qn_ref, qm_ref, smask_ref, mmask_ref, nsel_ref, nmob_ref, opart_ref, g1_ref,
     on_ref, om_ref, mn, ln, accn, mm, lm, accm) = refs[2 * pps:]
    j = pl.program_id(1)
    kw_n = NSA_KV * _HD
    kw_m = _C_MK

    @pl.when(j == 0)
    def _():
        mn[...] = jnp.full(mn.shape, NEG, F32)
        mm[...] = jnp.full(mm.shape, NEG, F32)
        ln[...] = jnp.zeros(ln.shape, F32)
        lm[...] = jnp.zeros(lm.shape, F32)
        accn[...] = jnp.zeros(accn.shape, F32)
        accm[...] = jnp.zeros(accm.shape, F32)

    qn = qn_ref[0]
    qm = qm_ref[0]
    rows = pps * PAGE_SIZE
    ksv = jnp.concatenate([r[0].T for r in sel_refs], axis=0)
    nb = rows // NSA_SEL_BLOCK
    srow = smask_ref[0, pl.ds(pl.multiple_of(j * nb, nb), nb), :]
    smask = jnp.concatenate([jnp.broadcast_to(srow[u:u + 1], (NSA_SEL_BLOCK, LANES)) for u in range(nb)], axis=0)
    _flash_update(mn, ln, accn, ksv[:, :kw_n], ksv[:, kw_n:], qn, smask > 0.5)

    kmv = jnp.concatenate([r[0].T for r in mob_refs], axis=0)
    nbm = rows // MOBA_BLOCK
    mmask = jnp.concatenate([jnp.broadcast_to(mmask_ref[0, pl.ds(j * nbm + u, 1), :], (MOBA_BLOCK, LANES))
                             for u in range(nbm)], axis=0)
    _flash_update(mm, lm, accm, kmv[:, :kw_m], kmv[:, kw_m:], qm, mmask > 0.5)

    @pl.when(j == nsteps - 1)
    def _():
        lane = lax.broadcasted_iota(jnp.int32, (1, LANES), 1)
        tpos = past + (lane & (seq_pad - 1))
        new_rows = nsel_ref.shape[1]
        kpos = past + lax.broadcasted_iota(jnp.int32, (new_rows, 1), 0)
        causal = kpos <= tpos
        new_blk = past // NSA_SEL_BLOCK
        nrow = jnp.broadcast_to(smask_ref[0, new_blk:new_blk + 1, :], (new_rows, LANES)) > 0.5
        nsel = nsel_ref[0]
        _flash_update(mn, ln, accn, nsel[:, :kw_n], nsel[:, kw_n:], qn, causal & nrow)
        nmob = nmob_ref[0]
        _flash_update(mm, lm, accm, nmob[:, :kw_m], nmob[:, kw_m:], qm, causal)
        on_ref[0] = opart_ref[0] + g1_ref[0] * (accn[...] / jnp.maximum(ln[...], 1e-30))
        om_ref[0] = accm[...] / jnp.maximum(lm[...], 1e-30)


def _block_diag_q(x, batch, seq, groups, rep):
    x = x.reshape(batch, seq, groups, rep, _HD).transpose(0, 2, 4, 3, 1).reshape(batch, groups, _HD, rep * seq)
    bd = jnp.einsum('bgdn,gh->bgdhn', x, jnp.eye(groups, dtype=x.dtype))
    bd = bd.reshape(batch, groups * _HD, groups * rep * seq)
    return jnp.pad(bd, ((0, 0), (0, 0), (0, LANES - bd.shape[2])))


def _diag_blocks(o, batch, seq, groups, rep):
    o = o[:, :, :groups * rep * seq].reshape(batch, groups, _HD, groups, rep, seq)
    o = jnp.stack([o[:, g, :, g] for g in range(groups)], axis=1)
    return o.transpose(0, 4, 1, 3, 2).reshape(batch * seq, groups * rep * _HD)


def _decode_attention(qu, qr, mq, ng, rows3, win3, pool, pt, cache_win, layer, wbig, pex, w2big, batch, seq):
    npages = pt.shape[1]
    past = npages * PAGE_SIZE
    lt = past + seq
    assert seq <= 8 and seq < NSA_CMP_STRIDE and past % MOBA_BLOCK == 0
    seq_pad = 8
    tq = 64
    scale = _HD ** -0.5

    n_layers = pool.shape[1]
    poolv = _pool_view(pool)
    cmpflat, kbar = _scan_pages(poolv, pt, layer, n_layers)
    kcvc = _compress(cmpflat, wbig, pex, w2big)
    nseg = kcvc.shape[1]
    nc = (lt - NSA_CMP_LEN) // NSA_CMP_STRIDE + 1
    assert nseg == nc + 1
    n_sel = -(-lt // NSA_SEL_BLOCK)
    nsel_pad = -(-n_sel // 8) * 8
    nbf = lt // MOBA_BLOCK
    nbf_pad = -(-(nbf + 1) // 8) * 8
    ng_steps = npages // PAGES_PER_STEP

    def padq(x):
        x = x.reshape(batch, seq, -1)
        return jnp.pad(x, ((0, 0), (0, tq - seq), (0, 0))).reshape(batch * tq, -1)

    qu_t = _to_qt(padq(qu) * scale, batch, 1, tq, NSA_KV, NSA_GRP)
    qr_t = _to_qt(padq(qr) * scale, batch, 1, tq, NSA_KV, NSA_GRP)
    mq_t = _to_qt(padq(mq), batch, 1, tq, MOBA_KV, MOBA_GRP)
    gl = padq(ng)[:, :NSA_HEADS * 3].reshape(batch, 1, tq, NSA_KV, NSA_GRP, 3).transpose(0, 1, 3, 5, 4, 2)
    gl = gl.reshape(batch, 1, NSA_KV, 3, NSA_GRP * tq)
    kc = _head_major(kcvc[:, :, 0:_C_KV], NSA_KV)
    vc = _head_major(kcvc[:, :, _C_KV:2 * _C_KV], NSA_KV)
    mt = _sel_score_matrix(nsel_pad, nseg, nc)
    wb = cache_win.reshape(batch, -1, 2 * _C_KV)
    wcat = jnp.concatenate([wb, win3], axis=1)
    wl = -(-wcat.shape[1] // LANES) * LANES
    wfull = jnp.pad(wcat, ((0, 0), (0, wl - wcat.shape[1]), (0, 0)))
    kw = _head_major(wfull[:, :, 0:_C_KV], NSA_KV).astype(BF16)
    vw = _head_major(wfull[:, :, _C_KV:], NSA_KV).astype(BF16)
    kb = _head_major(jnp.pad(kbar, ((0, 0), (0, nbf_pad - nbf), (0, 0))), MOBA_KV)

    n_n = NSA_GRP * tq
    n_m = MOBA_GRP * tq
    b1 = lambda shp: pl.BlockSpec((1,) + shp, lambda b: (b,) + (0,) * len(shp))
    o_part, g1, sel = pl.pallas_call(
        functools.partial(_nsa_select_kernel, tq=tq, nc=nc, n_sel=n_sel, qpos0=past, wpos0=past - wb.shape[1]),
        grid=(batch,),
        in_specs=[b1((1, NSA_KV, _HD, n_n)), b1((1, NSA_KV, _HD, n_n)), b1((1, NSA_KV, 3, n_n)),
                  b1(kc.shape[1:]), b1(vc.shape[1:]), _full(mt.shape), b1(kw.shape[1:]), b1(vw.shape[1:])],
        out_specs=[b1((NSA_KV, _HD, n_n)), b1((NSA_KV, 1, n_n)), b1((NSA_KV, nsel_pad, tq))],
        out_shape=[jax.ShapeDtypeStruct((batch, NSA_KV, _HD, n_n), F32),
                   jax.ShapeDtypeStruct((batch, NSA_KV, 1, n_n), F32),
                   jax.ShapeDtypeStruct((batch, NSA_KV, nsel_pad, tq), F32)],
        compiler_params=_cp("parallel"),
        name="nsa_select",
    )(qu_t, qr_t, gl, kc, vc, mt, kw, vw)
    msel = pl.pallas_call(
        functools.partial(_moba_select_kernel, own=past // MOBA_BLOCK, nbf=nbf),
        grid=(batch,),
        in_specs=[b1((1, MOBA_KV, _HD, n_m)), b1(kb.shape[1:])],
        out_specs=b1((MOBA_KV, nbf_pad, n_m)),
        out_shape=jax.ShapeDtypeStruct((batch, MOBA_KV, nbf_pad, n_m), F32),
        compiler_params=_cp("parallel"),
        name="moba_select",
    )(mq_t, kb)

    def lanes(x):
        return jnp.pad(x, [(0, 0)] * (x.ndim - 1) + [(0, LANES - x.shape[-1])])

    pad_s = lambda x: jnp.pad(x.reshape(batch, seq, -1), ((0, 0), (0, seq_pad - seq), (0, 0)))
    qn_bd = _block_diag_q(pad_s(qr).reshape(batch * seq_pad, -1) * scale, batch, seq_pad, NSA_KV, NSA_GRP).astype(BF16)
    qm_bd = _block_diag_q(pad_s(mq).reshape(batch * seq_pad, -1) * scale, batch, seq_pad, MOBA_KV, MOBA_GRP).astype(BF16)
    smask = jnp.broadcast_to(sel[:, :, :, None, :seq_pad], (batch, NSA_KV, nsel_pad, NSA_GRP, seq_pad))
    smask = lanes(smask.transpose(0, 2, 1, 3, 4).reshape(batch, nsel_pad, -1))
    mmask = msel.reshape(batch, MOBA_KV, nbf_pad, MOBA_GRP, tq)[..., :seq_pad]
    mmask = lanes(mmask.transpose(0, 2, 1, 3, 4).reshape(batch, nbf_pad, -1))
    op = o_part.reshape(batch, NSA_KV, _HD, NSA_GRP, tq)[..., :seq_pad]
    op_bd = jnp.einsum('bgdrt,gh->bgdhrt', op, jnp.eye(NSA_KV, dtype=F32))
    op_bd = lanes(op_bd.reshape(batch, NSA_KV * _HD, -1))
    g1l = g1.reshape(batch, NSA_KV, NSA_GRP, tq)[..., :seq_pad].reshape(batch, 1, -1)
    g1l = lanes(g1l)
    new_pad = 16
    pad_n = lambda x: jnp.pad(x, ((0, 0), (0, new_pad - seq), (0, 0)))
    new_sel = pad_n(rows3[:, :, 2 * _C_KV:4 * _C_KV])
    new_mob = pad_n(rows3[:, :, 4 * _C_KV:])

    kw_n = NSA_KV * _HD
    bj = lambda shp: pl.BlockSpec((1,) + shp, lambda b, j, p: (b,) + (0,) * len(shp))
    gs = pltpu.PrefetchScalarGridSpec(
        num_scalar_prefetch=1,
        grid=(batch, ng_steps),
        in_specs=(_page_specs(npages, n_layers, layer, 2 * kw_n, 1) + _page_specs(npages, n_layers, layer, 2 * _C_MK, 1)
                  + [bj((kw_n, LANES)), bj((_C_MK, LANES)), bj((nsel_pad, LANES)), bj((nbf_pad, LANES)),
                     bj((new_pad, 2 * kw_n)), bj((new_pad, 2 * _C_MK)), bj((kw_n, LANES)), bj((1, LANES))]),
        out_specs=[bj((kw_n, LANES)), bj((_C_MK, LANES))],
        scratch_shapes=[pltpu.VMEM((1, LANES), F32), pltpu.VMEM((1, LANES), F32), pltpu.VMEM((kw_n, LANES), F32),
                        pltpu.VMEM((1, LANES), F32), pltpu.VMEM((1, LANES), F32), pltpu.VMEM((_C_MK, LANES), F32)],
    )
    o_n, o_m = pl.pallas_call(
        functools.partial(_dec_attn_kernel, past=past, nsteps=ng_steps, seq_pad=seq_pad),
        grid_spec=gs,
        out_shape=[jax.ShapeDtypeStruct((batch, kw_n, LANES), F32), jax.ShapeDtypeStruct((batch, _C_MK, LANES), F32)],
        compiler_params=_cp("parallel", "arbitrary"),
        name="decode_attn",
    )(pt.reshape(-1), *([poolv] * (2 * PAGES_PER_STEP)), qn_bd, qm_bd, smask, mmask, new_sel, new_mob, op_bd, g1l)
    o_n = _diag_blocks(o_n, batch, seq_pad, NSA_KV, NSA_GRP).reshape(batch, seq_pad, -1)[:, :seq]
    o_m = _diag_blocks(o_m, batch, seq_pad, MOBA_KV, MOBA_GRP).reshape(batch, seq_pad, -1)[:, :seq]
    return o_n.reshape(batch * seq, -1), o_m.reshape(batch * seq, -1)


def _prep_weights(prm):
    w = {}
    w['w_in_a'] = prm['w_in_a'].astype(BF16)
    w['w_out_a'] = prm['w_out_a'].astype(BF16)
    w['w_out_c'] = prm['w_out_c'].astype(BF16)
    w['w_in_c'] = [_proj_c_weights(prm['w_in_c'][c]) for c in range(prm['w_in_c'].shape[0])]
    w['cmp'] = [_compress_weights(prm['cmp_w1_k'][c], prm['cmp_w1_v'][c], prm['cmp_w2_k'][c], prm['cmp_w2_v'][c],
                                  prm['cmp_pe_k'][c], prm['cmp_pe_v'][c]) for c in range(prm['w_in_c'].shape[0])]
    depth, d, _ = prm['router_c_w'].shape
    wr = jnp.concatenate([prm['router_c_w'], prm['router_f_w'].transpose(0, 2, 1, 3).reshape(depth, d, -1)], axis=2)
    w['router_w'] = jnp.pad(wr, ((0, 0), (0, 0), (0, LANES - wr.shape[2])))
    br = jnp.concatenate([prm['router_c_b'], prm['router_f_b'].reshape(depth, -1)], axis=1)
    w['router_b'] = jnp.pad(br, ((0, 0), (0, LANES - br.shape[1])))[:, None, :]
    w['moe_w1'] = prm['moe_w1']
    w['moe_w3'] = prm['moe_w3']
    w['moe_w2'] = prm['moe_w2']
    w['ple_w'] = prm['ple_w'].astype(BF16)
    w['ple_gate_w'] = prm['ple_gate_w'].astype(BF16)
    w['lb'] = jnp.cumsum(jax.nn.softmax(prm['hgrn_lb'].astype(F32), axis=0), axis=0)
    return w


def _forward(x, p, pos, prm, w, ctx):
    batch, seq, d = x.shape
    t = batch * seq
    depth = p.shape[0]
    h = x.reshape(t, d)
    outs = {}
    for i in range(depth):
        gmix = prm['norm_mix'][i][None, :]
        if i % 2 == 0:
            a = i // 2
            lb512 = jnp.tile(w['lb'][a], HG_HEADS)[None, :]
            q, k, lf, iv, sg, gu, vn = _proj_a(h, gmix, w['w_in_a'][a], lb512, prm['cm_vnorm'][a][None, :])
            if ctx is None:
                st0 = jnp.zeros((batch, HG_WIDTH, HG_WIDTH), F32)
            else:
                s0 = ctx['state_hgrn'][a].astype(F32)
                eye = jnp.eye(HG_HEADS, dtype=F32)
                st0 = jnp.einsum('bhde,hg->bhegd', s0, eye).reshape(batch, HG_WIDTH, HG_WIDTH)
            o, st = _hgrn(q, k, lf, iv, sg, st0, prm['hgrn_onorm'][a][None, :], batch, seq)
            st5 = st.reshape(batch, HG_HEADS, HG_DK, HG_HEADS, HG_DK)
            s_new = jnp.stack([st5[:, hh, :, hh, :] for hh in range(HG_HEADS)], axis=1).transpose(0, 1, 3, 2)
            cm = _cmix(vn, gu, prm['cm_ws'][a], prm['cm_bs'][a].T, batch, seq)
            h = _out_proj(h, o, cm, w['w_out_a'][a])
            outs.setdefault('hg', []).append(s_new)
            outs.setdefault('cm', []).append(vn.reshape(batch, seq, CM_WIDTH))
        else:
            c = i // 2
            qu, qr, mq, ng, rows, win, *flat = _proj_c(h, gmix, w['w_in_c'][c], pos, seq, want_flat=ctx is None)
            rows3 = rows.reshape(batch, seq, ROW_W)
            win3 = win.reshape(batch, seq, 2 * _C_KV)
            wbig, pex, w2big = w['cmp'][c]
            if ctx is None:
                assert seq % KEY_TILE == 0
                nseg = seq // NSA_CMP_STRIDE
                kcvc = _compress(flat[0].reshape(batch, nseg, -1), wbig, pex, w2big)
                o_n, o_m = _mixer_c_attention(qu, qr, mq, ng, rows3, win3, kcvc, batch, seq)
                new_win = win3[:, seq - min(NSA_WINDOW, seq):]
                kv_rows = flat[1].reshape(batch, KV_ROW_HEADS, _HD, seq).transpose(0, 3, 1, 2)
            else:
                o_n, o_m = _decode_attention(qu, qr, mq, ng, rows3, win3, ctx['cache_kv'], ctx['page_table'],
                                             ctx['cache_win'][c], c, wbig, pex, w2big, batch, seq)
                new_win = win3
                kv_rows = rows3.reshape(batch, seq, KV_ROW_HEADS, _HD)
            h = _out_proj(h, o_n, o_m, w['w_out_c'][c])
            outs.setdefault('kv', []).append(kv_rows)
            outs.setdefault('win', []).append(new_win.reshape(batch, -1, 2 * NSA_KV, _HD))
        y = _moe(h, prm['norm_ffn'][i][None, :], w['router_w'][i], w['router_b'][i],
                 w['moe_w1'][i], w['moe_w3'][i], w['moe_w2'][i])
        h = _ple(h, y, p[i].reshape(t, -1), prm['norm_ple'][i][None, :], w['ple_gate_w'][i], w['ple_w'][i],
                 prm['final_norm'][None, :], final=(i == depth - 1))
    return h.reshape(batch, seq, d), outs


def kernel(x_prompt, x_sample, cache_kv, cache_win, state_hgrn, page_table, p_prompt, p_sample,
           norm_mix, norm_ffn, norm_ple, final_norm, w_in_a, w_out_a, hgrn_lb, hgrn_onorm, cm_vnorm,
           cm_ws, cm_bs, w_in_c, w_out_c, cmp_pe_k, cmp_w1_k, cmp_w2_k, cmp_pe_v, cmp_w1_v, cmp_w2_v,
           router_c_w, router_c_b, router_f_w, router_f_b, moe_w1, moe_w3, moe_w2, ple_w, ple_gate_w):
    prm = dict(norm_mix=norm_mix, norm_ffn=norm_ffn, norm_ple=norm_ple, final_norm=final_norm,
               w_in_a=w_in_a, w_out_a=w_out_a, hgrn_lb=hgrn_lb, hgrn_onorm=hgrn_onorm, cm_vnorm=cm_vnorm,
               cm_ws=cm_ws, cm_bs=cm_bs, w_in_c=w_in_c, w_out_c=w_out_c, cmp_pe_k=cmp_pe_k,
               cmp_w1_k=cmp_w1_k, cmp_w2_k=cmp_w2_k, cmp_pe_v=cmp_pe_v, cmp_w1_v=cmp_w1_v, cmp_w2_v=cmp_w2_v,
               router_c_w=router_c_w, router_c_b=router_c_b, router_f_w=router_f_w, router_f_b=router_f_b,
               moe_w1=moe_w1, moe_w3=moe_w3, moe_w2=moe_w2, ple_w=ple_w, ple_gate_w=ple_gate_w)
    w = _prep_weights(prm)
    past = page_table.shape[1] * PAGE_SIZE
    y_p, o_p = _forward(x_prompt, p_prompt, jnp.arange(x_prompt.shape[1]), prm, w, None)
    ctx = dict(cache_kv=cache_kv, cache_win=cache_win, state_hgrn=state_hgrn, page_table=page_table)
    y_s, o_s = _forward(x_sample, p_sample, past + jnp.arange(x_sample.shape[1]), prm, w, ctx)
    return (y_p, y_s,
            jnp.stack(o_p['kv'], axis=1), jnp.stack(o_s['kv'], axis=1),
            jnp.stack(o_p['win'], axis=0), jnp.stack(o_s['win'], axis=0),
            jnp.stack(o_p['hg'], axis=0), jnp.stack(o_s['hg'], axis=0),
            jnp.stack(o_s['cm'], axis=0))
```

```python
import functools
import math

import numpy as np
import jax
import jax.numpy as jnp
from jax import lax
from jax.experimental import pallas as pl
from jax.experimental.pallas import tpu as pltpu

F32 = jnp.float32
BF16 = jnp.bfloat16
HI = lax.Precision.HIGHEST

PAGE_SIZE = 128
HEAD_DIM = 64
ROT_DIM = HEAD_DIM // 4
ROPE_THETA = 500000.0
HG_HEADS = 8
HG_DK = 64
HG_WIDTH = HG_HEADS * HG_DK
HG_CHUNK = 64
CM_GROUPS = 4
CM_GW = 128
CM_WIDTH = CM_GROUPS * CM_GW
CM_CHUNK = 128
NSA_HEADS = 8
NSA_KV = 2
NSA_GRP = NSA_HEADS // NSA_KV
NSA_CMP_LEN = 32
NSA_CMP_STRIDE = 16
NSA_SEL_BLOCK = 64
NSA_TOPN = 16
NSA_WINDOW = 512
MOBA_HEADS = 8
MOBA_KV = 4
MOBA_GRP = MOBA_HEADS // MOBA_KV
MOBA_BLOCK = 256
MOBA_TOPK = 3
MOE_GROUPS = 4
MOE_EPG = 8
MOE_EXPERTS = MOE_GROUPS * MOE_EPG
KV_ROW_HEADS = 16
ROW_W = KV_ROW_HEADS * HEAD_DIM
A_SPLITS = [HG_WIDTH] * 4 + [CM_WIDTH] * 2
C_SPLITS = [NSA_HEADS * HEAD_DIM] + [NSA_KV * HEAD_DIM] * 6 + [NSA_HEADS * 3, MOBA_HEADS * HEAD_DIM,
                                                              MOBA_KV * HEAD_DIM, MOBA_KV * HEAD_DIM]
NEG = -1e30
BIG = 1e9
EPS = 1e-6
KEY_TILE = 256
LANES = 128
_MOBA_SHIFT = MOBA_BLOCK.bit_length() - 1
_SEL_SHIFT = NSA_SEL_BLOCK.bit_length() - 1
VMEM_LIMIT = 56 * 1024 * 1024


def _cp(*sem):
    return pltpu.CompilerParams(dimension_semantics=sem, vmem_limit_bytes=VMEM_LIMIT)


def _sigmoid(x):
    return 1.0 / (1.0 + jnp.exp(-x))


def _silu(x):
    return x * _sigmoid(x)


def _gelu(x):
    return 0.5 * x * (1.0 + jnp.tanh(math.sqrt(2.0 / math.pi) * (x + 0.044715 * (x * x * x))))


def _rms(x, g):
    return x * lax.rsqrt(jnp.mean(x * x, axis=-1, keepdims=True) + EPS) * g


def _dot(a, b):
    return jnp.dot(a, b, preferred_element_type=F32)


def _dot_hi(a, b):
    return jnp.dot(a, b, precision=HI, preferred_element_type=F32)


def _dot3(a, b):
    a_hi = a.astype(BF16)
    b_hi = b.astype(BF16)
    a_lo = (a - a_hi.astype(F32)).astype(BF16)
    b_lo = (b - b_hi.astype(F32)).astype(BF16)
    return _dot(a_hi, b_hi) + (_dot(a_hi, b_lo) + _dot(a_lo, b_hi))


def _full(shape):
    n = len(shape)
    return pl.BlockSpec(shape, lambda *_: (0,) * n)


def _row_tile(t):
    for tm in (256, 128, 64, 32, 16, 8):
        if t % tm == 0:
            return tm
    raise ValueError(f"token count {t} is not a multiple of 8")


def _proj_a_kernel(x_ref, g_ref, w_ref, lb_ref, vg_ref, q_ref, k_ref, lf_ref, iv_ref, sg_ref, gu_ref, vn_ref):
    xn = _rms(x_ref[...], g_ref[...]).astype(BF16)
    z = _dot(xn, w_ref[...])
    w = HG_WIDTH
    q_ref[...] = _silu(z[:, 0:w])
    lb = lb_ref[...]
    f = lb + (1.0 - lb) * _sigmoid(z[:, w:2 * w])
    k_ref[...] = 1.0 - f
    lf_ref[...] = jnp.log(f)
    iv_ref[...] = z[:, 2 * w:3 * w]
    sg_ref[...] = _silu(z[:, 3 * w:4 * w])
    gu_ref[...] = _gelu(z[:, 4 * w:4 * w + CM_WIDTH])
    v = _gelu(z[:, 4 * w + CM_WIDTH:])
    for gi in range(CM_GROUPS):
        sl = slice(gi * CM_GW, (gi + 1) * CM_GW)
        vn_ref[:, sl] = _rms(v[:, sl], vg_ref[:, sl])


def _proj_a(x, g, w_bf, lb512, vgain):
    t, d = x.shape
    tm = _row_tile(t)
    n = w_bf.shape[1]
    row = lambda wd: pl.BlockSpec((tm, wd), lambda i: (i, 0))
    outs = [jax.ShapeDtypeStruct((t, HG_WIDTH), F32)] * 5 + [jax.ShapeDtypeStruct((t, CM_WIDTH), F32)] * 2
    return pl.pallas_call(
        _proj_a_kernel,
        grid=(t // tm,),
        in_specs=[row(d), _full((1, d)), _full((d, n)), _full((1, HG_WIDTH)), _full((1, CM_WIDTH))],
        out_specs=[row(HG_WIDTH)] * 5 + [row(CM_WIDTH)] * 2,
        out_shape=outs,
        compiler_params=_cp("parallel"),
        name="proj_a",
    )(x, g, w_bf, lb512, vgain)


def _hgrn_kernel(q_ref, k_ref, lf_ref, v_ref, sg_ref, s0_ref, gain_ref, bones_ref, bmask_ref, tri_ref,
                 o_ref, sout_ref, st_scr, p_scr, *, chunk):
    c = pl.program_id(1)
    w = HG_WIDTH
    hw = w // 2

    @pl.when(c == 0)
    def _():
        st_scr[...] = s0_ref[0]

    q = q_ref[0]
    k = k_ref[0]
    v = v_ref[0]
    b = _dot_hi(tri_ref[...], lf_ref[0])
    bones = bones_ref[...]

    def head_sum(x):
        xb = x.astype(BF16)
        return jnp.concatenate([_dot(xb[:, :hw], bones), _dot(xb[:, hw:], bones)], axis=1)

    o_rows = []
    for blk in range(chunk // 8):
        s_len = 8 * (blk + 1)
        bs = b[:s_len]
        ks = k[:s_len]
        vs = v[:s_len]
        row_id = lax.broadcasted_iota(jnp.int32, (s_len, w), 0)
        for t in range(8):
            r = 8 * blk + t
            diff = jnp.where(row_id <= r, b[r:r + 1, :] - bs, NEG)
            p_scr[t * s_len:(t + 1) * s_len, :] = jnp.exp(diff) * q[r:r + 1, :] * ks
        att = head_sum(p_scr[0:8 * s_len, :])
        for t in range(8):
            o_rows.append(jnp.sum(att[t * s_len:(t + 1) * s_len] * vs, axis=0, keepdims=True))
    o_intra = jnp.concatenate(o_rows, axis=0)

    st = st_scr[...]
    qe = (q * jnp.exp(b)).astype(BF16)
    o = o_intra + lax.dot_general(qe, st.astype(BF16), (((1,), (1,)), ((), ())), preferred_element_type=F32)

    b_end = b[chunk - 1:chunk, :]
    kd = (k * jnp.exp(b_end - b)).astype(BF16)
    upd = lax.dot_general(v.astype(BF16), kd, (((0,), (0,)), ((), ())), preferred_element_type=F32)
    st_new = st * jnp.exp(b_end) + upd * bmask_ref[...]
    st_scr[...] = st_new

    o2 = o * o
    hi = o2.astype(BF16).astype(F32)
    ms = (head_sum(hi) + head_sum(o2 - hi)) * (1.0 / HG_DK)
    o_ref[0] = o * lax.rsqrt(ms + EPS) * gain_ref[...] * sg_ref[0]

    @pl.when(c == pl.num_programs(1) - 1)
    def _():
        sout_ref[0] = st_new


def _block_ones(n, blk, dtype):
    i = np.arange(n) // blk
    return jnp.asarray((i[:, None] == i[None, :]).astype(np.float32), dtype)


def _hgrn(q, k, lf, v, sg, st0, gain, batch, seq):
    w = HG_WIDTH
    chunk = math.gcd(seq, HG_CHUNK)
    nch = seq // chunk
    r3 = lambda a: a.reshape(batch, seq, w)
    tile = pl.BlockSpec((1, chunk, w), lambda b, c: (b, c, 0))
    state = pl.BlockSpec((1, w, w), lambda b, c: (b, 0, 0))
    tri = jnp.asarray(np.tril(np.ones((chunk, chunk), np.float32)))
    o, st = pl.pallas_call(
        functools.partial(_hgrn_kernel, chunk=chunk),
        grid=(batch, nch),
        in_specs=[tile] * 5 + [state, _full((1, w)), _full((w // 2, w // 2)), _full((w, w)), _full((chunk, chunk))],
        out_specs=[tile, state],
        out_shape=[jax.ShapeDtypeStruct((batch, seq, w), F32), jax.ShapeDtypeStruct((batch, w, w), F32)],
        scratch_shapes=[pltpu.VMEM((w, w), F32), pltpu.VMEM((8 * chunk, w), F32)],
        compiler_params=_cp("parallel", "arbitrary"),
        name="hgrn",
    )(r3(q), r3(k), r3(lf), r3(v), r3(sg), st0, gain, _block_ones(w // 2, HG_DK, BF16),
      _block_ones(w, HG_DK, F32), tri)
    return o.reshape(batch * seq, w), st


def _cmix_kernel(vn_ref, gu_ref, ws_ref, bst_ref, o_ref, *, tl):
    ri = lax.broadcasted_iota(jnp.int32, (tl, tl), 0)
    ci = lax.broadcasted_iota(jnp.int32, (tl, tl), 1)
    for gi in range(CM_GROUPS):
        sl = slice(gi * CM_GW, (gi + 1) * CM_GW)
        wg = jnp.where(ci <= ri, ws_ref[gi, 0:tl, 0:tl], 0.0).astype(BF16)
        mix = _dot(wg, vn_ref[0, :, sl].astype(BF16)) + bst_ref[0:tl, gi:gi + 1]
        o_ref[0, :, sl] = gu_ref[0, :, sl] * mix


def _cmix(vn, gu, ws, bst, batch, seq):
    tl = min(seq, CM_CHUNK)
    assert seq % tl == 0 and tl % 8 == 0
    w = CM_WIDTH
    tile = pl.BlockSpec((1, tl, w), lambda b, c: (b, c, 0))
    out = pl.pallas_call(
        functools.partial(_cmix_kernel, tl=tl),
        grid=(batch, seq // tl),
        in_specs=[tile, tile, _full(ws.shape), _full(bst.shape)],
        out_specs=tile,
        out_shape=jax.ShapeDtypeStruct((batch, seq, w), F32),
        compiler_params=_cp("parallel", "parallel"),
        name="cmix",
    )(vn.reshape(batch, seq, w), gu.reshape(batch, seq, w), ws, bst)
    return out.reshape(batch * seq, w)


def _out_proj_kernel(res_ref, a1_ref, a2_ref, w1_ref, w2_ref, o_ref):
    o_ref[...] = (res_ref[...] + _dot(a1_ref[...].astype(BF16), w1_ref[...])
                  + _dot(a2_ref[...].astype(BF16), w2_ref[...]))


def _out_proj(res, a1, a2, w_bf):
    t, d = res.shape
    k1, k2 = a1.shape[1], a2.shape[1]
    tm = _row_tile(t)
    row = lambda wd: pl.BlockSpec((tm, wd), lambda i: (i, 0))
    return pl.pallas_call(
        _out_proj_kernel,
        grid=(t // tm,),
        in_specs=[row(d), row(k1), row(k2), _full((k1, d)), _full((k2, d))],
        out_specs=row(d),
        out_shape=jax.ShapeDtypeStruct((t, d), F32),
        compiler_params=_cp("parallel"),
        name="out_proj",
    )(res, a1, a2, w_bf[:k1], w_bf[k1:])


def _router_kernel(h_ref, g_ref, w_ref, b_ref, xn_ref, info_ref):
    xn = _rms(h_ref[...], g_ref[...])
    xn_ref[...] = xn.astype(BF16)
    logit = _dot3(xn, w_ref[...]) + b_ref[...]
    lane = lax.broadcasted_iota(jnp.int32, logit.shape, 1)
    far = 1 << 20

    def first_max(vals, mask):
        m = jnp.max(jnp.where(mask, vals, -1.0), axis=-1, keepdims=True)
        idx = jnp.min(jnp.where(mask & (vals == m), lane, far), axis=-1, keepdims=True)
        return m, idx

    def softmax(mask):
        m = jnp.max(jnp.where(mask, logit, NEG), axis=-1, keepdims=True)
        e = jnp.where(mask, jnp.exp(logit - m), 0.0)
        return e / jnp.sum(e, axis=-1, keepdims=True)

    cmask = lane < MOE_GROUPS
    pg, grp = first_max(softmax(cmask), cmask)
    lo = MOE_GROUPS + grp * MOE_EPG
    fmask = (lane >= lo) & (lane < lo + MOE_EPG)
    pf = softmax(fmask)
    v1, i1 = first_max(pf, fmask)
    v2, i2 = first_max(pf, fmask & (lane != i1))
    den = v1 + v2
    info = jnp.where(lane == 0, (i1 - MOE_GROUPS).astype(F32), 0.0)
    info = jnp.where(lane == 1, (i2 - MOE_GROUPS).astype(F32), info)
    info = jnp.where(lane == 2, pg * v1 / den, info)
    info = jnp.where(lane == 3, pg * v2 / den, info)
    info_ref[...] = info


def _router(h, g, w_r, b_r):
    t, d = h.shape
    tm = _row_tile(t)
    row = lambda wd: pl.BlockSpec((tm, wd), lambda i: (i, 0))
    return pl.pallas_call(
        _router_kernel,
        grid=(t // tm,),
        in_specs=[row(d), _full((1, d)), _full((d, LANES)), _full((1, LANES))],
        out_specs=[row(d), row(LANES)],
        out_shape=[jax.ShapeDtypeStruct((t, d), BF16), jax.ShapeDtypeStruct((t, LANES), F32)],
        compiler_params=_cp("parallel"),
        name="router",
    )(h, g, w_r, b_r)


def _moe_ffn_kernel(be_ref, nu_ref, x_ref, w1_ref, w3_ref, w2_ref, o_ref, w1b, w3b, w2b):
    i = pl.program_id(0)

    @pl.when((i == 0) | (be_ref[i] != be_ref[jnp.maximum(i - 1, 0)]))
    def _():
        w1b[...] = w1_ref[0, 0].astype(BF16)
        w3b[...] = w3_ref[0, 0].astype(BF16)
        w2b[...] = w2_ref[0, 0].astype(BF16)

    @pl.when(i < nu_ref[0])
    def _():
        x = x_ref[...]
        hdn = _silu(_dot(x, w1b[...])) * _dot(x, w3b[...])
        o_ref[...] = _dot(hdn.astype(BF16), w2b[...])

    @pl.when(i >= nu_ref[0])
    def _():
        o_ref[...] = jnp.zeros_like(o_ref)


def _moe_ffn(xp, blk_e, nused, w1, w3, w2, layer, rb):
    n, d = xp.shape
    ff = w1.shape[3]
    nblk = n // rb
    gs = pltpu.PrefetchScalarGridSpec(
        num_scalar_prefetch=2,
        grid=(nblk,),
        in_specs=[pl.BlockSpec((rb, d), lambda i, be, nu: (i, 0)),
                  pl.BlockSpec((1, 1, d, ff), lambda i, be, nu: (layer, be[i], 0, 0)),
                  pl.BlockSpec((1, 1, d, ff), lambda i, be, nu: (layer, be[i], 0, 0)),
                  pl.BlockSpec((1, 1, ff, d), lambda i, be, nu: (layer, be[i], 0, 0))],
        out_specs=pl.BlockSpec((rb, d), lambda i, be, nu: (i, 0)),
        scratch_shapes=[pltpu.VMEM((d, ff), BF16), pltpu.VMEM((d, ff), BF16), pltpu.VMEM((ff, d), BF16)],
    )
    return pl.pallas_call(
        _moe_ffn_kernel,
        grid_spec=gs,
        out_shape=jax.ShapeDtypeStruct((n, d), F32),
        compiler_params=_cp("arbitrary"),
        name="moe_ffn",
    )(blk_e, nused, xp, w1, w3, w2)


def _moe(h, g, w_r, b_r, w1, w3, w2, layer):
    t, d = h.shape
    xn, info = _router(h, g, w_r, b_r)
    eid = info[:, 0:2].astype(jnp.int32)
    gate = info[:, 2:4]
    ne = MOE_EXPERTS
    tk = 2 * t
    rb = 256 if t >= 4096 else 32
    flat = eid.reshape(-1)
    order = jnp.argsort(flat).astype(jnp.int32)
    rank = jnp.argsort(order).astype(jnp.int32)
    counts = jnp.sum((flat[:, None] == jnp.arange(ne)[None, :]).astype(jnp.int32), axis=0)
    padc = (counts + rb - 1) // rb * rb
    pend = jnp.cumsum(padc)
    pstart = pend - padc
    cstart = jnp.cumsum(counts) - counts
    slot = (pstart[flat] + rank - cstart[flat]).reshape(t, 2)
    nblk = -(-tk // rb) + ne
    blk_e = jnp.minimum(jnp.sum((pend[None, :] <= (jnp.arange(nblk) * rb)[:, None]).astype(jnp.int32), axis=1),
                        ne - 1).astype(jnp.int32)
    pe = jnp.repeat(blk_e, rb)
    off = jnp.arange(nblk * rb) - pstart[pe]
    rows = jnp.where(off < counts[pe], order[jnp.clip(off + cstart[pe], 0, tk - 1)] // 2, 0)
    xp = xn[rows]
    nused = (pend[-1:] // rb).astype(jnp.int32)
    yp = _moe_ffn(xp, blk_e, nused, w1, w3, w2, layer, rb)
    return gate[:, 0:1] * yp[slot[:, 0]] + gate[:, 1:2] * yp[slot[:, 1]]


def _ple_kernel(h_ref, y_ref, p_ref, g_ref, wg_ref, wp_ref, fg_ref, o_ref, *, final):
    h = h_ref[...] + y_ref[...]
    gate = _sigmoid(_dot(_rms(h, g_ref[...]).astype(BF16), wg_ref[...]))
    out = h + gate * _dot(p_ref[...].astype(BF16), wp_ref[...])
    if final:
        out = _rms(out, fg_ref[...])
    o_ref[...] = out


def _ple(h, y, p, g, wg_bf, wp_bf, fg, final):
    t, d = h.shape
    pd = p.shape[1]
    tm = _row_tile(t)
    row = lambda wd: pl.BlockSpec((tm, wd), lambda i: (i, 0))
    return pl.pallas_call(
        functools.partial(_ple_kernel, final=final),
        grid=(t // tm,),
        in_specs=[row(d), row(d), row(pd), _full((1, d)), _full((d, d)), _full((pd, d)), _full((1, d))],
        out_specs=row(d),
        out_shape=jax.ShapeDtypeStruct((t, d), F32),
        compiler_params=_cp("parallel"),
        name="ple",
    )(h, y, p, g, wg_bf, wp_bf, fg)


_HD = HEAD_DIM
_C_NQ = NSA_HEADS * _HD
_C_KV = NSA_KV * _HD
_C_MQ = MOBA_HEADS * _HD
_C_MK = MOBA_KV * _HD
_O_NQ = 0
_O_CK = _O_NQ + _C_NQ
_O_CV = _O_CK + _C_KV
_O_SK = _O_CV + _C_KV
_O_SV = _O_SK + _C_KV
_O_WK = _O_SV + _C_KV
_O_WV = _O_WK + _C_KV
_O_MQ = _O_WV + _C_KV
_O_MK = _O_MQ + _C_MQ
_O_MV = _O_MK + _C_MK
_O_NG = _O_MV + _C_MK
_O_NQS = _O_NG + LANES
_O_SKS = _O_NQS + _C_NQ
_O_WKS = _O_SKS + _C_KV
_O_MQS = _O_WKS + _C_KV
_O_MKS = _O_MQS + _C_MQ
_C_TOTAL = _O_MKS + _C_MK


def _proj_c_kernel(x_ref, g_ref, w_ref, cos_ref, sin_ref, qu_ref, qr_ref, mq_ref, ng_ref, rows_ref, win_ref,
                   *flat_refs):
    xn = _rms(x_ref[...], g_ref[...]).astype(BF16)
    z = _dot(xn, w_ref[...])
    cs = cos_ref[...]
    sn = sin_ref[...]

    def rope(o, os, wd):
        reps = wd // LANES
        return z[:, o:o + wd] * jnp.tile(cs, (1, reps)) + z[:, os:os + wd] * jnp.tile(sn, (1, reps))

    qu_ref[...] = z[:, _O_NQ:_O_NQ + _C_NQ]
    qr_ref[...] = rope(_O_NQ, _O_NQS, _C_NQ)
    mq_ref[...] = rope(_O_MQ, _O_MQS, _C_MQ)
    ng_ref[...] = z[:, _O_NG:_O_NG + LANES]
    rows_ref[:, 0:2 * _C_KV] = z[:, _O_CK:_O_CK + 2 * _C_KV]
    rows_ref[:, 2 * _C_KV:3 * _C_KV] = rope(_O_SK, _O_SKS, _C_KV)
    rows_ref[:, 3 * _C_KV:4 * _C_KV] = z[:, _O_SV:_O_SV + _C_KV]
    rows_ref[:, 4 * _C_KV:4 * _C_KV + _C_MK] = rope(_O_MK, _O_MKS, _C_MK)
    rows_ref[:, 4 * _C_KV + _C_MK:] = z[:, _O_MV:_O_MV + _C_MK]
    win_ref[:, 0:_C_KV] = rope(_O_WK, _O_WKS, _C_KV)
    win_ref[:, _C_KV:] = z[:, _O_WV:_O_WV + _C_KV]
    if not flat_refs:
        return
    flat_ref, rows_t_ref, ck_scr, cv_scr = flat_refs
    rows_t_ref[0] = rows_ref[...].T
    ck_scr[...] = z[:, _O_CK:_O_CK + _C_KV]
    cv_scr[...] = z[:, _O_CV:_O_CV + _C_KV]
    nseg = flat_ref.shape[0]
    for i in range(NSA_CMP_STRIDE):
        c0 = i * 2 * _C_KV
        flat_ref[:, c0:c0 + _C_KV] = ck_scr[pl.ds(i, nseg, stride=NSA_CMP_STRIDE), :].astype(BF16)
        flat_ref[:, c0 + _C_KV:c0 + 2 * _C_KV] = cv_scr[pl.ds(i, nseg, stride=NSA_CMP_STRIDE), :].astype(BF16)


def _proj_c_weights(w):
    offs = np.concatenate([[0], np.cumsum(C_SPLITS)])
    nq, ck, cv, sk, sv, wk, wv, ng, mq, mk, mv = [w[:, offs[i]:offs[i + 1]] for i in range(len(C_SPLITS))]
    half = ROT_DIM // 2

    def swapped(m):
        d = m.shape[0]
        m3 = m.reshape(d, -1, _HD)
        out = jnp.concatenate([m3[..., half:ROT_DIM], m3[..., :half], jnp.zeros_like(m3[..., ROT_DIM:])], axis=-1)
        return out.reshape(d, -1)

    ngp = jnp.pad(ng, ((0, 0), (0, LANES - ng.shape[1])))
    return jnp.concatenate([nq, ck, cv, sk, sv, wk, wv, mq, mk, mv, ngp,
                            swapped(nq), swapped(sk), swapped(wk), swapped(mq), swapped(mk)], axis=1).astype(BF16)


def _rope_tables(pos):
    half = ROT_DIM // 2
    inv = ROPE_THETA ** (-jnp.arange(half, dtype=F32) / half)
    ang = pos.astype(F32)[:, None] * inv
    cos, sin = jnp.cos(ang), jnp.sin(ang)
    n = pos.shape[0]
    c64 = jnp.concatenate([cos, cos, jnp.ones((n, _HD - ROT_DIM), F32)], axis=1)
    s64 = jnp.concatenate([-sin, sin, jnp.zeros((n, _HD - ROT_DIM), F32)], axis=1)
    return jnp.tile(c64, (1, LANES // _HD)), jnp.tile(s64, (1, LANES // _HD))


def _proj_c(x, g, wc_bf, pos, seq, want_flat):
    t, d = x.shape
    tm = _row_tile(t)
    tr = max(seq, tm)
    assert tr % tm == 0 and tr % seq == 0
    cs, sn = _rope_tables(jnp.tile(pos, tr // seq))
    ntab = tr // tm
    row = lambda wd: pl.BlockSpec((tm, wd), lambda i: (i, 0))
    tab = pl.BlockSpec((tm, LANES), lambda i: (i % ntab, 0))
    widths = [_C_NQ, _C_NQ, _C_MQ, LANES, ROW_W, 2 * _C_KV]
    st = NSA_CMP_STRIDE
    fw = st * 2 * _C_KV
    out_specs = [row(wd) for wd in widths]
    out_shape = [jax.ShapeDtypeStruct((t, wd), F32) for wd in widths]
    scratch = []
    if want_flat:
        assert tm % (16 * st) == 0 and seq % tm == 0
        tpb = seq // tm
        out_specs.append(pl.BlockSpec((tm // st, fw), lambda i: (i, 0)))
        out_shape.append(jax.ShapeDtypeStruct((t // st, fw), BF16))
        out_specs.append(pl.BlockSpec((1, ROW_W, tm), lambda i: (i // tpb, 0, i % tpb)))
        out_shape.append(jax.ShapeDtypeStruct((t // seq, ROW_W, seq), F32))
        scratch = [pltpu.VMEM((tm, _C_KV), F32), pltpu.VMEM((tm, _C_KV), F32)]
    return pl.pallas_call(
        _proj_c_kernel,
        grid=(t // tm,),
        in_specs=[row(d), _full((1, d)), _full((d, _C_TOTAL)), tab, tab],
        out_specs=out_specs,
        out_shape=out_shape,
        scratch_shapes=scratch,
        compiler_params=_cp("parallel"),
        name="proj_c",
    )(x, g, wc_bf, cs, sn)


def _mm_kernel(a_ref, w_ref, o_ref):
    o_ref[...] = _dot(a_ref[...], w_ref[...])


def _mm(a_bf, w_bf):
    m, k = a_bf.shape
    n = w_bf.shape[1]
    tm = _row_tile(m)
    return pl.pallas_call(
        _mm_kernel,
        grid=(m // tm,),
        in_specs=[pl.BlockSpec((tm, k), lambda i: (i, 0)), _full((k, n))],
        out_specs=pl.BlockSpec((tm, n), lambda i: (i, 0)),
        out_shape=jax.ShapeDtypeStruct((m, n), F32),
        compiler_params=_cp("parallel"),
        name="mm",
    )(a_bf, w_bf)


def _cmp_fin_kernel(a_ref, pb_ref, w2_ref, o_ref, *, nseg):
    a = a_ref[0]
    hw = a.shape[1] // 2
    bias = pb_ref[0:1, :hw] + pb_ref[1:2, hw:]
    pre = a[:, :hw] + pltpu.roll(a[:, hw:], nseg - 1, axis=0) + bias
    o_ref[0] = _dot(_gelu(pre).astype(BF16), w2_ref[...])


def _compress_weights(w1k, w1v, w2k, w2v, pek, pev):
    span = NSA_CMP_LEN // NSA_CMP_STRIDE
    st = NSA_CMP_STRIDE
    hid = w1k.shape[1]
    slots = 2 * NSA_KV
    eye = jnp.eye(slots, dtype=F32)
    per_slot = lambda k, v: jnp.stack([k] * NSA_KV + [v] * NSA_KV)
    w1 = per_slot(w1k.reshape(span, st, _HD, hid), w1v.reshape(span, st, _HD, hid))
    wbig = jnp.einsum('smide,st->isdmte', w1, eye)
    pe = per_slot(pek.reshape(span, st, _HD), pev.reshape(span, st, _HD))
    pex = jnp.pad(pe.transpose(1, 2, 0, 3), ((0, 8 - span), (0, 0), (0, 0), (0, 0)))
    w2big = jnp.einsum('sed,st->setd', per_slot(w2k, w2v), eye)
    return (wbig.reshape(st * slots * _HD, span * slots * hid).astype(BF16),
            pex.reshape(8, st * slots * _HD).astype(BF16),
            w2big.reshape(slots * hid, slots * _HD).astype(BF16))


def _compress(cmpflat_bf, wbig, pex, w2big):
    batch, nseg, kdim = cmpflat_bf.shape
    a = _mm(cmpflat_bf.reshape(batch * nseg, kdim), wbig).reshape(batch, nseg, -1)
    pb = _mm(pex, wbig)
    n2 = a.shape[2]
    ow = w2big.shape[1]
    return pl.pallas_call(
        functools.partial(_cmp_fin_kernel, nseg=nseg),
        grid=(batch,),
        in_specs=[pl.BlockSpec((1, nseg, n2), lambda b: (b, 0, 0)), _full(pb.shape), _full(w2big.shape)],
        out_specs=pl.BlockSpec((1, nseg, ow), lambda b: (b, 0, 0)),
        out_shape=jax.ShapeDtypeStruct((batch, nseg, ow), F32),
        compiler_params=_cp("parallel"),
        name="compress_fin",
    )(a, pb, w2big)


def _block_mean_kernel(k_ref, o_ref):
    o_ref[0, 0] = jnp.sum(k_ref[0], axis=0, keepdims=True) * (1.0 / MOBA_BLOCK)


def _block_mean(rows3, nbf, col_block):
    batch = rows3.shape[0]
    wd = _C_MK
    return pl.pallas_call(
        _block_mean_kernel,
        grid=(batch, nbf),
        in_specs=[pl.BlockSpec((1, MOBA_BLOCK, wd), lambda b, n: (b, n, col_block))],
        out_specs=pl.BlockSpec((1, 1, 1, wd), lambda b, n: (b, n, 0, 0)),
        out_shape=jax.ShapeDtypeStruct((batch, nbf, 1, wd), F32),
        compiler_params=_cp("parallel", "parallel"),
        name="block_mean",
    )(rows3)


def _top_rounds(score, row_id, k):
    far = 1 << 20
    sel = jnp.zeros(score.shape, F32)
    cur = score
    for _ in range(k):
        m = jnp.max(cur, axis=0, keepdims=True)
        idx = jnp.min(jnp.where(cur == m, row_id, far), axis=0, keepdims=True)
        pick = row_id == idx
        sel = jnp.where(pick, 1.0, sel)
        cur = jnp.where(pick, -jnp.inf, cur)
    return sel


def _softmax0(s, mask):
    s = jnp.where(mask, s, NEG)
    e = jnp.where(mask, jnp.exp(s - jnp.max(s, axis=0, keepdims=True)), 0.0)
    return e / jnp.maximum(jnp.sum(e, axis=0, keepdims=True), 1e-30)


def _tdot(a, b):
    return lax.dot_general(a, b, (((0,), (0,)), ((), ())), preferred_element_type=F32)


def _nsa_front(qu, kc, vc, mt, tpos, tq, nc, n_sel):
    nc_pad = kc.shape[0]
    nsel_pad = mt.shape[0]
    ci = lax.broadcasted_iota(jnp.int32, (nc_pad, 1), 0)
    bi = lax.broadcasted_iota(jnp.int32, (nsel_pad, tq), 0)
    cur = tpos[:, :tq] >> _SEL_SHIFT
    cmask = ((ci * NSA_CMP_STRIDE + (NSA_CMP_LEN - 1)) <= tpos) & (ci < nc)
    pc = _softmax0(_dot_hi(kc, qu), cmask)
    o_c = _tdot(vc.astype(BF16), pc.astype(BF16))
    imp = pc[:, 0:tq]
    for r in range(1, NSA_GRP):
        imp = imp + pc[:, r * tq:(r + 1) * tq]
    score = _dot_hi(mt, imp)
    forced = (bi == 0) | (bi == cur) | (bi == cur - 1)
    score = jnp.where(forced, BIG, score)
    score = jnp.where(bi <= cur, score, -BIG)
    score = jnp.where(bi < n_sel, score, -3e38)
    sel = _top_rounds(score, bi, min(NSA_TOPN, n_sel))
    return o_c, jnp.where(bi <= cur, sel, 0.0)


def _nsa_window(qr, kw, vw, tpos, wpos):
    wmask = (wpos <= tpos) & (wpos >= tpos - NSA_WINDOW) & (wpos >= 0)
    pw = _softmax0(_dot(kw.astype(BF16), qr), wmask).astype(BF16)
    return _tdot(vw.astype(BF16), pw)


def _online_softmax(m_ref, l_ref, acc_ref, s, mask, pv, guard=True):
    s = jnp.where(mask, s, NEG)
    m_old = m_ref[...]
    m_new = jnp.maximum(m_old, jnp.max(s, axis=0, keepdims=True))
    alpha = jnp.exp(m_old - m_new)
    p = jnp.exp(s - m_new)
    if guard:
        p = jnp.where(mask, p, 0.0)
    l_ref[...] = alpha * l_ref[...] + jnp.sum(p, axis=0, keepdims=True)
    acc_ref[...] = alpha * acc_ref[...] + pv(p.astype(BF16))
    m_ref[...] = m_new


def _heads_t(x, tq, groups, rep):
    heads = []
    for p in range(groups * rep // 2):
        xt = x[:, p * LANES:(p + 1) * LANES].T
        heads += [xt[0:_HD], xt[_HD:2 * _HD]]
    return [jnp.concatenate(heads[g * rep:(g + 1) * rep], axis=1) for g in range(groups)]


def _heads_untranspose(o_ref, per_group, tq, rep):
    heads = [og[:, r * tq:(r + 1) * tq] for og in per_group for r in range(rep)]
    for p in range(len(heads) // 2):
        o_ref[:, p * LANES:(p + 1) * LANES] = jnp.concatenate([heads[2 * p], heads[2 * p + 1]], axis=0).T


def _block_diag(qs):
    z = jnp.zeros_like(qs[0])
    return jnp.concatenate([jnp.concatenate([q if h == g else z for h in range(len(qs))], axis=1)
                            for g, q in enumerate(qs)], axis=0)


def _nsa_kernel(qu_ref, qr_ref, ng_ref, kcvc_ref, mt_ref, ksv_ref, win_ref,
                o_ref, sel_scr, m_scr, l_scr, acc_scr, *, tq, nc, n_sel, nkt_max, wl):
    qi = pl.program_id(1)
    n = NSA_GRP * tq
    tk = KEY_TILE
    kw_n = NSA_KV * _HD
    scale = _HD ** -0.5
    t0 = qi * tq
    lane = lax.broadcasted_iota(jnp.int32, (1, n), 1)
    tpos = t0 + (lane & (tq - 1))
    nkt = jnp.minimum((t0 + tq - 1) // tk + 1, nkt_max)
    wstart = pl.multiple_of(jnp.maximum(t0 - NSA_WINDOW, 0), LANES)
    wpos = wstart + lax.broadcasted_iota(jnp.int32, (wl, 1), 0)

    qus = _heads_t(qu_ref[...] * scale, tq, NSA_KV, NSA_GRP)
    qrs = [q.astype(BF16) for q in _heads_t(qr_ref[...] * scale, tq, NSA_KV, NSA_GRP)]
    gl_t = _sigmoid(ng_ref[...].T)
    gates = [[jnp.concatenate([gl_t[(g * NSA_GRP + r) * 3 + j:(g * NSA_GRP + r) * 3 + j + 1] for r in range(NSA_GRP)],
                              axis=1) for j in range(3)] for g in range(NSA_KV)]
    kcvc = kcvc_ref[0]
    win = win_ref[0, pl.ds(wstart, wl), :]

    o_cw = []
    for g in range(NSA_KV):
        hs = slice(g * _HD, (g + 1) * _HD)
        o_c, sel = _nsa_front(qus[g], kcvc[:, hs], kcvc[:, kw_n + g * _HD:kw_n + (g + 1) * _HD], mt_ref[...],
                              tpos, tq, nc, n_sel)
        sel_scr[g] = sel
        o_w = _nsa_window(qrs[g], win[:, hs], win[:, kw_n + g * _HD:kw_n + (g + 1) * _HD], tpos, wpos)
        o_cw.append(gates[g][0] * o_c + gates[g][2] * o_w)
    m_scr[...] = jnp.full(m_scr.shape, NEG, F32)
    l_scr[...] = jnp.zeros(l_scr.shape, F32)
    acc_scr[...] = jnp.zeros(acc_scr.shape, F32)
    qbd = _block_diag(qrs)

    def tile(j, causal):
        nb = tk // NSA_SEL_BLOCK
        rows = pl.ds(pl.multiple_of(j * tk, tk), tk)
        ksv = ksv_ref[0, rows, :]
        masks = []
        for g in range(NSA_KV):
            pieces = [jnp.broadcast_to(sel_scr[g, pl.ds(j * nb + u, 1), :], (NSA_SEL_BLOCK, tq)) for u in range(nb)]
            mk = jnp.concatenate(pieces, axis=0)
            masks.append(jnp.concatenate([mk] * NSA_GRP, axis=1) > 0.5)
        mask = jnp.concatenate(masks, axis=1)
        if causal:
            kpos = j * tk + lax.broadcasted_iota(jnp.int32, (tk, 1), 0)
            mask = mask & jnp.concatenate([kpos <= tpos] * NSA_KV, axis=1)
        vb = ksv[:, kw_n:].astype(BF16)
        _online_softmax(m_scr, l_scr, acc_scr, _dot(ksv[:, :kw_n].astype(BF16), qbd), mask,
                        lambda p: _tdot(vb, p), guard=False)

    def body(j, carry):
        tile(j, False)
        return carry

    lax.fori_loop(0, nkt - 1, body, 0)
    tile(nkt - 1, True)
    o_s = acc_scr[...] / jnp.maximum(l_scr[...], 1e-30)
    outs = [o_cw[g] + gates[g][1] * o_s[g * _HD:(g + 1) * _HD, g * n:(g + 1) * n] for g in range(NSA_KV)]
    _heads_untranspose(o_ref, outs, tq, NSA_GRP)


def _nsa_select_kernel(qu_ref, qr_ref, gl_ref, kc_ref, vc_ref, mt_ref, kw_ref, vw_ref,
                       o_ref, g1_ref, sel_ref, *, tq, nc, n_sel, qpos0, wpos0):
    n = NSA_GRP * tq
    lane = lax.broadcasted_iota(jnp.int32, (1, n), 1)
    tpos = qpos0 + (lane & (tq - 1))
    wpos = wpos0 + lax.broadcasted_iota(jnp.int32, (kw_ref.shape[2], 1), 0)
    for g in range(NSA_KV):
        qr = qr_ref[0, 0, g].astype(BF16)
        o_c, sel = _nsa_front(qu_ref[0, 0, g], kc_ref[0, g], vc_ref[0, g], mt_ref[...], tpos, tq, nc, n_sel)
        sel_ref[0, g] = sel
        o_w = _nsa_window(qr, kw_ref[0, g], vw_ref[0, g], tpos, wpos)
        gate = _sigmoid(gl_ref[0, 0, g])
        o_ref[0, g] = gate[0:1] * o_c + gate[2:3] * o_w
        g1_ref[0, g] = gate[1:2]


def _moba_select(q, kb, own, nbf):
    ni = lax.broadcasted_iota(jnp.int32, (kb.shape[0], q.shape[1]), 0)
    gs = _dot_hi(kb, q)
    gs = jnp.where(ni < own, gs, -BIG)
    gs = jnp.where(ni < nbf, gs, -3e38)
    sel = _top_rounds(gs, ni, min(MOBA_TOPK, nbf))
    return jnp.where((ni < own) & (ni < nbf), sel, 0.0)


def _moba_select_kernel(q_ref, kb_ref, sel_ref, *, own, nbf):
    for g in range(MOBA_KV):
        sel_ref[0, g] = _moba_select(q_ref[0, 0, g], kb_ref[0, g], own, nbf)


def _moba_kernel(q_ref, kb_ref, kmv_ref, o_ref, sel_scr, m_scr, l_scr, acc_scr, *, tq, nbf, nkt_max):
    qi = pl.program_id(1)
    n = MOBA_GRP * tq
    tk = KEY_TILE
    assert tk == MOBA_BLOCK
    t0 = qi * tq
    lane = lax.broadcasted_iota(jnp.int32, (1, MOBA_KV * n), 1)
    tpos = t0 + (lane & (tq - 1))
    own = t0 // MOBA_BLOCK
    scale = _HD ** -0.5

    qs = _heads_t(q_ref[...], tq, MOBA_KV, MOBA_GRP)
    for g in range(MOBA_KV):
        sel_scr[g] = _moba_select(qs[g], kb_ref[0, g], own, nbf)
    qbd = _block_diag([(q * scale).astype(BF16) for q in qs])
    m_scr[...] = jnp.full(m_scr.shape, NEG, F32)
    l_scr[...] = jnp.zeros(l_scr.shape, F32)
    acc_scr[...] = jnp.zeros(acc_scr.shape, F32)

    def tile(j, mask):
        kmv = kmv_ref[0, pl.ds(pl.multiple_of(j * tk, tk), tk), :]
        vb = kmv[:, _C_MK:].astype(BF16)
        _online_softmax(m_scr, l_scr, acc_scr, _dot(kmv[:, :_C_MK].astype(BF16), qbd), mask,
                        lambda p: _tdot(vb, p), guard=False)

    kpos = own * tk + lax.broadcasted_iota(jnp.int32, (tk, 1), 0)
    tile(own, kpos <= tpos)

    def body(j, carry):
        picked = jnp.concatenate([jnp.broadcast_to(sel_scr[g, pl.ds(j, 1), :], (tk, n)) for g in range(MOBA_KV)],
                                 axis=1)
        tile(j, picked > 0.5)
        return carry

    lax.fori_loop(0, jnp.minimum(own, nkt_max), body, 0)
    o = acc_scr[...] / jnp.maximum(l_scr[...], 1e-30)
    _heads_untranspose(o_ref, [o[g * _HD:(g + 1) * _HD, g * n:(g + 1) * n] for g in range(MOBA_KV)], tq, MOBA_GRP)


def _to_qt(x, batch, nt, tq, groups, rep):
    x = x.reshape(batch, nt, tq, groups, rep, _HD).transpose(0, 1, 3, 5, 4, 2)
    return x.reshape(batch, nt, groups, _HD, rep * tq)


def _from_qt(x, batch, nt, tq, groups, rep):
    x = x.reshape(batch, nt, groups, _HD, rep, tq).transpose(0, 1, 5, 2, 4, 3)
    return x.reshape(batch, nt * tq, groups * rep * _HD)


def _head_major(x, heads):
    b, l, _ = x.shape
    return x.reshape(b, l, heads, _HD).transpose(0, 2, 1, 3)


def _vt_tiles(x, heads, tile):
    b, l, _ = x.shape
    return x.reshape(b, l // tile, tile, heads, _HD).transpose(0, 3, 1, 4, 2)


def _sel_score_matrix(nsel_pad, nc_pad, nc):
    ratio = NSA_SEL_BLOCK // NSA_CMP_STRIDE
    span = NSA_CMP_LEN // NSA_CMP_STRIDE
    mt = np.zeros((nsel_pad, nc_pad), np.float32)
    for j in range(nsel_pad):
        for o in range(-(span - 1), ratio):
            c = j * ratio + o
            wt = sum(1 for m in range(ratio) for q in range(span) if m - q == o)
            if 0 <= c < nc:
                mt[j, c] = wt
    return jnp.asarray(mt)


def _mixer_c_attention(qu, qr, mq, ng, rows3, win3, kcvc, batch, seq):
    tk = KEY_TILE
    tq = LANES
    assert seq % tk == 0 and seq >= NSA_WINDOW + tq
    nt = seq // tq
    nkt_max = seq // tk
    nc = (seq - NSA_CMP_LEN) // NSA_CMP_STRIDE + 1
    nseg = kcvc.shape[1]
    n_sel = seq // NSA_SEL_BLOCK
    nsel_pad = -(-n_sel // 8) * 8
    nbf = seq // MOBA_BLOCK
    nbf_pad = -(-nbf // 8) * 8
    wl = NSA_WINDOW + tq
    mt = _sel_score_matrix(nsel_pad, nseg, nc)
    kw_n = NSA_KV * _HD

    kb = _block_mean(rows3, nbf, col_block=2)[:, :, 0, :]
    kb = _head_major(jnp.pad(kb, ((0, 0), (0, nbf_pad - nbf), (0, 0))), MOBA_KV)

    n_n = NSA_GRP * tq
    n_m = MOBA_GRP * tq
    tile = lambda wd: pl.BlockSpec((tq, wd), lambda b, i: (b * nt + i, 0))
    per_b = lambda shp, cb=0: pl.BlockSpec((1,) + shp, lambda b, i: (b,) + (0,) * (len(shp) - 1) + (cb,))
    o_n = pl.pallas_call(
        functools.partial(_nsa_kernel, tq=tq, nc=nc, n_sel=n_sel, nkt_max=nkt_max, wl=wl),
        grid=(batch, nt),
        in_specs=[tile(NSA_HEADS * _HD), tile(NSA_HEADS * _HD), tile(LANES), per_b((nseg, 2 * kw_n)), _full(mt.shape),
                  per_b((seq, 2 * kw_n), 1), per_b((seq, 2 * kw_n))],
        out_specs=tile(NSA_HEADS * _HD),
        out_shape=jax.ShapeDtypeStruct((batch * seq, NSA_HEADS * _HD), F32),
        scratch_shapes=[pltpu.VMEM((NSA_KV, nsel_pad, tq), F32), pltpu.VMEM((1, NSA_KV * n_n), F32),
                        pltpu.VMEM((1, NSA_KV * n_n), F32), pltpu.VMEM((kw_n, NSA_KV * n_n), F32)],
        compiler_params=_cp("parallel", "arbitrary"),
        name="nsa",
    )(qu, qr, ng, kcvc, mt, rows3, win3)

    o_m = pl.pallas_call(
        functools.partial(_moba_kernel, tq=tq, nbf=nbf, nkt_max=nkt_max),
        grid=(batch, nt),
        in_specs=[tile(MOBA_HEADS * _HD), per_b(kb.shape[1:]), per_b((seq, 2 * _C_MK), 1)],
        out_specs=tile(MOBA_HEADS * _HD),
        out_shape=jax.ShapeDtypeStruct((batch * seq, MOBA_HEADS * _HD), F32),
        scratch_shapes=[pltpu.VMEM((MOBA_KV, nbf_pad, n_m), F32), pltpu.VMEM((1, MOBA_KV * n_m), F32),
                        pltpu.VMEM((1, MOBA_KV * n_m), F32), pltpu.VMEM((_C_MK, MOBA_KV * n_m), F32)],
        compiler_params=_cp("parallel", "arbitrary"),
        name="moba",
    )(mq, kb, rows3)
    return o_n, o_m


PAGES_PER_STEP = 4
_SEG_PER_PAGE = PAGE_SIZE // NSA_CMP_STRIDE


def _pool_view(pool):
    n_pool, n_layers = pool.shape[0], pool.shape[1]
    return jnp.transpose(pool, (0, 1, 3, 4, 2)).reshape(n_pool * n_layers, ROW_W, PAGE_SIZE)


def _page_specs(npages, n_layers, layer, rows, row_block):
    def spec(pg):
        return pl.BlockSpec(
            (1, rows, PAGE_SIZE),
            lambda b, j, p: (p[b * npages + j * PAGES_PER_STEP + pg] * n_layers + layer, row_block, 0))
    return [spec(pg) for pg in range(PAGES_PER_STEP)]


def _scan_kernel(pt_ref, *refs):
    pps = PAGES_PER_STEP
    cmp_refs, mk_refs = refs[0:pps], refs[pps:2 * pps]
    flat_ref, kbar_ref, ck_scr, cv_scr = refs[2 * pps:]
    wd = 2 * _C_KV
    for pr in range(pps // 2):
        for q in range(2):
            x = cmp_refs[2 * pr + q][0].T
            ck_scr[q * PAGE_SIZE:(q + 1) * PAGE_SIZE, :] = x[:, :_C_KV]
            cv_scr[q * PAGE_SIZE:(q + 1) * PAGE_SIZE, :] = x[:, _C_KV:]
        r0 = 2 * pr * _SEG_PER_PAGE
        for i in range(NSA_CMP_STRIDE):
            c0 = i * wd
            flat_ref[0, r0:r0 + 2 * _SEG_PER_PAGE, c0:c0 + _C_KV] = \
                ck_scr[pl.ds(i, 2 * _SEG_PER_PAGE, stride=NSA_CMP_STRIDE), :].astype(BF16)
            flat_ref[0, r0:r0 + 2 * _SEG_PER_PAGE, c0 + _C_KV:c0 + wd] = \
                cv_scr[pl.ds(i, 2 * _SEG_PER_PAGE, stride=NSA_CMP_STRIDE), :].astype(BF16)
        ksum = (jnp.sum(mk_refs[2 * pr][0].T, axis=0, keepdims=True)
                + jnp.sum(mk_refs[2 * pr + 1][0].T, axis=0, keepdims=True))
        kbar_ref[0, pr] = ksum * (1.0 / MOBA_BLOCK)


def _scan_pages(poolv, pt, layer, n_layers):
    batch, npages = pt.shape
    assert npages % PAGES_PER_STEP == 0 and MOBA_BLOCK == 2 * PAGE_SIZE
    ng = npages // PAGES_PER_STEP
    nseg = npages * _SEG_PER_PAGE
    nbf = npages // 2
    wd = 2 * _C_KV
    gs = pltpu.PrefetchScalarGridSpec(
        num_scalar_prefetch=1,
        grid=(batch, ng),
        in_specs=_page_specs(npages, n_layers, layer, wd, 0) + _page_specs(npages, n_layers, layer, _C_MK, 2),
        out_specs=[pl.BlockSpec((1, PAGES_PER_STEP * _SEG_PER_PAGE, NSA_CMP_STRIDE * wd), lambda b, j, p: (b, j, 0)),
                   pl.BlockSpec((1, PAGES_PER_STEP // 2, 1, _C_MK), lambda b, j, p: (b, j, 0, 0))],
        scratch_shapes=[pltpu.VMEM((2 * PAGE_SIZE, _C_KV), F32), pltpu.VMEM((2 * PAGE_SIZE, _C_KV), F32)],
    )
    flat, kbar = pl.pallas_call(
        _scan_kernel,
        grid_spec=gs,
        out_shape=[jax.ShapeDtypeStruct((batch, nseg, NSA_CMP_STRIDE * wd), BF16),
                   jax.ShapeDtypeStruct((batch, nbf, 1, _C_MK), F32)],
        compiler_params=_cp("parallel", "arbitrary"),
        name="scan_pages",
    )(pt.reshape(-1), *([poolv] * (2 * PAGES_PER_STEP)))
    return flat, kbar[:, :, 0, :]


def _flash_update(m_ref, l_ref, acc_ref, k, v, q, mask):
    vb = v.astype(BF16)
    _online_softmax(m_ref, l_ref, acc_ref, _dot(k.astype(BF16), q), mask, lambda p: _tdot(vb, p))


def _dec_attn_kernel(pt_ref, *refs, past, nsteps, seq_pad):
    pps = PAGES_PER_STEP
    sel_refs = refs[0:pps]
    mob_refs = refs[pps:2 * pps]
    (qn_ref, qm_ref, smask_ref, mmask_ref, nsel_ref, nmob_ref, opart_ref, g1_ref,
     on_ref, om_ref, mn, ln, accn, mm, lm, accm) = refs[2 * pps:]
    j = pl.program_id(1)
    kw_n = NSA_KV * _HD
    kw_m = _C_MK

    @pl.when(j == 0)
    def _():
        mn[...] = jnp.full(mn.shape, NEG, F32)
        mm[...] = jnp.full(mm.shape, NEG, F32)
        ln[...] = jnp.zeros(ln.shape, F32)
        lm[...] = jnp.zeros(lm.shape, F32)
        accn[...] = jnp.zeros(accn.shape, F32)
        accm[...] = jnp.zeros(accm.shape, F32)

    qn = qn_ref[0]
    qm = qm_ref[0]
    rows = pps * PAGE_SIZE
    ksv = jnp.concatenate([r[0].T for r in sel_refs], axis=0)
    nb = rows // NSA_SEL_BLOCK
    srow = smask_ref[0, pl.ds(pl.multiple_of(j * nb, nb), nb), :]
    smask = jnp.concatenate([jnp.broadcast_to(srow[u:u + 1], (NSA_SEL_BLOCK, LANES)) for u in range(nb)], axis=0)
    _flash_update(mn, ln, accn, ksv[:, :kw_n], ksv[:, kw_n:], qn, smask > 0.5)

    kmv = jnp.concatenate([r[0].T for r in mob_refs], axis=0)
    nbm = rows // MOBA_BLOCK
    mmask = jnp.concatenate([jnp.broadcast_to(mmask_ref[0, pl.ds(j * nbm + u, 1), :], (MOBA_BLOCK, LANES))
                             for u in range(nbm)], axis=0)
    _flash_update(mm, lm, accm, kmv[:, :kw_m], kmv[:, kw_m:], qm, mmask > 0.5)

    @pl.when(j == nsteps - 1)
    def _():
        lane = lax.broadcasted_iota(jnp.int32, (1, LANES), 1)
        tpos = past + (lane & (seq_pad - 1))
        new_rows = nsel_ref.shape[1]
        kpos = past + lax.broadcasted_iota(jnp.int32, (new_rows, 1), 0)
        causal = kpos <= tpos
        new_blk = past // NSA_SEL_BLOCK
        nrow = jnp.broadcast_to(smask_ref[0, new_blk:new_blk + 1, :], (new_rows, LANES)) > 0.5
        nsel = nsel_ref[0]
        _flash_update(mn, ln, accn, nsel[:, :kw_n], nsel[:, kw_n:], qn, causal & nrow)
        nmob = nmob_ref[0]
        _flash_update(mm, lm, accm, nmob[:, :kw_m], nmob[:, kw_m:], qm, causal)
        on_ref[0] = opart_ref[0] + g1_ref[0] * (accn[...] / jnp.maximum(ln[...], 1e-30))
        om_ref[0] = accm[...] / jnp.maximum(lm[...], 1e-30)


def _block_diag_q(x, batch, seq, groups, rep):
    x = x.reshape(batch, seq, groups, rep, _HD).transpose(0, 2, 4, 3, 1).reshape(batch, groups, _HD, rep * seq)
    bd = jnp.einsum('bgdn,gh->bgdhn', x, jnp.eye(groups, dtype=x.dtype))
    bd = bd.reshape(batch, groups * _HD, groups * rep * seq)
    return jnp.pad(bd, ((0, 0), (0, 0), (0, LANES - bd.shape[2])))


def _diag_blocks(o, batch, seq, groups, rep):
    o = o[:, :, :groups * rep * seq].reshape(batch, groups, _HD, groups, rep, seq)
    o = jnp.stack([o[:, g, :, g] for g in range(groups)], axis=1)
    return o.transpose(0, 4, 1, 3, 2).reshape(batch * seq, groups * rep * _HD)


def _decode_attention(qu, qr, mq, ng, rows3, win3, pool, pt, cache_win, layer, wbig, pex, w2big, batch, seq):
    npages = pt.shape[1]
    past = npages * PAGE_SIZE
    lt = past + seq
    assert seq <= 8 and seq < NSA_CMP_STRIDE and past % MOBA_BLOCK == 0
    seq_pad = 8
    tq = 64
    scale = _HD ** -0.5

    n_layers = pool.shape[1]
    poolv = _pool_view(pool)
    cmpflat, kbar = _scan_pages(poolv, pt, layer, n_layers)
    kcvc = _compress(cmpflat, wbig, pex, w2big)
    nseg = kcvc.shape[1]
    nc = (lt - NSA_CMP_LEN) // NSA_CMP_STRIDE + 1
    assert nseg == nc + 1
    n_sel = -(-lt // NSA_SEL_BLOCK)
    nsel_pad = -(-n_sel // 8) * 8
    nbf = lt // MOBA_BLOCK
    nbf_pad = -(-(nbf + 1) // 8) * 8
    ng_steps = npages // PAGES_PER_STEP

    def padq(x):
        x = x.reshape(batch, seq, -1)
        return jnp.pad(x, ((0, 0), (0, tq - seq), (0, 0))).reshape(batch * tq, -1)

    qu_t = _to_qt(padq(qu) * scale, batch, 1, tq, NSA_KV, NSA_GRP)
    qr_t = _to_qt(padq(qr) * scale, batch, 1, tq, NSA_KV, NSA_GRP)
    mq_t = _to_qt(padq(mq), batch, 1, tq, MOBA_KV, MOBA_GRP)
    gl = padq(ng)[:, :NSA_HEADS * 3].reshape(batch, 1, tq, NSA_KV, NSA_GRP, 3).transpose(0, 1, 3, 5, 4, 2)
    gl = gl.reshape(batch, 1, NSA_KV, 3, NSA_GRP * tq)
    kc = _head_major(kcvc[:, :, 0:_C_KV], NSA_KV)
    vc = _head_major(kcvc[:, :, _C_KV:2 * _C_KV], NSA_KV)
    mt = _sel_score_matrix(nsel_pad, nseg, nc)
    wb = cache_win.reshape(batch, -1, 2 * _C_KV)
    wcat = jnp.concatenate([wb, win3], axis=1)
    wl = -(-wcat.shape[1] // LANES) * LANES
    wfull = jnp.pad(wcat, ((0, 0), (0, wl - wcat.shape[1]), (0, 0)))
    kw = _head_major(wfull[:, :, 0:_C_KV], NSA_KV).astype(BF16)
    vw = _head_major(wfull[:, :, _C_KV:], NSA_KV).astype(BF16)
    kb = _head_major(jnp.pad(kbar, ((0, 0), (0, nbf_pad - nbf), (0, 0))), MOBA_KV)

    n_n = NSA_GRP * tq
    n_m = MOBA_GRP * tq
    b1 = lambda shp: pl.BlockSpec((1,) + shp, lambda b: (b,) + (0,) * len(shp))
    o_part, g1, sel = pl.pallas_call(
        functools.partial(_nsa_select_kernel, tq=tq, nc=nc, n_sel=n_sel, qpos0=past, wpos0=past - wb.shape[1]),
        grid=(batch,),
        in_specs=[b1((1, NSA_KV, _HD, n_n)), b1((1, NSA_KV, _HD, n_n)), b1((1, NSA_KV, 3, n_n)),
                  b1(kc.shape[1:]), b1(vc.shape[1:]), _full(mt.shape), b1(kw.shape[1:]), b1(vw.shape[1:])],
        out_specs=[b1((NSA_KV, _HD, n_n)), b1((NSA_KV, 1, n_n)), b1((NSA_KV, nsel_pad, tq))],
        out_shape=[jax.ShapeDtypeStruct((batch, NSA_KV, _HD, n_n), F32),
                   jax.ShapeDtypeStruct((batch, NSA_KV, 1, n_n), F32),
                   jax.ShapeDtypeStruct((batch, NSA_KV, nsel_pad, tq), F32)],
        compiler_params=_cp("parallel"),
        name="nsa_select",
    )(qu_t, qr_t, gl, kc, vc, mt, kw, vw)
    msel = pl.pallas_call(
        functools.partial(_moba_select_kernel, own=past // MOBA_BLOCK, nbf=nbf),
        grid=(batch,),
        in_specs=[b1((1, MOBA_KV, _HD, n_m)), b1(kb.shape[1:])],
        out_specs=b1((MOBA_KV, nbf_pad, n_m)),
        out_shape=jax.ShapeDtypeStruct((batch, MOBA_KV, nbf_pad, n_m), F32),
        compiler_params=_cp("parallel"),
        name="moba_select",
    )(mq_t, kb)

    def lanes(x):
        return jnp.pad(x, [(0, 0)] * (x.ndim - 1) + [(0, LANES - x.shape[-1])])

    pad_s = lambda x: jnp.pad(x.reshape(batch, seq, -1), ((0, 0), (0, seq_pad - seq), (0, 0)))
    qn_bd = _block_diag_q(pad_s(qr).reshape(batch * seq_pad, -1) * scale, batch, seq_pad, NSA_KV, NSA_GRP).astype(BF16)
    qm_bd = _block_diag_q(pad_s(mq).reshape(batch * seq_pad, -1) * scale, batch, seq_pad, MOBA_KV, MOBA_GRP).astype(BF16)
    smask = jnp.broadcast_to(sel[:, :, :, None, :seq_pad], (batch, NSA_KV, nsel_pad, NSA_GRP, seq_pad))
    smask = lanes(smask.transpose(0, 2, 1, 3, 4).reshape(batch, nsel_pad, -1))
    mmask = msel.reshape(batch, MOBA_KV, nbf_pad, MOBA_GRP, tq)[..., :seq_pad]
    mmask = lanes(mmask.transpose(0, 2, 1, 3, 4).reshape(batch, nbf_pad, -1))
    op = o_part.reshape(batch, NSA_KV, _HD, NSA_GRP, tq)[..., :seq_pad]
    op_bd = jnp.einsum('bgdrt,gh->bgdhrt', op, jnp.eye(NSA_KV, dtype=F32))
    op_bd = lanes(op_bd.reshape(batch, NSA_KV * _HD, -1))
    g1l = g1.reshape(batch, NSA_KV, NSA_GRP, tq)[..., :seq_pad].reshape(batch, 1, -1)
    g1l = lanes(g1l)
    new_pad = 16
    pad_n = lambda x: jnp.pad(x, ((0, 0), (0, new_pad - seq), (0, 0)))
    new_sel = pad_n(rows3[:, :, 2 * _C_KV:4 * _C_KV])
    new_mob = pad_n(rows3[:, :, 4 * _C_KV:])

    kw_n = NSA_KV * _HD
    bj = lambda shp: pl.BlockSpec((1,) + shp, lambda b, j, p: (b,) + (0,) * len(shp))
    gs = pltpu.PrefetchScalarGridSpec(
        num_scalar_prefetch=1,
        grid=(batch, ng_steps),
        in_specs=(_page_specs(npages, n_layers, layer, 2 * kw_n, 1) + _page_specs(npages, n_layers, layer, 2 * _C_MK, 1)
                  + [bj((kw_n, LANES)), bj((_C_MK, LANES)), bj((nsel_pad, LANES)), bj((nbf_pad, LANES)),
                     bj((new_pad, 2 * kw_n)), bj((new_pad, 2 * _C_MK)), bj((kw_n, LANES)), bj((1, LANES))]),
        out_specs=[bj((kw_n, LANES)), bj((_C_MK, LANES))],
        scratch_shapes=[pltpu.VMEM((1, LANES), F32), pltpu.VMEM((1, LANES), F32), pltpu.VMEM((kw_n, LANES), F32),
                        pltpu.VMEM((1, LANES), F32), pltpu.VMEM((1, LANES), F32), pltpu.VMEM((_C_MK, LANES), F32)],
    )
    o_n, o_m = pl.pallas_call(
        functools.partial(_dec_attn_kernel, past=past, nsteps=ng_steps, seq_pad=seq_pad),
        grid_spec=gs,
        out_shape=[jax.ShapeDtypeStruct((batch, kw_n, LANES), F32), jax.ShapeDtypeStruct((batch, _C_MK, LANES), F32)],
        compiler_params=_cp("parallel", "arbitrary"),
        name="decode_attn",
    )(pt.reshape(-1), *([poolv] * (2 * PAGES_PER_STEP)), qn_bd, qm_bd, smask, mmask, new_sel, new_mob, op_bd, g1l)
    o_n = _diag_blocks(o_n, batch, seq_pad, NSA_KV, NSA_GRP).reshape(batch, seq_pad, -1)[:, :seq]
    o_m = _diag_blocks(o_m, batch, seq_pad, MOBA_KV, MOBA_GRP).reshape(batch, seq_pad, -1)[:, :seq]
    return o_n.reshape(batch * seq, -1), o_m.reshape(batch * seq, -1)


def _prep_weights(prm):
    w = {}
    w['w_in_a'] = prm['w_in_a'].astype(BF16)
    w['w_out_a'] = prm['w_out_a'].astype(BF16)
    w['w_out_c'] = prm['w_out_c'].astype(BF16)
    w['w_in_c'] = [_proj_c_weights(prm['w_in_c'][c]) for c in range(prm['w_in_c'].shape[0])]
    w['cmp'] = [_compress_weights(prm['cmp_w1_k'][c], prm['cmp_w1_v'][c], prm['cmp_w2_k'][c], prm['cmp_w2_v'][c],
                                  prm['cmp_pe_k'][c], prm['cmp_pe_v'][c]) for c in range(prm['w_in_c'].shape[0])]
    depth, d, _ = prm['router_c_w'].shape
    wr = jnp.concatenate([prm['router_c_w'], prm['router_f_w'].transpose(0, 2, 1, 3).reshape(depth, d, -1)], axis=2)
    w['router_w'] = jnp.pad(wr, ((0, 0), (0, 0), (0, LANES - wr.shape[2])))
    br = jnp.concatenate([prm['router_c_b'], prm['router_f_b'].reshape(depth, -1)], axis=1)
    w['router_b'] = jnp.pad(br, ((0, 0), (0, LANES - br.shape[1])))[:, None, :]
    w['moe_w1'] = prm['moe_w1']
    w['moe_w3'] = prm['moe_w3']
    w['moe_w2'] = prm['moe_w2']
    w['ple_w'] = prm['ple_w'].astype(BF16)
    w['ple_gate_w'] = prm['ple_gate_w'].astype(BF16)
    w['lb'] = jnp.cumsum(jax.nn.softmax(prm['hgrn_lb'].astype(F32), axis=0), axis=0)
    return w


def _forward(x, p, pos, prm, w, ctx):
    batch, seq, d = x.shape
    t = batch * seq
    depth = p.shape[0]
    h = x.reshape(t, d)
    outs = {}
    for i in range(depth):
        gmix = prm['norm_mix'][i][None, :]
        if i % 2 == 0:
            a = i // 2
            lb512 = jnp.tile(w['lb'][a], HG_HEADS)[None, :]
            q, k, lf, iv, sg, gu, vn = _proj_a(h, gmix, w['w_in_a'][a], lb512, prm['cm_vnorm'][a][None, :])
            if ctx is None:
                st0 = jnp.zeros((batch, HG_WIDTH, HG_WIDTH), F32)
            else:
                s0 = ctx['state_hgrn'][a].astype(F32)
                eye = jnp.eye(HG_HEADS, dtype=F32)
                st0 = jnp.einsum('bhde,hg->bhegd', s0, eye).reshape(batch, HG_WIDTH, HG_WIDTH)
            o, st = _hgrn(q, k, lf, iv, sg, st0, prm['hgrn_onorm'][a][None, :], batch, seq)
            st5 = st.reshape(batch, HG_HEADS, HG_DK, HG_HEADS, HG_DK)
            s_new = jnp.stack([st5[:, hh, :, hh, :] for hh in range(HG_HEADS)], axis=1).transpose(0, 1, 3, 2)
            cm = _cmix(vn, gu, prm['cm_ws'][a], prm['cm_bs'][a].T, batch, seq)
            h = _out_proj(h, o, cm, w['w_out_a'][a])
            outs.setdefault('hg', []).append(s_new)
            outs.setdefault('cm', []).append(vn.reshape(batch, seq, CM_WIDTH))
        else:
            c = i // 2
            qu, qr, mq, ng, rows, win, *flat = _proj_c(h, gmix, w['w_in_c'][c], pos, seq, want_flat=ctx is None)
            rows3 = rows.reshape(batch, seq, ROW_W)
            win3 = win.reshape(batch, seq, 2 * _C_KV)
            wbig, pex, w2big = w['cmp'][c]
            if ctx is None:
                assert seq % KEY_TILE == 0
                nseg = seq // NSA_CMP_STRIDE
                kcvc = _compress(flat[0].reshape(batch, nseg, -1), wbig, pex, w2big)
                o_n, o_m = _mixer_c_attention(qu, qr, mq, ng, rows3, win3, kcvc, batch, seq)
                new_win = win3[:, seq - min(NSA_WINDOW, seq):]
                kv_rows = flat[1].reshape(batch, KV_ROW_HEADS, _HD, seq).transpose(0, 3, 1, 2)
            else:
                o_n, o_m = _decode_attention(qu, qr, mq, ng, rows3, win3, ctx['cache_kv'], ctx['page_table'],
                                             ctx['cache_win'][c], c, wbig, pex, w2big, batch, seq)
                new_win = win3
                kv_rows = rows3.reshape(batch, seq, KV_ROW_HEADS, _HD)
            h = _out_proj(h, o_n, o_m, w['w_out_c'][c])
            outs.setdefault('kv', []).append(kv_rows)
            outs.setdefault('win', []).append(new_win.reshape(batch, -1, 2 * NSA_KV, _HD))
        y = _moe(h, prm['norm_ffn'][i][None, :], w['router_w'][i], w['router_b'][i],
                 w['moe_w1'], w['moe_w3'], w['moe_w2'], i)
        h = _ple(h, y, p[i].reshape(t, -1), prm['norm_ple'][i][None, :], w['ple_gate_w'][i], w['ple_w'][i],
                 prm['final_norm'][None, :], final=(i == depth - 1))
    return h.reshape(batch, seq, d), outs


def kernel(x_prompt, x_sample, cache_kv, cache_win, state_hgrn, page_table, p_prompt, p_sample,
           norm_mix, norm_ffn, norm_ple, final_norm, w_in_a, w_out_a, hgrn_lb, hgrn_onorm, cm_vnorm,
           cm_ws, cm_bs, w_in_c, w_out_c, cmp_pe_k, cmp_w1_k, cmp_w2_k, cmp_pe_v, cmp_w1_v, cmp_w2_v,
           router_c_w, router_c_b, router_f_w, router_f_b, moe_w1, moe_w3, moe_w2, ple_w, ple_gate_w):
    prm = dict(norm_mix=norm_mix, norm_ffn=norm_ffn, norm_ple=norm_ple, final_norm=final_norm,
               w_in_a=w_in_a, w_out_a=w_out_a, hgrn_lb=hgrn_lb, hgrn_onorm=hgrn_onorm, cm_vnorm=cm_vnorm,
               cm_ws=cm_ws, cm_bs=cm_bs, w_in_c=w_in_c, w_out_c=w_out_c, cmp_pe_k=cmp_pe_k,
               cmp_w1_k=cmp_w1_k, cmp_w2_k=cmp_w2_k, cmp_pe_v=cmp_pe_v, cmp_w1_v=cmp_w1_v, cmp_w2_v=cmp_w2_v,
               router_c_w=router_c_w, router_c_b=router_c_b, router_f_w=router_f_w, router_f_b=router_f_b,
               moe_w1=moe_w1, moe_w3=moe_w3, moe_w2=moe_w2, ple_w=ple_w, ple_gate_w=ple_gate_w)
    w = _prep_weights(prm)
    past = page_table.shape[1] * PAGE_SIZE
    y_p, o_p = _forward(x_prompt, p_prompt, jnp.arange(x_prompt.shape[1]), prm, w, None)
    ctx = dict(cache_kv=cache_kv, cache_win=cache_win, state_hgrn=state_hgrn, page_table=page_table)
    y_s, o_s = _forward(x_sample, p_sample, past + jnp.arange(x_sample.shape[1]), prm, w, ctx)
    return (y_p, y_s,
            jnp.stack(o_p['kv'], axis=1), jnp.stack(o_s['kv'], axis=1),
            jnp.stack(o_p['win'], axis=0), jnp.stack(o_s['win'], axis=0),
            jnp.stack(o_p['hg'], axis=0), jnp.stack(o_s['hg'], axis=0),
            jnp.stack(o_s['cm'], axis=0))
```

```python
import functools
import math

import numpy as np
import jax
import jax.numpy as jnp
from jax import lax
from jax.experimental import pallas as pl
from jax.experimental.pallas import tpu as pltpu

F32 = jnp.float32
BF16 = jnp.bfloat16
HI = lax.Precision.HIGHEST

PAGE_SIZE = 128
HEAD_DIM = 64
ROT_DIM = HEAD_DIM // 4
ROPE_THETA = 500000.0
HG_HEADS = 8
HG_DK = 64
HG_WIDTH = HG_HEADS * HG_DK
HG_CHUNK = 64
CM_GROUPS = 4
CM_GW = 128
CM_WIDTH = CM_GROUPS * CM_GW
CM_CHUNK = 128
NSA_HEADS = 8
NSA_KV = 2
NSA_GRP = NSA_HEADS // NSA_KV
NSA_CMP_LEN = 32
NSA_CMP_STRIDE = 16
NSA_SEL_BLOCK = 64
NSA_TOPN = 16
NSA_WINDOW = 512
MOBA_HEADS = 8
MOBA_KV = 4
MOBA_GRP = MOBA_HEADS // MOBA_KV
MOBA_BLOCK = 256
MOBA_TOPK = 3
MOE_GROUPS = 4
MOE_EPG = 8
MOE_EXPERTS = MOE_GROUPS * MOE_EPG
KV_ROW_HEADS = 16
ROW_W = KV_ROW_HEADS * HEAD_DIM
A_SPLITS = [HG_WIDTH] * 4 + [CM_WIDTH] * 2
C_SPLITS = [NSA_HEADS * HEAD_DIM] + [NSA_KV * HEAD_DIM] * 6 + [NSA_HEADS * 3, MOBA_HEADS * HEAD_DIM,
                                                              MOBA_KV * HEAD_DIM, MOBA_KV * HEAD_DIM]
NEG = -1e30
BIG = 1e9
EPS = 1e-6
KEY_TILE = 256
LANES = 128
_MOBA_SHIFT = MOBA_BLOCK.bit_length() - 1
_SEL_SHIFT = NSA_SEL_BLOCK.bit_length() - 1
VMEM_LIMIT = 56 * 1024 * 1024


def _cp(*sem):
    return pltpu.CompilerParams(dimension_semantics=sem, vmem_limit_bytes=VMEM_LIMIT)


def _sigmoid(x):
    return 1.0 / (1.0 + jnp.exp(-x))


def _silu(x):
    return x * _sigmoid(x)


def _gelu(x):
    return 0.5 * x * (1.0 + jnp.tanh(math.sqrt(2.0 / math.pi) * (x + 0.044715 * (x * x * x))))


def _rms(x, g):
    return x * lax.rsqrt(jnp.mean(x * x, axis=-1, keepdims=True) + EPS) * g


def _dot(a, b):
    return jnp.dot(a, b, preferred_element_type=F32)


def _dot_hi(a, b):
    return jnp.dot(a, b, precision=HI, preferred_element_type=F32)


def _dot3(a, b):
    a_hi = a.astype(BF16)
    b_hi = b.astype(BF16)
    a_lo = (a - a_hi.astype(F32)).astype(BF16)
    b_lo = (b - b_hi.astype(F32)).astype(BF16)
    return _dot(a_hi, b_hi) + (_dot(a_hi, b_lo) + _dot(a_lo, b_hi))


def _full(shape):
    n = len(shape)
    return pl.BlockSpec(shape, lambda *_: (0,) * n)


def _row_tile(t):
    for tm in (256, 128, 64, 32, 16, 8):
        if t % tm == 0:
            return tm
    raise ValueError(f"token count {t} is not a multiple of 8")


def _proj_a_kernel(x_ref, g_ref, w_ref, lb_ref, vg_ref, q_ref, k_ref, lf_ref, iv_ref, sg_ref, gu_ref, vn_ref):
    xn = _rms(x_ref[...], g_ref[...]).astype(BF16)
    z = _dot(xn, w_ref[...])
    w = HG_WIDTH
    q_ref[...] = _silu(z[:, 0:w])
    lb = lb_ref[...]
    f = lb + (1.0 - lb) * _sigmoid(z[:, w:2 * w])
    k_ref[...] = 1.0 - f
    lf_ref[...] = jnp.log(f)
    iv_ref[...] = z[:, 2 * w:3 * w]
    sg_ref[...] = _silu(z[:, 3 * w:4 * w])
    gu_ref[...] = _gelu(z[:, 4 * w:4 * w + CM_WIDTH])
    v = _gelu(z[:, 4 * w + CM_WIDTH:])
    for gi in range(CM_GROUPS):
        sl = slice(gi * CM_GW, (gi + 1) * CM_GW)
        vn_ref[:, sl] = _rms(v[:, sl], vg_ref[:, sl])


def _proj_a(x, g, w_bf, lb512, vgain):
    t, d = x.shape
    tm = _row_tile(t)
    n = w_bf.shape[1]
    row = lambda wd: pl.BlockSpec((tm, wd), lambda i: (i, 0))
    outs = [jax.ShapeDtypeStruct((t, HG_WIDTH), F32)] * 5 + [jax.ShapeDtypeStruct((t, CM_WIDTH), F32)] * 2
    return pl.pallas_call(
        _proj_a_kernel,
        grid=(t // tm,),
        in_specs=[row(d), _full((1, d)), _full((d, n)), _full((1, HG_WIDTH)), _full((1, CM_WIDTH))],
        out_specs=[row(HG_WIDTH)] * 5 + [row(CM_WIDTH)] * 2,
        out_shape=outs,
        compiler_params=_cp("parallel"),
        name="proj_a",
    )(x, g, w_bf, lb512, vgain)


def _hgrn_kernel(q_ref, k_ref, lf_ref, v_ref, sg_ref, s0_ref, gain_ref, bones_ref, bmask_ref, tri_ref,
                 o_ref, sout_ref, st_scr, p_scr, *, chunk):
    c = pl.program_id(1)
    w = HG_WIDTH
    hw = w // 2

    @pl.when(c == 0)
    def _():
        st_scr[...] = s0_ref[0]

    q = q_ref[0]
    k = k_ref[0]
    v = v_ref[0]
    b = _dot_hi(tri_ref[...], lf_ref[0])
    bones = bones_ref[...]

    def head_sum(x):
        xb = x.astype(BF16)
        return jnp.concatenate([_dot(xb[:, :hw], bones), _dot(xb[:, hw:], bones)], axis=1)

    o_rows = []
    for blk in range(chunk // 8):
        s_len = 8 * (blk + 1)
        bs = b[:s_len]
        ks = k[:s_len]
        vs = v[:s_len]
        row_id = lax.broadcasted_iota(jnp.int32, (s_len, w), 0)
        for t in range(8):
            r = 8 * blk + t
            diff = jnp.where(row_id <= r, b[r:r + 1, :] - bs, NEG)
            p_scr[t * s_len:(t + 1) * s_len, :] = jnp.exp(diff) * q[r:r + 1, :] * ks
        att = head_sum(p_scr[0:8 * s_len, :])
        for t in range(8):
            o_rows.append(jnp.sum(att[t * s_len:(t + 1) * s_len] * vs, axis=0, keepdims=True))
    o_intra = jnp.concatenate(o_rows, axis=0)

    st = st_scr[...]
    qe = (q * jnp.exp(b)).astype(BF16)
    o = o_intra + lax.dot_general(qe, st.astype(BF16), (((1,), (1,)), ((), ())), preferred_element_type=F32)

    b_end = b[chunk - 1:chunk, :]
    kd = (k * jnp.exp(b_end - b)).astype(BF16)
    upd = lax.dot_general(v.astype(BF16), kd, (((0,), (0,)), ((), ())), preferred_element_type=F32)
    st_new = st * jnp.exp(b_end) + upd * bmask_ref[...]
    st_scr[...] = st_new

    o2 = o * o
    hi = o2.astype(BF16).astype(F32)
    ms = (head_sum(hi) + head_sum(o2 - hi)) * (1.0 / HG_DK)
    o_ref[0] = o * lax.rsqrt(ms + EPS) * gain_ref[...] * sg_ref[0]

    @pl.when(c == pl.num_programs(1) - 1)
    def _():
        sout_ref[0] = st_new


def _block_ones(n, blk, dtype):
    i = np.arange(n) // blk
    return jnp.asarray((i[:, None] == i[None, :]).astype(np.float32), dtype)


def _hgrn(q, k, lf, v, sg, st0, gain, batch, seq):
    w = HG_WIDTH
    chunk = math.gcd(seq, HG_CHUNK)
    nch = seq // chunk
    r3 = lambda a: a.reshape(batch, seq, w)
    tile = pl.BlockSpec((1, chunk, w), lambda b, c: (b, c, 0))
    state = pl.BlockSpec((1, w, w), lambda b, c: (b, 0, 0))
    tri = jnp.asarray(np.tril(np.ones((chunk, chunk), np.float32)))
    o, st = pl.pallas_call(
        functools.partial(_hgrn_kernel, chunk=chunk),
        grid=(batch, nch),
        in_specs=[tile] * 5 + [state, _full((1, w)), _full((w // 2, w // 2)), _full((w, w)), _full((chunk, chunk))],
        out_specs=[tile, state],
        out_shape=[jax.ShapeDtypeStruct((batch, seq, w), F32), jax.ShapeDtypeStruct((batch, w, w), F32)],
        scratch_shapes=[pltpu.VMEM((w, w), F32), pltpu.VMEM((8 * chunk, w), F32)],
        compiler_params=_cp("parallel", "arbitrary"),
        name="hgrn",
    )(r3(q), r3(k), r3(lf), r3(v), r3(sg), st0, gain, _block_ones(w // 2, HG_DK, BF16),
      _block_ones(w, HG_DK, F32), tri)
    return o.reshape(batch * seq, w), st


def _cmix_kernel(vn_ref, gu_ref, ws_ref, bst_ref, o_ref, *, tl):
    ri = lax.broadcasted_iota(jnp.int32, (tl, tl), 0)
    ci = lax.broadcasted_iota(jnp.int32, (tl, tl), 1)
    for gi in range(CM_GROUPS):
        sl = slice(gi * CM_GW, (gi + 1) * CM_GW)
        wg = jnp.where(ci <= ri, ws_ref[gi, 0:tl, 0:tl], 0.0).astype(BF16)
        mix = _dot(wg, vn_ref[0, :, sl].astype(BF16)) + bst_ref[0:tl, gi:gi + 1]
        o_ref[0, :, sl] = gu_ref[0, :, sl] * mix


def _cmix(vn, gu, ws, bst, batch, seq):
    tl = min(seq, CM_CHUNK)
    assert seq % tl == 0 and tl % 8 == 0
    w = CM_WIDTH
    tile = pl.BlockSpec((1, tl, w), lambda b, c: (b, c, 0))
    out = pl.pallas_call(
        functools.partial(_cmix_kernel, tl=tl),
        grid=(batch, seq // tl),
        in_specs=[tile, tile, _full(ws.shape), _full(bst.shape)],
        out_specs=tile,
        out_shape=jax.ShapeDtypeStruct((batch, seq, w), F32),
        compiler_params=_cp("parallel", "parallel"),
        name="cmix",
    )(vn.reshape(batch, seq, w), gu.reshape(batch, seq, w), ws, bst)
    return out.reshape(batch * seq, w)


def _out_proj_kernel(res_ref, a1_ref, a2_ref, w1_ref, w2_ref, o_ref):
    o_ref[...] = (res_ref[...] + _dot(a1_ref[...].astype(BF16), w1_ref[...])
                  + _dot(a2_ref[...].astype(BF16), w2_ref[...]))


def _out_proj(res, a1, a2, w_bf):
    t, d = res.shape
    k1, k2 = a1.shape[1], a2.shape[1]
    tm = _row_tile(t)
    row = lambda wd: pl.BlockSpec((tm, wd), lambda i: (i, 0))
    return pl.pallas_call(
        _out_proj_kernel,
        grid=(t // tm,),
        in_specs=[row(d), row(k1), row(k2), _full((k1, d)), _full((k2, d))],
        out_specs=row(d),
        out_shape=jax.ShapeDtypeStruct((t, d), F32),
        compiler_params=_cp("parallel"),
        name="out_proj",
    )(res, a1, a2, w_bf[:k1], w_bf[k1:])


def _router_kernel(h_ref, g_ref, w_ref, b_ref, xn_ref, info_ref):
    xn = _rms(h_ref[...], g_ref[...])
    xn_ref[...] = xn.astype(BF16)
    logit = _dot3(xn, w_ref[...]) + b_ref[...]
    lane = lax.broadcasted_iota(jnp.int32, logit.shape, 1)
    far = 1 << 20

    def first_max(vals, mask):
        m = jnp.max(jnp.where(mask, vals, -1.0), axis=-1, keepdims=True)
        idx = jnp.min(jnp.where(mask & (vals == m), lane, far), axis=-1, keepdims=True)
        return m, idx

    def softmax(mask):
        m = jnp.max(jnp.where(mask, logit, NEG), axis=-1, keepdims=True)
        e = jnp.where(mask, jnp.exp(logit - m), 0.0)
        return e / jnp.sum(e, axis=-1, keepdims=True)

    cmask = lane < MOE_GROUPS
    pg, grp = first_max(softmax(cmask), cmask)
    lo = MOE_GROUPS + grp * MOE_EPG
    fmask = (lane >= lo) & (lane < lo + MOE_EPG)
    pf = softmax(fmask)
    v1, i1 = first_max(pf, fmask)
    v2, i2 = first_max(pf, fmask & (lane != i1))
    den = v1 + v2
    info = jnp.where(lane == 0, (i1 - MOE_GROUPS).astype(F32), 0.0)
    info = jnp.where(lane == 1, (i2 - MOE_GROUPS).astype(F32), info)
    info = jnp.where(lane == 2, pg * v1 / den, info)
    info = jnp.where(lane == 3, pg * v2 / den, info)
    info_ref[...] = info


def _router(h, g, w_r, b_r):
    t, d = h.shape
    tm = _row_tile(t)
    row = lambda wd: pl.BlockSpec((tm, wd), lambda i: (i, 0))
    return pl.pallas_call(
        _router_kernel,
        grid=(t // tm,),
        in_specs=[row(d), _full((1, d)), _full((d, LANES)), _full((1, LANES))],
        out_specs=[row(d), row(LANES)],
        out_shape=[jax.ShapeDtypeStruct((t, d), BF16), jax.ShapeDtypeStruct((t, LANES), F32)],
        compiler_params=_cp("parallel"),
        name="router",
    )(h, g, w_r, b_r)


def _moe_ffn_kernel(be_ref, nu_ref, x_ref, w1_ref, w3_ref, w2_ref, o_ref, w1b, w3b, w2b):
    i = pl.program_id(0)

    @pl.when((i == 0) | (be_ref[i] != be_ref[jnp.maximum(i - 1, 0)]))
    def _():
        w1b[...] = w1_ref[0, 0].astype(BF16)
        w3b[...] = w3_ref[0, 0].astype(BF16)
        w2b[...] = w2_ref[0, 0].astype(BF16)

    @pl.when(i < nu_ref[0])
    def _():
        x = x_ref[...]
        hdn = _silu(_dot(x, w1b[...])) * _dot(x, w3b[...])
        o_ref[...] = _dot(hdn.astype(BF16), w2b[...])

    @pl.when(i >= nu_ref[0])
    def _():
        o_ref[...] = jnp.zeros_like(o_ref)


def _moe_ffn(xp, blk_e, nused, w1, w3, w2, layer, rb):
    n, d = xp.shape
    ff = w1.shape[3]
    nblk = n // rb
    gs = pltpu.PrefetchScalarGridSpec(
        num_scalar_prefetch=2,
        grid=(nblk,),
        in_specs=[pl.BlockSpec((rb, d), lambda i, be, nu: (i, 0)),
                  pl.BlockSpec((1, 1, d, ff), lambda i, be, nu: (layer, be[i], 0, 0)),
                  pl.BlockSpec((1, 1, d, ff), lambda i, be, nu: (layer, be[i], 0, 0)),
                  pl.BlockSpec((1, 1, ff, d), lambda i, be, nu: (layer, be[i], 0, 0))],
        out_specs=pl.BlockSpec((rb, d), lambda i, be, nu: (i, 0)),
        scratch_shapes=[pltpu.VMEM((d, ff), BF16), pltpu.VMEM((d, ff), BF16), pltpu.VMEM((ff, d), BF16)],
    )
    return pl.pallas_call(
        _moe_ffn_kernel,
        grid_spec=gs,
        out_shape=jax.ShapeDtypeStruct((n, d), F32),
        compiler_params=_cp("arbitrary"),
        name="moe_ffn",
    )(blk_e, nused, xp, w1, w3, w2)


def _moe(h, g, w_r, b_r, w1, w3, w2, layer):
    t, d = h.shape
    xn, info = _router(h, g, w_r, b_r)
    eid = info[:, 0:2].astype(jnp.int32)
    gate = info[:, 2:4]
    ne = MOE_EXPERTS
    tk = 2 * t
    rb = 256 if t >= 4096 else 32
    flat = eid.reshape(-1)
    order = jnp.argsort(flat).astype(jnp.int32)
    rank = jnp.argsort(order).astype(jnp.int32)
    counts = jnp.sum((flat[:, None] == jnp.arange(ne)[None, :]).astype(jnp.int32), axis=0)
    padc = (counts + rb - 1) // rb * rb
    pend = jnp.cumsum(padc)
    pstart = pend - padc
    cstart = jnp.cumsum(counts) - counts
    slot = (pstart[flat] + rank - cstart[flat]).reshape(t, 2)
    nblk = -(-tk // rb) + ne
    blk_e = jnp.minimum(jnp.sum((pend[None, :] <= (jnp.arange(nblk) * rb)[:, None]).astype(jnp.int32), axis=1),
                        ne - 1).astype(jnp.int32)
    pe = jnp.repeat(blk_e, rb)
    off = jnp.arange(nblk * rb) - pstart[pe]
    rows = jnp.where(off < counts[pe], order[jnp.clip(off + cstart[pe], 0, tk - 1)] // 2, 0)
    xp = xn[rows]
    nused = (pend[-1:] // rb).astype(jnp.int32)
    yp = _moe_ffn(xp, blk_e, nused, w1, w3, w2, layer, rb)
    return gate[:, 0:1] * yp[slot[:, 0]] + gate[:, 1:2] * yp[slot[:, 1]]


def _ple_kernel(h_ref, y_ref, p_ref, g_ref, wg_ref, wp_ref, fg_ref, o_ref, *, final):
    h = h_ref[...] + y_ref[...]
    gate = _sigmoid(_dot(_rms(h, g_ref[...]).astype(BF16), wg_ref[...]))
    out = h + gate * _dot(p_ref[...].astype(BF16), wp_ref[...])
    if final:
        out = _rms(out, fg_ref[...])
    o_ref[...] = out


def _ple(h, y, p, g, wg_bf, wp_bf, fg, final):
    t, d = h.shape
    pd = p.shape[1]
    tm = _row_tile(t)
    row = lambda wd: pl.BlockSpec((tm, wd), lambda i: (i, 0))
    return pl.pallas_call(
        functools.partial(_ple_kernel, final=final),
        grid=(t // tm,),
        in_specs=[row(d), row(d), row(pd), _full((1, d)), _full((d, d)), _full((pd, d)), _full((1, d))],
        out_specs=row(d),
        out_shape=jax.ShapeDtypeStruct((t, d), F32),
        compiler_params=_cp("parallel"),
        name="ple",
    )(h, y, p, g, wg_bf, wp_bf, fg)


_HD = HEAD_DIM
_C_NQ = NSA_HEADS * _HD
_C_KV = NSA_KV * _HD
_C_MQ = MOBA_HEADS * _HD
_C_MK = MOBA_KV * _HD
_O_NQ = 0
_O_CK = _O_NQ + _C_NQ
_O_CV = _O_CK + _C_KV
_O_SK = _O_CV + _C_KV
_O_SV = _O_SK + _C_KV
_O_WK = _O_SV + _C_KV
_O_WV = _O_WK + _C_KV
_O_MQ = _O_WV + _C_KV
_O_MK = _O_MQ + _C_MQ
_O_MV = _O_MK + _C_MK
_O_NG = _O_MV + _C_MK
_O_NQS = _O_NG + LANES
_O_SKS = _O_NQS + _C_NQ
_O_WKS = _O_SKS + _C_KV
_O_MQS = _O_WKS + _C_KV
_O_MKS = _O_MQS + _C_MQ
_C_TOTAL = _O_MKS + _C_MK


def _proj_c_kernel(x_ref, g_ref, w_ref, cos_ref, sin_ref, qu_ref, qr_ref, mq_ref, ng_ref, rows_ref, win_ref,
                   *flat_refs):
    xn = _rms(x_ref[...], g_ref[...]).astype(BF16)
    z = _dot(xn, w_ref[...])
    cs = cos_ref[...]
    sn = sin_ref[...]

    def rope(o, os, wd):
        reps = wd // LANES
        return z[:, o:o + wd] * jnp.tile(cs, (1, reps)) + z[:, os:os + wd] * jnp.tile(sn, (1, reps))

    qu_ref[...] = z[:, _O_NQ:_O_NQ + _C_NQ]
    qr_ref[...] = rope(_O_NQ, _O_NQS, _C_NQ)
    mq_ref[...] = rope(_O_MQ, _O_MQS, _C_MQ)
    ng_ref[...] = z[:, _O_NG:_O_NG + LANES]
    rows_ref[:, 0:2 * _C_KV] = z[:, _O_CK:_O_CK + 2 * _C_KV]
    rows_ref[:, 2 * _C_KV:3 * _C_KV] = rope(_O_SK, _O_SKS, _C_KV)
    rows_ref[:, 3 * _C_KV:4 * _C_KV] = z[:, _O_SV:_O_SV + _C_KV]
    rows_ref[:, 4 * _C_KV:4 * _C_KV + _C_MK] = rope(_O_MK, _O_MKS, _C_MK)
    rows_ref[:, 4 * _C_KV + _C_MK:] = z[:, _O_MV:_O_MV + _C_MK]
    win_ref[:, 0:_C_KV] = rope(_O_WK, _O_WKS, _C_KV)
    win_ref[:, _C_KV:] = z[:, _O_WV:_O_WV + _C_KV]
    if not flat_refs:
        return
    flat_ref, rows_t_ref, ck_scr, cv_scr = flat_refs
    rows_t_ref[0] = rows_ref[...].T
    ck_scr[...] = z[:, _O_CK:_O_CK + _C_KV]
    cv_scr[...] = z[:, _O_CV:_O_CV + _C_KV]
    nseg = flat_ref.shape[0]
    for i in range(NSA_CMP_STRIDE):
        c0 = i * 2 * _C_KV
        flat_ref[:, c0:c0 + _C_KV] = ck_scr[pl.ds(i, nseg, stride=NSA_CMP_STRIDE), :].astype(BF16)
        flat_ref[:, c0 + _C_KV:c0 + 2 * _C_KV] = cv_scr[pl.ds(i, nseg, stride=NSA_CMP_STRIDE), :].astype(BF16)


def _proj_c_weights(w):
    offs = np.concatenate([[0], np.cumsum(C_SPLITS)])
    nq, ck, cv, sk, sv, wk, wv, ng, mq, mk, mv = [w[:, offs[i]:offs[i + 1]] for i in range(len(C_SPLITS))]
    half = ROT_DIM // 2

    def swapped(m):
        d = m.shape[0]
        m3 = m.reshape(d, -1, _HD)
        out = jnp.concatenate([m3[..., half:ROT_DIM], m3[..., :half], jnp.zeros_like(m3[..., ROT_DIM:])], axis=-1)
        return out.reshape(d, -1)

    ngp = jnp.pad(ng, ((0, 0), (0, LANES - ng.shape[1])))
    return jnp.concatenate([nq, ck, cv, sk, sv, wk, wv, mq, mk, mv, ngp,
                            swapped(nq), swapped(sk), swapped(wk), swapped(mq), swapped(mk)], axis=1).astype(BF16)


def _rope_tables(pos):
    half = ROT_DIM // 2
    inv = ROPE_THETA ** (-jnp.arange(half, dtype=F32) / half)
    ang = pos.astype(F32)[:, None] * inv
    cos, sin = jnp.cos(ang), jnp.sin(ang)
    n = pos.shape[0]
    c64 = jnp.concatenate([cos, cos, jnp.ones((n, _HD - ROT_DIM), F32)], axis=1)
    s64 = jnp.concatenate([-sin, sin, jnp.zeros((n, _HD - ROT_DIM), F32)], axis=1)
    return jnp.tile(c64, (1, LANES // _HD)), jnp.tile(s64, (1, LANES // _HD))


def _proj_c(x, g, wc_bf, pos, seq, want_flat):
    t, d = x.shape
    tm = _row_tile(t)
    tr = max(seq, tm)
    assert tr % tm == 0 and tr % seq == 0
    cs, sn = _rope_tables(jnp.tile(pos, tr // seq))
    ntab = tr // tm
    row = lambda wd: pl.BlockSpec((tm, wd), lambda i: (i, 0))
    tab = pl.BlockSpec((tm, LANES), lambda i: (i % ntab, 0))
    widths = [_C_NQ, _C_NQ, _C_MQ, LANES, ROW_W, 2 * _C_KV]
    st = NSA_CMP_STRIDE
    fw = st * 2 * _C_KV
    out_specs = [row(wd) for wd in widths]
    out_shape = [jax.ShapeDtypeStruct((t, wd), F32) for wd in widths]
    scratch = []
    if want_flat:
        assert tm % (16 * st) == 0 and seq % tm == 0
        tpb = seq // tm
        out_specs.append(pl.BlockSpec((tm // st, fw), lambda i: (i, 0)))
        out_shape.append(jax.ShapeDtypeStruct((t // st, fw), BF16))
        out_specs.append(pl.BlockSpec((1, ROW_W, tm), lambda i: (i // tpb, 0, i % tpb)))
        out_shape.append(jax.ShapeDtypeStruct((t // seq, ROW_W, seq), F32))
        scratch = [pltpu.VMEM((tm, _C_KV), F32), pltpu.VMEM((tm, _C_KV), F32)]
    return pl.pallas_call(
        _proj_c_kernel,
        grid=(t // tm,),
        in_specs=[row(d), _full((1, d)), _full((d, _C_TOTAL)), tab, tab],
        out_specs=out_specs,
        out_shape=out_shape,
        scratch_shapes=scratch,
        compiler_params=_cp("parallel"),
        name="proj_c",
    )(x, g, wc_bf, cs, sn)


def _mm_kernel(a_ref, w_ref, o_ref):
    o_ref[...] = _dot(a_ref[...], w_ref[...])


def _mm(a_bf, w_bf):
    m, k = a_bf.shape
    n = w_bf.shape[1]
    tm = _row_tile(m)
    return pl.pallas_call(
        _mm_kernel,
        grid=(m // tm,),
        in_specs=[pl.BlockSpec((tm, k), lambda i: (i, 0)), _full((k, n))],
        out_specs=pl.BlockSpec((tm, n), lambda i: (i, 0)),
        out_shape=jax.ShapeDtypeStruct((m, n), F32),
        compiler_params=_cp("parallel"),
        name="mm",
    )(a_bf, w_bf)


def _cmp_fin_kernel(a_ref, pb_ref, w2_ref, o_ref, *, nseg):
    a = a_ref[0]
    hw = a.shape[1] // 2
    bias = pb_ref[0:1, :hw] + pb_ref[1:2, hw:]
    pre = a[:, :hw] + pltpu.roll(a[:, hw:], nseg - 1, axis=0) + bias
    o_ref[0] = _dot(_gelu(pre).astype(BF16), w2_ref[...])


def _compress_weights(w1k, w1v, w2k, w2v, pek, pev):
    span = NSA_CMP_LEN // NSA_CMP_STRIDE
    st = NSA_CMP_STRIDE
    hid = w1k.shape[1]
    slots = 2 * NSA_KV
    eye = jnp.eye(slots, dtype=F32)
    per_slot = lambda k, v: jnp.stack([k] * NSA_KV + [v] * NSA_KV)
    w1 = per_slot(w1k.reshape(span, st, _HD, hid), w1v.reshape(span, st, _HD, hid))
    wbig = jnp.einsum('smide,st->isdmte', w1, eye)
    pe = per_slot(pek.reshape(span, st, _HD), pev.reshape(span, st, _HD))
    pex = jnp.pad(pe.transpose(1, 2, 0, 3), ((0, 8 - span), (0, 0), (0, 0), (0, 0)))
    w2big = jnp.einsum('sed,st->setd', per_slot(w2k, w2v), eye)
    return (wbig.reshape(st * slots * _HD, span * slots * hid).astype(BF16),
            pex.reshape(8, st * slots * _HD).astype(BF16),
            w2big.reshape(slots * hid, slots * _HD).astype(BF16))


def _compress(cmpflat_bf, wbig, pex, w2big):
    batch, nseg, kdim = cmpflat_bf.shape
    a = _mm(cmpflat_bf.reshape(batch * nseg, kdim), wbig).reshape(batch, nseg, -1)
    pb = _mm(pex, wbig)
    n2 = a.shape[2]
    ow = w2big.shape[1]
    return pl.pallas_call(
        functools.partial(_cmp_fin_kernel, nseg=nseg),
        grid=(batch,),
        in_specs=[pl.BlockSpec((1, nseg, n2), lambda b: (b, 0, 0)), _full(pb.shape), _full(w2big.shape)],
        out_specs=pl.BlockSpec((1, nseg, ow), lambda b: (b, 0, 0)),
        out_shape=jax.ShapeDtypeStruct((batch, nseg, ow), F32),
        compiler_params=_cp("parallel"),
        name="compress_fin",
    )(a, pb, w2big)


def _block_mean_kernel(k_ref, o_ref):
    o_ref[0, 0] = jnp.sum(k_ref[0], axis=0, keepdims=True) * (1.0 / MOBA_BLOCK)


def _block_mean(rows3, nbf, col_block):
    batch = rows3.shape[0]
    wd = _C_MK
    return pl.pallas_call(
        _block_mean_kernel,
        grid=(batch, nbf),
        in_specs=[pl.BlockSpec((1, MOBA_BLOCK, wd), lambda b, n: (b, n, col_block))],
        out_specs=pl.BlockSpec((1, 1, 1, wd), lambda b, n: (b, n, 0, 0)),
        out_shape=jax.ShapeDtypeStruct((batch, nbf, 1, wd), F32),
        compiler_params=_cp("parallel", "parallel"),
        name="block_mean",
    )(rows3)


def _top_rounds(score, row_id, k):
    far = 1 << 20
    sel = jnp.zeros(score.shape, F32)
    cur = score
    for _ in range(k):
        m = jnp.max(cur, axis=0, keepdims=True)
        idx = jnp.min(jnp.where(cur == m, row_id, far), axis=0, keepdims=True)
        pick = row_id == idx
        sel = jnp.where(pick, 1.0, sel)
        cur = jnp.where(pick, -jnp.inf, cur)
    return sel


def _softmax0(s, mask):
    s = jnp.where(mask, s, NEG)
    e = jnp.where(mask, jnp.exp(s - jnp.max(s, axis=0, keepdims=True)), 0.0)
    return e / jnp.maximum(jnp.sum(e, axis=0, keepdims=True), 1e-30)


def _tdot(a, b):
    return lax.dot_general(a, b, (((0,), (0,)), ((), ())), preferred_element_type=F32)


def _nsa_front(qu, kc, vc, mt, tpos, tq, nc, n_sel):
    nc_pad = kc.shape[0]
    nsel_pad = mt.shape[0]
    ci = lax.broadcasted_iota(jnp.int32, (nc_pad, 1), 0)
    bi = lax.broadcasted_iota(jnp.int32, (nsel_pad, tq), 0)
    cur = tpos[:, :tq] >> _SEL_SHIFT
    cmask = ((ci * NSA_CMP_STRIDE + (NSA_CMP_LEN - 1)) <= tpos) & (ci < nc)
    pc = _softmax0(_dot3(kc, qu), cmask)
    o_c = _tdot(vc.astype(BF16), pc.astype(BF16))
    imp = pc[:, 0:tq]
    for r in range(1, NSA_GRP):
        imp = imp + pc[:, r * tq:(r + 1) * tq]
    score = _dot3(mt, imp)
    forced = (bi == 0) | (bi == cur) | (bi == cur - 1)
    score = jnp.where(forced, BIG, score)
    score = jnp.where(bi <= cur, score, -BIG)
    score = jnp.where(bi < n_sel, score, -3e38)
    sel = _top_rounds(score, bi, min(NSA_TOPN, n_sel))
    return o_c, jnp.where(bi <= cur, sel, 0.0)


def _nsa_window(qr, kw, vw, tpos, wpos):
    wmask = (wpos <= tpos) & (wpos >= tpos - NSA_WINDOW) & (wpos >= 0)
    pw = _softmax0(_dot(kw.astype(BF16), qr), wmask).astype(BF16)
    return _tdot(vw.astype(BF16), pw)


def _online_softmax(m_ref, l_ref, acc_ref, s, mask, pv, guard=True):
    s = jnp.where(mask, s, NEG)
    m_old = m_ref[...]
    m_new = jnp.maximum(m_old, jnp.max(s, axis=0, keepdims=True))
    alpha = jnp.exp(m_old - m_new)
    p = jnp.exp(s - m_new)
    if guard:
        p = jnp.where(mask, p, 0.0)
    l_ref[...] = alpha * l_ref[...] + jnp.sum(p, axis=0, keepdims=True)
    acc_ref[...] = alpha * acc_ref[...] + pv(p.astype(BF16))
    m_ref[...] = m_new


def _heads_t(x, tq, groups, rep):
    heads = []
    for p in range(groups * rep // 2):
        xt = x[:, p * LANES:(p + 1) * LANES].T
        heads += [xt[0:_HD], xt[_HD:2 * _HD]]
    return [jnp.concatenate(heads[g * rep:(g + 1) * rep], axis=1) for g in range(groups)]


def _heads_untranspose(o_ref, per_group, tq, rep):
    heads = [og[:, r * tq:(r + 1) * tq] for og in per_group for r in range(rep)]
    for p in range(len(heads) // 2):
        o_ref[:, p * LANES:(p + 1) * LANES] = jnp.concatenate([heads[2 * p], heads[2 * p + 1]], axis=0).T


def _block_diag(qs):
    z = jnp.zeros_like(qs[0])
    return jnp.concatenate([jnp.concatenate([q if h == g else z for h in range(len(qs))], axis=1)
                            for g, q in enumerate(qs)], axis=0)


def _nsa_kernel(qu_ref, qr_ref, ng_ref, kcvc_ref, mt_ref, ksv_ref, win_ref,
                o_ref, sel_scr, m_scr, l_scr, acc_scr, *, tq, nc, n_sel, nkt_max, wl):
    qi = pl.program_id(1)
    n = NSA_GRP * tq
    tk = KEY_TILE
    kw_n = NSA_KV * _HD
    scale = _HD ** -0.5
    t0 = qi * tq
    lane = lax.broadcasted_iota(jnp.int32, (1, n), 1)
    tpos = t0 + (lane & (tq - 1))
    nkt = jnp.minimum((t0 + tq - 1) // tk + 1, nkt_max)
    wstart = pl.multiple_of(jnp.maximum(t0 - NSA_WINDOW, 0), LANES)
    wpos = wstart + lax.broadcasted_iota(jnp.int32, (wl, 1), 0)

    qus = _heads_t(qu_ref[...] * scale, tq, NSA_KV, NSA_GRP)
    qrs = [q.astype(BF16) for q in _heads_t(qr_ref[...] * scale, tq, NSA_KV, NSA_GRP)]
    gl_t = _sigmoid(ng_ref[...].T)
    gates = [[jnp.concatenate([gl_t[(g * NSA_GRP + r) * 3 + j:(g * NSA_GRP + r) * 3 + j + 1] for r in range(NSA_GRP)],
                              axis=1) for j in range(3)] for g in range(NSA_KV)]
    kcvc = kcvc_ref[0]
    win = win_ref[0, pl.ds(wstart, wl), :]

    o_cw = []
    for g in range(NSA_KV):
        hs = slice(g * _HD, (g + 1) * _HD)
        o_c, sel = _nsa_front(qus[g], kcvc[:, hs], kcvc[:, kw_n + g * _HD:kw_n + (g + 1) * _HD], mt_ref[...],
                              tpos, tq, nc, n_sel)
        sel_scr[g] = sel
        o_w = _nsa_window(qrs[g], win[:, hs], win[:, kw_n + g * _HD:kw_n + (g + 1) * _HD], tpos, wpos)
        o_cw.append(gates[g][0] * o_c + gates[g][2] * o_w)
    m_scr[...] = jnp.full(m_scr.shape, NEG, F32)
    l_scr[...] = jnp.zeros(l_scr.shape, F32)
    acc_scr[...] = jnp.zeros(acc_scr.shape, F32)
    qbd = _block_diag(qrs)

    def tile(j, causal):
        nb = tk // NSA_SEL_BLOCK
        rows = pl.ds(pl.multiple_of(j * tk, tk), tk)
        ksv = ksv_ref[0, rows, :]
        masks = []
        for g in range(NSA_KV):
            pieces = [jnp.broadcast_to(sel_scr[g, pl.ds(j * nb + u, 1), :], (NSA_SEL_BLOCK, tq)) for u in range(nb)]
            mk = jnp.concatenate(pieces, axis=0)
            masks.append(jnp.concatenate([mk] * NSA_GRP, axis=1) > 0.5)
        mask = jnp.concatenate(masks, axis=1)
        if causal:
            kpos = j * tk + lax.broadcasted_iota(jnp.int32, (tk, 1), 0)
            mask = mask & jnp.concatenate([kpos <= tpos] * NSA_KV, axis=1)
        vb = ksv[:, kw_n:].astype(BF16)
        _online_softmax(m_scr, l_scr, acc_scr, _dot(ksv[:, :kw_n].astype(BF16), qbd), mask,
                        lambda p: _tdot(vb, p), guard=False)

    def body(j, carry):
        tile(j, False)
        return carry

    lax.fori_loop(0, nkt - 1, body, 0)
    tile(nkt - 1, True)
    o_s = acc_scr[...] / jnp.maximum(l_scr[...], 1e-30)
    outs = [o_cw[g] + gates[g][1] * o_s[g * _HD:(g + 1) * _HD, g * n:(g + 1) * n] for g in range(NSA_KV)]
    _heads_untranspose(o_ref, outs, tq, NSA_GRP)


def _nsa_select_kernel(qu_ref, qr_ref, gl_ref, kc_ref, vc_ref, mt_ref, kw_ref, vw_ref,
                       o_ref, g1_ref, sel_ref, *, tq, nc, n_sel, qpos0, wpos0):
    n = NSA_GRP * tq
    lane = lax.broadcasted_iota(jnp.int32, (1, n), 1)
    tpos = qpos0 + (lane & (tq - 1))
    wpos = wpos0 + lax.broadcasted_iota(jnp.int32, (kw_ref.shape[2], 1), 0)
    for g in range(NSA_KV):
        qr = qr_ref[0, 0, g].astype(BF16)
        o_c, sel = _nsa_front(qu_ref[0, 0, g], kc_ref[0, g], vc_ref[0, g], mt_ref[...], tpos, tq, nc, n_sel)
        sel_ref[0, g] = sel
        o_w = _nsa_window(qr, kw_ref[0, g], vw_ref[0, g], tpos, wpos)
        gate = _sigmoid(gl_ref[0, 0, g])
        o_ref[0, g] = gate[0:1] * o_c + gate[2:3] * o_w
        g1_ref[0, g] = gate[1:2]


def _moba_select(q, kb, own, nbf):
    ni = lax.broadcasted_iota(jnp.int32, (kb.shape[0], q.shape[1]), 0)
    gs = _dot3(kb, q)
    gs = jnp.where(ni < own, gs, -BIG)
    gs = jnp.where(ni < nbf, gs, -3e38)
    sel = _top_rounds(gs, ni, min(MOBA_TOPK, nbf))
    return jnp.where((ni < own) & (ni < nbf), sel, 0.0)


def _moba_select_kernel(q_ref, kb_ref, sel_ref, *, own, nbf):
    for g in range(MOBA_KV):
        sel_ref[0, g] = _moba_select(q_ref[0, 0, g], kb_ref[0, g], own, nbf)


def _moba_kernel(q_ref, kb_ref, kmv_ref, o_ref, sel_scr, m_scr, l_scr, acc_scr, *, tq, nbf, nkt_max):
    qi = pl.program_id(1)
    n = MOBA_GRP * tq
    tk = KEY_TILE
    assert tk == MOBA_BLOCK
    t0 = qi * tq
    lane = lax.broadcasted_iota(jnp.int32, (1, MOBA_KV * n), 1)
    tpos = t0 + (lane & (tq - 1))
    own = t0 // MOBA_BLOCK
    scale = _HD ** -0.5

    qs = _heads_t(q_ref[...], tq, MOBA_KV, MOBA_GRP)
    for g in range(MOBA_KV):
        sel_scr[g] = _moba_select(qs[g], kb_ref[0, g], own, nbf)
    qbd = _block_diag([(q * scale).astype(BF16) for q in qs])
    m_scr[...] = jnp.full(m_scr.shape, NEG, F32)
    l_scr[...] = jnp.zeros(l_scr.shape, F32)
    acc_scr[...] = jnp.zeros(acc_scr.shape, F32)

    def tile(j, mask):
        kmv = kmv_ref[0, pl.ds(pl.multiple_of(j * tk, tk), tk), :]
        vb = kmv[:, _C_MK:].astype(BF16)
        _online_softmax(m_scr, l_scr, acc_scr, _dot(kmv[:, :_C_MK].astype(BF16), qbd), mask,
                        lambda p: _tdot(vb, p), guard=False)

    kpos = own * tk + lax.broadcasted_iota(jnp.int32, (tk, 1), 0)
    tile(own, kpos <= tpos)

    def body(j, carry):
        picked = jnp.concatenate([jnp.broadcast_to(sel_scr[g, pl.ds(j, 1), :], (tk, n)) for g in range(MOBA_KV)],
                                 axis=1)
        tile(j, picked > 0.5)
        return carry

    lax.fori_loop(0, jnp.minimum(own, nkt_max), body, 0)
    o = acc_scr[...] / jnp.maximum(l_scr[...], 1e-30)
    _heads_untranspose(o_ref, [o[g * _HD:(g + 1) * _HD, g * n:(g + 1) * n] for g in range(MOBA_KV)], tq, MOBA_GRP)


def _to_qt(x, batch, nt, tq, groups, rep):
    x = x.reshape(batch, nt, tq, groups, rep, _HD).transpose(0, 1, 3, 5, 4, 2)
    return x.reshape(batch, nt, groups, _HD, rep * tq)


def _head_major(x, heads):
    b, l, _ = x.shape
    return x.reshape(b, l, heads, _HD).transpose(0, 2, 1, 3)


def _sel_score_matrix(nsel_pad, nc_pad, nc):
    ratio = NSA_SEL_BLOCK // NSA_CMP_STRIDE
    span = NSA_CMP_LEN // NSA_CMP_STRIDE
    mt = np.zeros((nsel_pad, nc_pad), np.float32)
    for j in range(nsel_pad):
        for o in range(-(span - 1), ratio):
            c = j * ratio + o
            wt = sum(1 for m in range(ratio) for q in range(span) if m - q == o)
            if 0 <= c < nc:
                mt[j, c] = wt
    return jnp.asarray(mt)


def _mixer_c_attention(qu, qr, mq, ng, rows3, win3, kcvc, batch, seq):
    tk = KEY_TILE
    tq = LANES
    assert seq % tk == 0 and seq >= NSA_WINDOW + tq
    nt = seq // tq
    nkt_max = seq // tk
    nc = (seq - NSA_CMP_LEN) // NSA_CMP_STRIDE + 1
    nseg = kcvc.shape[1]
    n_sel = seq // NSA_SEL_BLOCK
    nsel_pad = -(-n_sel // 8) * 8
    nbf = seq // MOBA_BLOCK
    nbf_pad = -(-nbf // 8) * 8
    wl = NSA_WINDOW + tq
    mt = _sel_score_matrix(nsel_pad, nseg, nc)
    kw_n = NSA_KV * _HD

    kb = _block_mean(rows3, nbf, col_block=2)[:, :, 0, :]
    kb = _head_major(jnp.pad(kb, ((0, 0), (0, nbf_pad - nbf), (0, 0))), MOBA_KV)

    n_n = NSA_GRP * tq
    n_m = MOBA_GRP * tq
    tile = lambda wd: pl.BlockSpec((tq, wd), lambda b, i: (b * nt + i, 0))
    per_b = lambda shp, cb=0: pl.BlockSpec((1,) + shp, lambda b, i: (b,) + (0,) * (len(shp) - 1) + (cb,))
    o_n = pl.pallas_call(
        functools.partial(_nsa_kernel, tq=tq, nc=nc, n_sel=n_sel, nkt_max=nkt_max, wl=wl),
        grid=(batch, nt),
        in_specs=[tile(NSA_HEADS * _HD), tile(NSA_HEADS * _HD), tile(LANES), per_b((nseg, 2 * kw_n)), _full(mt.shape),
                  per_b((seq, 2 * kw_n), 1), per_b((seq, 2 * kw_n))],
        out_specs=tile(NSA_HEADS * _HD),
        out_shape=jax.ShapeDtypeStruct((batch * seq, NSA_HEADS * _HD), F32),
        scratch_shapes=[pltpu.VMEM((NSA_KV, nsel_pad, tq), F32), pltpu.VMEM((1, NSA_KV * n_n), F32),
                        pltpu.VMEM((1, NSA_KV * n_n), F32), pltpu.VMEM((kw_n, NSA_KV * n_n), F32)],
        compiler_params=_cp("parallel", "arbitrary"),
        name="nsa",
    )(qu, qr, ng, kcvc, mt, rows3, win3)

    o_m = pl.pallas_call(
        functools.partial(_moba_kernel, tq=tq, nbf=nbf, nkt_max=nkt_max),
        grid=(batch, nt),
        in_specs=[tile(MOBA_HEADS * _HD), per_b(kb.shape[1:]), per_b((seq, 2 * _C_MK), 1)],
        out_specs=tile(MOBA_HEADS * _HD),
        out_shape=jax.ShapeDtypeStruct((batch * seq, MOBA_HEADS * _HD), F32),
        scratch_shapes=[pltpu.VMEM((MOBA_KV, nbf_pad, n_m), F32), pltpu.VMEM((1, MOBA_KV * n_m), F32),
                        pltpu.VMEM((1, MOBA_KV * n_m), F32), pltpu.VMEM((_C_MK, MOBA_KV * n_m), F32)],
        compiler_params=_cp("parallel", "arbitrary"),
        name="moba",
    )(mq, kb, rows3)
    return o_n, o_m


PAGES_PER_STEP = 4
_SEG_PER_PAGE = PAGE_SIZE // NSA_CMP_STRIDE


def _pool_view(pool):
    n_pool, n_layers = pool.shape[0], pool.shape[1]
    return jnp.transpose(pool, (0, 1, 3, 4, 2)).reshape(n_pool * n_layers, ROW_W, PAGE_SIZE)


def _page_specs(npages, n_layers, layer, rows, row_block):
    def spec(pg):
        return pl.BlockSpec(
            (1, rows, PAGE_SIZE),
            lambda b, j, p: (p[b * npages + j * PAGES_PER_STEP + pg] * n_layers + layer, row_block, 0))
    return [spec(pg) for pg in range(PAGES_PER_STEP)]


def _scan_kernel(pt_ref, *refs):
    pps = PAGES_PER_STEP
    cmp_refs, mk_refs = refs[0:pps], refs[pps:2 * pps]
    flat_ref, kbar_ref, ck_scr, cv_scr = refs[2 * pps:]
    wd = 2 * _C_KV
    for pr in range(pps // 2):
        for q in range(2):
            x = cmp_refs[2 * pr + q][0].T
            ck_scr[q * PAGE_SIZE:(q + 1) * PAGE_SIZE, :] = x[:, :_C_KV]
            cv_scr[q * PAGE_SIZE:(q + 1) * PAGE_SIZE, :] = x[:, _C_KV:]
        r0 = 2 * pr * _SEG_PER_PAGE
        for i in range(NSA_CMP_STRIDE):
            c0 = i * wd
            flat_ref[0, r0:r0 + 2 * _SEG_PER_PAGE, c0:c0 + _C_KV] = \
                ck_scr[pl.ds(i, 2 * _SEG_PER_PAGE, stride=NSA_CMP_STRIDE), :].astype(BF16)
            flat_ref[0, r0:r0 + 2 * _SEG_PER_PAGE, c0 + _C_KV:c0 + wd] = \
                cv_scr[pl.ds(i, 2 * _SEG_PER_PAGE, stride=NSA_CMP_STRIDE), :].astype(BF16)
        ksum = (jnp.sum(mk_refs[2 * pr][0].T, axis=0, keepdims=True)
                + jnp.sum(mk_refs[2 * pr + 1][0].T, axis=0, keepdims=True))
        kbar_ref[0, pr] = ksum * (1.0 / MOBA_BLOCK)


def _scan_pages(poolv, pt, layer, n_layers):
    batch, npages = pt.shape
    assert npages % PAGES_PER_STEP == 0 and MOBA_BLOCK == 2 * PAGE_SIZE
    ng = npages // PAGES_PER_STEP
    nseg = npages * _SEG_PER_PAGE
    nbf = npages // 2
    wd = 2 * _C_KV
    gs = pltpu.PrefetchScalarGridSpec(
        num_scalar_prefetch=1,
        grid=(batch, ng),
        in_specs=_page_specs(npages, n_layers, layer, wd, 0) + _page_specs(npages, n_layers, layer, _C_MK, 2),
        out_specs=[pl.BlockSpec((1, PAGES_PER_STEP * _SEG_PER_PAGE, NSA_CMP_STRIDE * wd), lambda b, j, p: (b, j, 0)),
                   pl.BlockSpec((1, PAGES_PER_STEP // 2, 1, _C_MK), lambda b, j, p: (b, j, 0, 0))],
        scratch_shapes=[pltpu.VMEM((2 * PAGE_SIZE, _C_KV), F32), pltpu.VMEM((2 * PAGE_SIZE, _C_KV), F32)],
    )
    flat, kbar = pl.pallas_call(
        _scan_kernel,
        grid_spec=gs,
        out_shape=[jax.ShapeDtypeStruct((batch, nseg, NSA_CMP_STRIDE * wd), BF16),
                   jax.ShapeDtypeStruct((batch, nbf, 1, _C_MK), F32)],
        compiler_params=_cp("parallel", "arbitrary"),
        name="scan_pages",
    )(pt.reshape(-1), *([poolv] * (2 * PAGES_PER_STEP)))
    return flat, kbar[:, :, 0, :]


def _flash_update(m_ref, l_ref, acc_ref, k, v, q, mask):
    vb = v.astype(BF16)
    _online_softmax(m_ref, l_ref, acc_ref, _dot(k.astype(BF16), q), mask, lambda p: _tdot(vb, p))


def _dec_attn_kernel(pt_ref, *refs, past, nsteps, seq_pad):
    pps = PAGES_PER_STEP
    sel_refs = refs[0:pps]
    mob_refs = refs[pps:2 * pps]
    (qn_ref, qm_ref, smask_ref, mmask_ref, nsel_ref, nmob_ref, opart_ref, g1_ref,
     on_ref, om_ref, mn, ln, accn, mm, lm, accm) = refs[2 * pps:]
    j = pl.program_id(1)
    kw_n = NSA_KV * _HD
    kw_m = _C_MK

    @pl.when(j == 0)
    def _():
        mn[...] = jnp.full(mn.shape, NEG, F32)
        mm[...] = jnp.full(mm.shape, NEG, F32)
        ln[...] = jnp.zeros(ln.shape, F32)
        lm[...] = jnp.zeros(lm.shape, F32)
        accn[...] = jnp.zeros(accn.shape, F32)
        accm[...] = jnp.zeros(accm.shape, F32)

    qn = qn_ref[0]
    qm = qm_ref[0]
    rows = pps * PAGE_SIZE
    ksv = jnp.concatenate([r[0].T for r in sel_refs], axis=0)
    nb = rows // NSA_SEL_BLOCK
    srow = smask_ref[0, pl.ds(pl.multiple_of(j * nb, nb), nb), :]
    smask = jnp.concatenate([jnp.broadcast_to(srow[u:u + 1], (NSA_SEL_BLOCK, LANES)) for u in range(nb)], axis=0)
    _flash_update(mn, ln, accn, ksv[:, :kw_n], ksv[:, kw_n:], qn, smask > 0.5)

    kmv = jnp.concatenate([r[0].T for r in mob_refs], axis=0)
    nbm = rows // MOBA_BLOCK
    mmask = jnp.concatenate([jnp.broadcast_to(mmask_ref[0, pl.ds(j * nbm + u, 1), :], (MOBA_BLOCK, LANES))
                             for u in range(nbm)], axis=0)
    _flash_update(mm, lm, accm, kmv[:, :kw_m], kmv[:, kw_m:], qm, mmask > 0.5)

    @pl.when(j == nsteps - 1)
    def _():
        lane = lax.broadcasted_iota(jnp.int32, (1, LANES), 1)
        tpos = past + (lane & (seq_pad - 1))
        new_rows = nsel_ref.shape[1]
        kpos = past + lax.broadcasted_iota(jnp.int32, (new_rows, 1), 0)
        causal = kpos <= tpos
        new_blk = past // NSA_SEL_BLOCK
        nrow = jnp.broadcast_to(smask_ref[0, new_blk:new_blk + 1, :], (new_rows, LANES)) > 0.5
        nsel = nsel_ref[0]
        _flash_update(mn, ln, accn, nsel[:, :kw_n], nsel[:, kw_n:], qn, causal & nrow)
        nmob = nmob_ref[0]
        _flash_update(mm, lm, accm, nmob[:, :kw_m], nmob[:, kw_m:], qm, causal)
        on_ref[0] = opart_ref[0] + g1_ref[0] * (accn[...] / jnp.maximum(ln[...], 1e-30))
        om_ref[0] = accm[...] / jnp.maximum(lm[...], 1e-30)


def _block_diag_q(x, batch, seq, groups, rep):
    x = x.reshape(batch, seq, groups, rep, _HD).transpose(0, 2, 4, 3, 1).reshape(batch, groups, _HD, rep * seq)
    bd = jnp.einsum('bgdn,gh->bgdhn', x, jnp.eye(groups, dtype=x.dtype))
    bd = bd.reshape(batch, groups * _HD, groups * rep * seq)
    return jnp.pad(bd, ((0, 0), (0, 0), (0, LANES - bd.shape[2])))


def _diag_blocks(o, batch, seq, groups, rep):
    o = o[:, :, :groups * rep * seq].reshape(batch, groups, _HD, groups, rep, seq)
    o = jnp.stack([o[:, g, :, g] for g in range(groups)], axis=1)
    return o.transpose(0, 4, 1, 3, 2).reshape(batch * seq, groups * rep * _HD)


def _decode_attention(qu, qr, mq, ng, rows3, win3, pool, pt, cache_win, layer, wbig, pex, w2big, batch, seq):
    npages = pt.shape[1]
    past = npages * PAGE_SIZE
    lt = past + seq
    assert seq <= 8 and seq < NSA_CMP_STRIDE and past % MOBA_BLOCK == 0
    seq_pad = 8
    tq = 64
    scale = _HD ** -0.5

    n_layers = pool.shape[1]
    poolv = _pool_view(pool)
    cmpflat, kbar = _scan_pages(poolv, pt, layer, n_layers)
    kcvc = _compress(cmpflat, wbig, pex, w2big)
    nseg = kcvc.shape[1]
    nc = (lt - NSA_CMP_LEN) // NSA_CMP_STRIDE + 1
    assert nseg == nc + 1
    n_sel = -(-lt // NSA_SEL_BLOCK)
    nsel_pad = -(-n_sel // 8) * 8
    nbf = lt // MOBA_BLOCK
    nbf_pad = -(-(nbf + 1) // 8) * 8
    ng_steps = npages // PAGES_PER_STEP

    def padq(x):
        x = x.reshape(batch, seq, -1)
        return jnp.pad(x, ((0, 0), (0, tq - seq), (0, 0))).reshape(batch * tq, -1)

    qu_t = _to_qt(padq(qu) * scale, batch, 1, tq, NSA_KV, NSA_GRP)
    qr_t = _to_qt(padq(qr) * scale, batch, 1, tq, NSA_KV, NSA_GRP)
    mq_t = _to_qt(padq(mq), batch, 1, tq, MOBA_KV, MOBA_GRP)
    gl = padq(ng)[:, :NSA_HEADS * 3].reshape(batch, 1, tq, NSA_KV, NSA_GRP, 3).transpose(0, 1, 3, 5, 4, 2)
    gl = gl.reshape(batch, 1, NSA_KV, 3, NSA_GRP * tq)
    kc = _head_major(kcvc[:, :, 0:_C_KV], NSA_KV)
    vc = _head_major(kcvc[:, :, _C_KV:2 * _C_KV], NSA_KV)
    mt = _sel_score_matrix(nsel_pad, nseg, nc)
    wb = cache_win.reshape(batch, -1, 2 * _C_KV)
    wcat = jnp.concatenate([wb, win3], axis=1)
    wl = -(-wcat.shape[1] // LANES) * LANES
    wfull = jnp.pad(wcat, ((0, 0), (0, wl - wcat.shape[1]), (0, 0)))
    kw = _head_major(wfull[:, :, 0:_C_KV], NSA_KV).astype(BF16)
    vw = _head_major(wfull[:, :, _C_KV:], NSA_KV).astype(BF16)
    kb = _head_major(jnp.pad(kbar, ((0, 0), (0, nbf_pad - nbf), (0, 0))), MOBA_KV)

    n_n = NSA_GRP * tq
    n_m = MOBA_GRP * tq
    b1 = lambda shp: pl.BlockSpec((1,) + shp, lambda b: (b,) + (0,) * len(shp))
    o_part, g1, sel = pl.pallas_call(
        functools.partial(_nsa_select_kernel, tq=tq, nc=nc, n_sel=n_sel, qpos0=past, wpos0=past - wb.shape[1]),
        grid=(batch,),
        in_specs=[b1((1, NSA_KV, _HD, n_n)), b1((1, NSA_KV, _HD, n_n)), b1((1, NSA_KV, 3, n_n)),
                  b1(kc.shape[1:]), b1(vc.shape[1:]), _full(mt.shape), b1(kw.shape[1:]), b1(vw.shape[1:])],
        out_specs=[b1((NSA_KV, _HD, n_n)), b1((NSA_KV, 1, n_n)), b1((NSA_KV, nsel_pad, tq))],
        out_shape=[jax.ShapeDtypeStruct((batch, NSA_KV, _HD, n_n), F32),
                   jax.ShapeDtypeStruct((batch, NSA_KV, 1, n_n), F32),
                   jax.ShapeDtypeStruct((batch, NSA_KV, nsel_pad, tq), F32)],
        compiler_params=_cp("parallel"),
        name="nsa_select",
    )(qu_t, qr_t, gl, kc, vc, mt, kw, vw)
    msel = pl.pallas_call(
        functools.partial(_moba_select_kernel, own=past // MOBA_BLOCK, nbf=nbf),
        grid=(batch,),
        in_specs=[b1((1, MOBA_KV, _HD, n_m)), b1(kb.shape[1:])],
        out_specs=b1((MOBA_KV, nbf_pad, n_m)),
        out_shape=jax.ShapeDtypeStruct((batch, MOBA_KV, nbf_pad, n_m), F32),
        compiler_params=_cp("parallel"),
        name="moba_select",
    )(mq_t, kb)

    def lanes(x):
        return jnp.pad(x, [(0, 0)] * (x.ndim - 1) + [(0, LANES - x.shape[-1])])

    pad_s = lambda x: jnp.pad(x.reshape(batch, seq, -1), ((0, 0), (0, seq_pad - seq), (0, 0)))
    qn_bd = _block_diag_q(pad_s(qr).reshape(batch * seq_pad, -1) * scale, batch, seq_pad, NSA_KV, NSA_GRP).astype(BF16)
    qm_bd = _block_diag_q(pad_s(mq).reshape(batch * seq_pad, -1) * scale, batch, seq_pad, MOBA_KV, MOBA_GRP).astype(BF16)
    smask = jnp.broadcast_to(sel[:, :, :, None, :seq_pad], (batch, NSA_KV, nsel_pad, NSA_GRP, seq_pad))
    smask = lanes(smask.transpose(0, 2, 1, 3, 4).reshape(batch, nsel_pad, -1))
    mmask = msel.reshape(batch, MOBA_KV, nbf_pad, MOBA_GRP, tq)[..., :seq_pad]
    mmask = lanes(mmask.transpose(0, 2, 1, 3, 4).reshape(batch, nbf_pad, -1))
    op = o_part.reshape(batch, NSA_KV, _HD, NSA_GRP, tq)[..., :seq_pad]
    op_bd = jnp.einsum('bgdrt,gh->bgdhrt', op, jnp.eye(NSA_KV, dtype=F32))
    op_bd = lanes(op_bd.reshape(batch, NSA_KV * _HD, -1))
    g1l = g1.reshape(batch, NSA_KV, NSA_GRP, tq)[..., :seq_pad].reshape(batch, 1, -1)
    g1l = lanes(g1l)
    new_pad = 16
    pad_n = lambda x: jnp.pad(x, ((0, 0), (0, new_pad - seq), (0, 0)))
    new_sel = pad_n(rows3[:, :, 2 * _C_KV:4 * _C_KV])
    new_mob = pad_n(rows3[:, :, 4 * _C_KV:])

    kw_n = NSA_KV * _HD
    bj = lambda shp: pl.BlockSpec((1,) + shp, lambda b, j, p: (b,) + (0,) * len(shp))
    gs = pltpu.PrefetchScalarGridSpec(
        num_scalar_prefetch=1,
        grid=(batch, ng_steps),
        in_specs=(_page_specs(npages, n_layers, layer, 2 * kw_n, 1) + _page_specs(npages, n_layers, layer, 2 * _C_MK, 1)
                  + [bj((kw_n, LANES)), bj((_C_MK, LANES)), bj((nsel_pad, LANES)), bj((nbf_pad, LANES)),
                     bj((new_pad, 2 * kw_n)), bj((new_pad, 2 * _C_MK)), bj((kw_n, LANES)), bj((1, LANES))]),
        out_specs=[bj((kw_n, LANES)), bj((_C_MK, LANES))],
        scratch_shapes=[pltpu.VMEM((1, LANES), F32), pltpu.VMEM((1, LANES), F32), pltpu.VMEM((kw_n, LANES), F32),
                        pltpu.VMEM((1, LANES), F32), pltpu.VMEM((1, LANES), F32), pltpu.VMEM((_C_MK, LANES), F32)],
    )
    o_n, o_m = pl.pallas_call(
        functools.partial(_dec_attn_kernel, past=past, nsteps=ng_steps, seq_pad=seq_pad),
        grid_spec=gs,
        out_shape=[jax.ShapeDtypeStruct((batch, kw_n, LANES), F32), jax.ShapeDtypeStruct((batch, _C_MK, LANES), F32)],
        compiler_params=_cp("parallel", "arbitrary"),
        name="decode_attn",
    )(pt.reshape(-1), *([poolv] * (2 * PAGES_PER_STEP)), qn_bd, qm_bd, smask, mmask, new_sel, new_mob, op_bd, g1l)
    o_n = _diag_blocks(o_n, batch, seq_pad, NSA_KV, NSA_GRP).reshape(batch, seq_pad, -1)[:, :seq]
    o_m = _diag_blocks(o_m, batch, seq_pad, MOBA_KV, MOBA_GRP).reshape(batch, seq_pad, -1)[:, :seq]
    return o_n.reshape(batch * seq, -1), o_m.reshape(batch * seq, -1)


def _prep_weights(prm):
    w = {}
    w['w_in_a'] = prm['w_in_a'].astype(BF16)
    w['w_out_a'] = prm['w_out_a'].astype(BF16)
    w['w_out_c'] = prm['w_out_c'].astype(BF16)
    w['w_in_c'] = [_proj_c_weights(prm['w_in_c'][c]) for c in range(prm['w_in_c'].shape[0])]
    w['cmp'] = [_compress_weights(prm['cmp_w1_k'][c], prm['cmp_w1_v'][c], prm['cmp_w2_k'][c], prm['cmp_w2_v'][c],
                                  prm['cmp_pe_k'][c], prm['cmp_pe_v'][c]) for c in range(prm['w_in_c'].shape[0])]
    depth, d, _ = prm['router_c_w'].shape
    wr = jnp.concatenate([prm['router_c_w'], prm['router_f_w'].transpose(0, 2, 1, 3).reshape(depth, d, -1)], axis=2)
    w['router_w'] = jnp.pad(wr, ((0, 0), (0, 0), (0, LANES - wr.shape[2])))
    br = jnp.concatenate([prm['router_c_b'], prm['router_f_b'].reshape(depth, -1)], axis=1)
    w['router_b'] = jnp.pad(br, ((0, 0), (0, LANES - br.shape[1])))[:, None, :]
    w['moe_w1'] = prm['moe_w1']
    w['moe_w3'] = prm['moe_w3']
    w['moe_w2'] = prm['moe_w2']
    w['ple_w'] = prm['ple_w'].astype(BF16)
    w['ple_gate_w'] = prm['ple_gate_w'].astype(BF16)
    w['lb'] = jnp.cumsum(jax.nn.softmax(prm['hgrn_lb'].astype(F32), axis=0), axis=0)
    return w


def _forward(x, p, pos, prm, w, ctx):
    batch, seq, d = x.shape
    t = batch * seq
    depth = p.shape[0]
    h = x.reshape(t, d)
    outs = {}
    for i in range(depth):
        gmix = prm['norm_mix'][i][None, :]
        if i % 2 == 0:
            a = i // 2
            lb512 = jnp.tile(w['lb'][a], HG_HEADS)[None, :]
            q, k, lf, iv, sg, gu, vn = _proj_a(h, gmix, w['w_in_a'][a], lb512, prm['cm_vnorm'][a][None, :])
            if ctx is None:
                st0 = jnp.zeros((batch, HG_WIDTH, HG_WIDTH), F32)
            else:
                s0 = ctx['state_hgrn'][a].astype(F32)
                eye = jnp.eye(HG_HEADS, dtype=F32)
                st0 = jnp.einsum('bhde,hg->bhegd', s0, eye).reshape(batch, HG_WIDTH, HG_WIDTH)
            o, st = _hgrn(q, k, lf, iv, sg, st0, prm['hgrn_onorm'][a][None, :], batch, seq)
            st5 = st.reshape(batch, HG_HEADS, HG_DK, HG_HEADS, HG_DK)
            s_new = jnp.stack([st5[:, hh, :, hh, :] for hh in range(HG_HEADS)], axis=1).transpose(0, 1, 3, 2)
            cm = _cmix(vn, gu, prm['cm_ws'][a], prm['cm_bs'][a].T, batch, seq)
            h = _out_proj(h, o, cm, w['w_out_a'][a])
            outs.setdefault('hg', []).append(s_new)
            outs.setdefault('cm', []).append(vn.reshape(batch, seq, CM_WIDTH))
        else:
            c = i // 2
            qu, qr, mq, ng, rows, win, *flat = _proj_c(h, gmix, w['w_in_c'][c], pos, seq, want_flat=ctx is None)
            rows3 = rows.reshape(batch, seq, ROW_W)
            win3 = win.reshape(batch, seq, 2 * _C_KV)
            wbig, pex, w2big = w['cmp'][c]
            if ctx is None:
                assert seq % KEY_TILE == 0
                nseg = seq // NSA_CMP_STRIDE
                kcvc = _compress(flat[0].reshape(batch, nseg, -1), wbig, pex, w2big)
                o_n, o_m = _mixer_c_attention(qu, qr, mq, ng, rows3, win3, kcvc, batch, seq)
                new_win = win3[:, seq - min(NSA_WINDOW, seq):]
                kv_rows = flat[1].reshape(batch, KV_ROW_HEADS, _HD, seq).transpose(0, 3, 1, 2)
            else:
                o_n, o_m = _decode_attention(qu, qr, mq, ng, rows3, win3, ctx['cache_kv'], ctx['page_table'],
                                             ctx['cache_win'][c], c, wbig, pex, w2big, batch, seq)
                new_win = win3
                kv_rows = rows3.reshape(batch, seq, KV_ROW_HEADS, _HD)
            h = _out_proj(h, o_n, o_m, w['w_out_c'][c])
            outs.setdefault('kv', []).append(kv_rows)
            outs.setdefault('win', []).append(new_win.reshape(batch, -1, 2 * NSA_KV, _HD))
        y = _moe(h, prm['norm_ffn'][i][None, :], w['router_w'][i], w['router_b'][i],
                 w['moe_w1'], w['moe_w3'], w['moe_w2'], i)
        h = _ple(h, y, p[i].reshape(t, -1), prm['norm_ple'][i][None, :], w['ple_gate_w'][i], w['ple_w'][i],
                 prm['final_norm'][None, :], final=(i == depth - 1))
    return h.reshape(batch, seq, d), outs


def kernel(x_prompt, x_sample, cache_kv, cache_win, state_hgrn, page_table, p_prompt, p_sample,
           norm_mix, norm_ffn, norm_ple, final_norm, w_in_a, w_out_a, hgrn_lb, hgrn_onorm, cm_vnorm,
           cm_ws, cm_bs, w_in_c, w_out_c, cmp_pe_k, cmp_w1_k, cmp_w2_k, cmp_pe_v, cmp_w1_v, cmp_w2_v,
           router_c_w, router_c_b, router_f_w, router_f_b, moe_w1, moe_w3, moe_w2, ple_w, ple_gate_w):
    prm = dict(norm_mix=norm_mix, norm_ffn=norm_ffn, norm_ple=norm_ple, final_norm=final_norm,
               w_in_a=w_in_a, w_out_a=w_out_a, hgrn_lb=hgrn_lb, hgrn_onorm=hgrn_onorm, cm_vnorm=cm_vnorm,
               cm_ws=cm_ws, cm_bs=cm_bs, w_in_c=w_in_c, w_out_c=w_out_c, cmp_pe_k=cmp_pe_k,
               cmp_w1_k=cmp_w1_k, cmp_w2_k=cmp_w2_k, cmp_pe_v=cmp_pe_v, cmp_w1_v=cmp_w1_v, cmp_w2_v=cmp_w2_v,
               router_c_w=router_c_w, router_c_b=router_c_b, router_f_w=router_f_w, router_f_b=router_f_b,
               moe_w1=moe_w1, moe_w3=moe_w3, moe_w2=moe_w2, ple_w=ple_w, ple_gate_w=ple_gate_w)
    w = _prep_weights(prm)
    past = page_table.shape[1] * PAGE_SIZE
    y_p, o_p = _forward(x_prompt, p_prompt, jnp.arange(x_prompt.shape[1]), prm, w, None)
    ctx = dict(cache_kv=cache_kv, cache_win=cache_win, state_hgrn=state_hgrn, page_table=page_table)
    y_s, o_s = _forward(x_sample, p_sample, past + jnp.arange(x_sample.shape[1]), prm, w, ctx)
    return (y_p, y_s,
            jnp.stack(o_p['kv'], axis=1), jnp.stack(o_s['kv'], axis=1),
            jnp.stack(o_p['win'], axis=0), jnp.stack(o_s['win'], axis=0),
            jnp.stack(o_p['hg'], axis=0), jnp.stack(o_s['hg'], axis=0),
            jnp.stack(o_s['cm'], axis=0))
```
